```python
import jax, jax.numpy as jnp
from jax import lax
import numpy as np

D_MODEL = 1024
BATCH = 8
SEQ = 4096
DEPTH = 1

CHUNK = 64
D_INNER = 2 * D_MODEL
SSD_HEAD_DIM = 64
SSD_HEADS = D_INNER // SSD_HEAD_DIM
SSD_GROUPS = 8
SSD_STATE = 128
SSD_CONV = 4
SSD_CONV_DIM = D_INNER + 2 * SSD_GROUPS * SSD_STATE
CONF_DIM = D_MODEL
CONF_KERNEL = 31
PEER_HEADS = 8
PEER_N_KEYS = 128
PEER_EXPERTS = PEER_N_KEYS * PEER_N_KEYS
PEER_TOPK = 16
PEER_KEY_DIM = 256
PEER_HALF = PEER_KEY_DIM // 2
PEER_TOKEN_BLOCK = 128
PLE_DIM = 256
COL_Z = D_INNER
COL_XBC = COL_Z + SSD_CONV_DIM
COL_DT = COL_XBC + SSD_HEADS
COL_GLU = COL_DT + 2 * CONF_DIM
COL_GA = COL_GLU + D_MODEL
IN_COLS = COL_GA + D_MODEL
EPS = 1e-6

kernel_name = "hybrid_ssd_conformer_peer_block"


def rms_norm(x, w):
    x32 = x.astype(jnp.float32)
    y = x32 * lax.rsqrt(jnp.mean(x32 * x32, axis=-1, keepdims=True) + EPS)
    return (y * w.astype(jnp.float32)).astype(x.dtype)


def layer_norm(x, w, b):
    x32 = x.astype(jnp.float32)
    mu = jnp.mean(x32, axis=-1, keepdims=True)
    xc = x32 - mu
    y = xc * lax.rsqrt(jnp.mean(xc * xc, axis=-1, keepdims=True) + EPS)
    return (y * w.astype(jnp.float32) + b.astype(jnp.float32)).astype(x.dtype)


def gated_group_rms_norm(y, z, w):
    yz = (y * jax.nn.silu(z)).astype(jnp.float32)
    shp = yz.shape
    yg = yz.reshape(shp[:-1] + (SSD_GROUPS, shp[-1] // SSD_GROUPS))
    yg = yg * lax.rsqrt(jnp.mean(yg * yg, axis=-1, keepdims=True) + EPS)
    return (yg.reshape(shp) * w.astype(jnp.float32)).astype(y.dtype)


def causal_depthwise_conv(x, w, b):
    width, ch = w.shape
    y = lax.conv_general_dilated(x, w[:, None, :].astype(x.dtype), window_strides=(1,),
                                 padding=[(width - 1, 0)],
                                 dimension_numbers=("NWC", "WIO", "NWC"),
                                 feature_group_count=ch)
    return y + b.astype(x.dtype)


def ssd_chunked_scan(xh, dt, a, bm, cm):
    f32 = jnp.float32
    bsz, s, h, pdim = xh.shape
    g, n = bm.shape[2], bm.shape[3]
    r = h // g
    nc = s // CHUNK
    x_dt = (xh.astype(f32) * dt[..., None]).reshape(bsz, nc, CHUNK, g, r, pdim)
    log_a = (dt * a).reshape(bsz, nc, CHUNK, g, r)
    bc_all = bm.astype(f32).reshape(bsz, nc, CHUNK, g, n)
    cc_all = cm.astype(f32).reshape(bsz, nc, CHUNK, g, n)
    xs = tuple(jnp.moveaxis(t, 1, 0) for t in (x_dt, log_a, bc_all, cc_all))
    causal = jnp.tril(jnp.ones((CHUNK, CHUNK), dtype=bool))[None, :, :, None, None]

    def step(state, inp):
        xc, lac, bc, cc = inp
        acum = jnp.cumsum(lac, axis=1)
        seg = acum[:, :, None] - acum[:, None, :]
        lmat = jnp.exp(jnp.where(causal, seg, -jnp.inf))
        cb = jnp.einsum("blgn,bsgn->blsg", cc, bc)
        y_diag = jnp.einsum("blsg,blsgr,bsgrp->blgrp", cb, lmat, xc)
        y_off = jnp.einsum("blgn,bgrpn->blgrp", cc, state) * jnp.exp(acum)[..., None]
        decay = jnp.exp(acum[:, -1:] - acum)
        new_state = (state * jnp.exp(acum[:, -1])[..., None, None]
                     + jnp.einsum("bsgn,bsgr,bsgrp->bgrpn", bc, decay, xc))
        return new_state, y_diag + y_off

    state0 = jnp.zeros((bsz, g, r, pdim, n), f32)
    _, ys = lax.scan(step, state0, xs)
    return jnp.moveaxis(ys, 0, 1).reshape(bsz, s, h, pdim)


def peer_mixer(xn, wq, keys, u_tab, v_tab):
    bsz, s, d = xn.shape
    blocks = xn.reshape(-1, PEER_TOKEN_BLOCK, d)

    def block(xb):
        t = xb.shape[0]
        q = (xb @ wq).reshape(t, PEER_HEADS, 2, PEER_HALF)
        scores = jnp.einsum("thcd,hckd->thck", q, keys.astype(xb.dtype))
        sv, si = lax.top_k(scores, PEER_TOPK)
        cand = (sv[:, :, 0, :, None] + sv[:, :, 1, None, :]).reshape(t, PEER_HEADS, PEER_TOPK * PEER_TOPK)
        best, j = lax.top_k(cand, PEER_TOPK)
        i1 = jnp.take_along_axis(si[:, :, 0], j // PEER_TOPK, axis=-1)
        i2 = jnp.take_along_axis(si[:, :, 1], j % PEER_TOPK, axis=-1)
        expert = i1 * PEER_N_KEYS + i2
        gate = jax.nn.softmax(best.astype(jnp.float32), axis=-1).astype(xb.dtype)
        u = u_tab[expert]
        act = jax.nn.gelu(jnp.einsum("thkd,td->thk", u, xb), approximate=False)
        v = v_tab[expert]
        return jnp.einsum("thk,thkd->td", gate * act, v)

    out = lax.map(block, blocks)
    return out.reshape(bsz, s, d)


def setup_inputs(seed: int = 0) -> dict:
    key = jax.random.key(seed)
    ks = jax.random.split(key, 32)
    f32 = jnp.float32
    nrm = lambda k, shape, scale: jax.random.normal(k, shape, f32) * scale
    L = DEPTH
    dt0 = jnp.exp(jax.random.uniform(ks[5], (L, SSD_HEADS), f32,
                                     minval=float(np.log(1e-3)), maxval=float(np.log(1e-1))))
    return {
        "x": nrm(ks[0], (BATCH, SEQ, D_MODEL), 1.0),
        "p": nrm(ks[1], (DEPTH, BATCH, SEQ, PLE_DIM), 1.0),
        "norm_mix_w": 1.0 + nrm(ks[2], (L, D_MODEL), 0.02),
        "w_in": nrm(ks[3], (L, D_MODEL, IN_COLS), D_MODEL ** -0.5),
        "conv_ssd_w": nrm(ks[4], (L, SSD_CONV, SSD_CONV_DIM), SSD_CONV ** -0.5),
        "conv_ssd_b": nrm(ks[6], (L, SSD_CONV_DIM), 0.02),
        "dt_bias": dt0 + jnp.log(-jnp.expm1(-dt0)),
        "a_log": jnp.log(jax.random.uniform(ks[7], (L, SSD_HEADS), f32, minval=1.0, maxval=16.0)),
        "d_skip": 1.0 + nrm(ks[8], (L, SSD_HEADS), 0.02),
        "ssd_norm_w": 1.0 + nrm(ks[9], (L, D_INNER), 0.02),
        "w_ssd_out": nrm(ks[10], (L, D_INNER, D_MODEL), D_INNER ** -0.5),
        "conv_dw_w": nrm(ks[11], (L, CONF_KERNEL, CONF_DIM), CONF_KERNEL ** -0.5),
        "conv_dw_b": nrm(ks[12], (L, CONF_DIM), 0.02),
        "conv_ln_w": 1.0 + nrm(ks[13], (L, CONF_DIM), 0.02),
        "conv_ln_b": nrm(ks[14], (L, CONF_DIM), 0.02),
        "w_conv_out": nrm(ks[15], (L, CONF_DIM, D_MODEL), CONF_DIM ** -0.5),
        "b_conv_out": nrm(ks[16], (L, D_MODEL), 0.02),
        "w_o": nrm(ks[17], (L, D_MODEL, D_MODEL), D_MODEL ** -0.5),
        "norm_ffn_w": 1.0 + nrm(ks[18], (L, D_MODEL), 0.02),
        "peer_wq": nrm(ks[19], (L, D_MODEL, PEER_HEADS * PEER_KEY_DIM), D_MODEL ** -0.5),
        "peer_keys": nrm(ks[20], (L, PEER_HEADS, 2, PEER_N_KEYS, PEER_HALF), PEER_HALF ** -0.5),
        "peer_u": nrm(ks[21], (L, PEER_EXPERTS, D_MODEL), D_MODEL ** -0.5),
        "peer_v": nrm(ks[22], (L, PEER_EXPERTS, D_MODEL), PEER_HEADS ** -0.5),
        "norm_ple_w": 1.0 + nrm(ks[23], (L, D_MODEL), 0.02),
        "w_ple_gate": nrm(ks[24], (L, D_MODEL, D_MODEL), D_MODEL ** -0.5),
        "w_ple_proj": nrm(ks[25], (L, PLE_DIM, D_MODEL), PLE_DIM ** -0.5),
        "final_norm_w": 1.0 + nrm(ks[26], (D_MODEL,), 0.02),
    }


def reference(x, p, norm_mix_w, w_in, conv_ssd_w, conv_ssd_b, dt_bias, a_log, d_skip,
              ssd_norm_w, w_ssd_out, conv_dw_w, conv_dw_b, conv_ln_w, conv_ln_b,
              w_conv_out, b_conv_out, w_o, norm_ffn_w, peer_wq, peer_keys, peer_u, peer_v,
              norm_ple_w, w_ple_gate, w_ple_proj, final_norm_w):
    f32 = jnp.float32
    bsz, s, _ = x.shape
    h = x
    for i in range(DEPTH):
        hn = rms_norm(h, norm_mix_w[i])
        proj = hn @ w_in[i]
        z = proj[..., :COL_Z]
        xbc = proj[..., COL_Z:COL_XBC]
        dt_raw = proj[..., COL_XBC:COL_DT]
        glu = proj[..., COL_DT:COL_GLU]
        gate_a = proj[..., COL_GLU:COL_GA]
        gate_b = proj[..., COL_GA:]

        xbc = jax.nn.silu(causal_depthwise_conv(xbc, conv_ssd_w[i], conv_ssd_b[i]))
        x_ssm = xbc[..., :D_INNER].reshape(bsz, s, SSD_HEADS, SSD_HEAD_DIM)
        b_ssm = xbc[..., D_INNER:D_INNER + SSD_GROUPS * SSD_STATE].reshape(bsz, s, SSD_GROUPS, SSD_STATE)
        c_ssm = xbc[..., D_INNER + SSD_GROUPS * SSD_STATE:].reshape(bsz, s, SSD_GROUPS, SSD_STATE)
        dt = jax.nn.softplus(dt_raw.astype(f32) + dt_bias[i].astype(f32))
        a = -jnp.exp(a_log[i].astype(f32))
        y_ssm = ssd_chunked_scan(x_ssm, dt, a, b_ssm, c_ssm)
        y_ssm = y_ssm + d_skip[i].astype(f32)[:, None] * x_ssm.astype(f32)
        y_ssm = y_ssm.reshape(bsz, s, D_INNER).astype(x.dtype)
        y_a = gated_group_rms_norm(y_ssm, z, ssd_norm_w[i]) @ w_ssd_out[i]

        u = glu[..., :CONF_DIM] * jax.nn.sigmoid(glu[..., CONF_DIM:])
        u = causal_depthwise_conv(u, conv_dw_w[i], conv_dw_b[i])
        u = jax.nn.silu(layer_norm(u, conv_ln_w[i], conv_ln_b[i]))
        y_b = u @ w_conv_out[i] + b_conv_out[i]

        merged = jax.nn.sigmoid(gate_a) * y_a + jax.nn.sigmoid(gate_b) * y_b
        h = h + merged @ w_o[i]

        h = h + peer_mixer(rms_norm(h, norm_ffn_w[i]), peer_wq[i], peer_keys[i], peer_u[i], peer_v[i])

        ple_gate = jax.nn.sigmoid(rms_norm(h, norm_ple_w[i]) @ w_ple_gate[i])
        h = h + ple_gate * (p[i] @ w_ple_proj[i])
    return rms_norm(h, final_norm_w)
```

```python
import functools
import jax
import jax.numpy as jnp
from jax import lax
import numpy as np
from jax.experimental import pallas as pl
from jax.experimental.pallas import tpu as pltpu

D_MODEL = 1024
CHUNK = 64
D_INNER = 2 * D_MODEL
SSD_HEAD_DIM = 64
SSD_HEADS = D_INNER // SSD_HEAD_DIM
SSD_GROUPS = 8
SSD_STATE = 128
SSD_CONV = 4
SSD_CONV_DIM = D_INNER + 2 * SSD_GROUPS * SSD_STATE
CONF_DIM = D_MODEL
CONF_KERNEL = 31
PEER_HEADS = 8
PEER_N_KEYS = 128
PEER_TOPK = 16
PEER_KEY_DIM = 256
PEER_HALF = PEER_KEY_DIM // 2
PEER_TOKEN_BLOCK = 128
COL_Z = D_INNER
COL_XBC = COL_Z + SSD_CONV_DIM
COL_DT = COL_XBC + SSD_HEADS
COL_GLU = COL_DT + 2 * CONF_DIM
COL_GA = COL_GLU + D_MODEL
EPS = 1e-6


def rms_norm(x, w):
    y = x * lax.rsqrt(jnp.mean(x * x, axis=-1, keepdims=True) + EPS)
    return y * w


def layer_norm(x, w, b):
    mu = jnp.mean(x, axis=-1, keepdims=True)
    xc = x - mu
    y = xc * lax.rsqrt(jnp.mean(xc * xc, axis=-1, keepdims=True) + EPS)
    return y * w + b


def gated_group_rms_norm(y, z, w):
    yz = y * jax.nn.silu(z)
    shp = yz.shape
    yg = yz.reshape(shp[:-1] + (SSD_GROUPS, shp[-1] // SSD_GROUPS))
    yg = yg * lax.rsqrt(jnp.mean(yg * yg, axis=-1, keepdims=True) + EPS)
    return yg.reshape(shp) * w


def causal_depthwise_conv(x, w, b):
    width, ch = w.shape
    y = lax.conv_general_dilated(x, w[:, None, :], window_strides=(1,),
                                 padding=[(width - 1, 0)],
                                 dimension_numbers=("NWC", "WIO", "NWC"),
                                 feature_group_count=ch)
    return y + b


def ssd_chunked_scan(xh, dt, a, bm, cm):
    f32 = jnp.float32
    bsz, s, h, pdim = xh.shape
    g, n = bm.shape[2], bm.shape[3]
    r = h // g
    nc = s // CHUNK
    x_dt = (xh * dt[..., None]).reshape(bsz, nc, CHUNK, g, r, pdim)
    log_a = (dt * a).reshape(bsz, nc, CHUNK, g, r)
    bc_all = bm.reshape(bsz, nc, CHUNK, g, n)
    cc_all = cm.reshape(bsz, nc, CHUNK, g, n)
    xs = tuple(jnp.moveaxis(t, 1, 0) for t in (x_dt, log_a, bc_all, cc_all))
    causal = jnp.tril(jnp.ones((CHUNK, CHUNK), dtype=bool))[None, :, :, None, None]

    def step(state, inp):
        xc, lac, bc, cc = inp
        acum = jnp.cumsum(lac, axis=1)
        seg = acum[:, :, None] - acum[:, None, :]
        lmat = jnp.exp(jnp.where(causal, seg, -jnp.inf))
        cb = jnp.einsum("blgn,bsgn->blsg", cc, bc)
        y_diag = jnp.einsum("blsg,blsgr,bsgrp->blgrp", cb, lmat, xc)
        y_off = jnp.einsum("blgn,bgrpn->blgrp", cc, state) * jnp.exp(acum)[..., None]
        decay = jnp.exp(acum[:, -1:] - acum)
        new_state = (state * jnp.exp(acum[:, -1])[..., None, None]
                     + jnp.einsum("bsgn,bsgr,bsgrp->bgrpn", bc, decay, xc))
        return new_state, y_diag + y_off

    state0 = jnp.zeros((bsz, g, r, pdim, n), f32)
    _, ys = lax.scan(step, state0, xs)
    return jnp.moveaxis(ys, 0, 1).reshape(bsz, s, h, pdim)


def peer_mixer(xn, wq, keys, u_tab, v_tab):
    bsz, s, d = xn.shape
    blocks = xn.reshape(-1, PEER_TOKEN_BLOCK, d)

    def block(xb):
        t = xb.shape[0]
        q = (xb @ wq).reshape(t, PEER_HEADS, 2, PEER_HALF)
        scores = jnp.einsum("thcd,hckd->thck", q, keys)
        sv, si = lax.top_k(scores, PEER_TOPK)
        cand = (sv[:, :, 0, :, None] + sv[:, :, 1, None, :]).reshape(t, PEER_HEADS, PEER_TOPK * PEER_TOPK)
        best, j = lax.top_k(cand, PEER_TOPK)
        i1 = jnp.take_along_axis(si[:, :, 0], j // PEER_TOPK, axis=-1)
        i2 = jnp.take_along_axis(si[:, :, 1], j % PEER_TOPK, axis=-1)
        expert = i1 * PEER_N_KEYS + i2
        gate = jax.nn.softmax(best, axis=-1)
        u = u_tab[expert]
        act = jax.nn.gelu(jnp.einsum("thkd,td->thk", u, xb), approximate=False)
        v = v_tab[expert]
        return jnp.einsum("thk,thkd->td", gate * act, v)

    out = lax.map(block, blocks)
    return out.reshape(bsz, s, d)


def _final_norm_body(h_ref, w_ref, o_ref):
    h = h_ref[...]
    o_ref[...] = h * lax.rsqrt(jnp.mean(h * h, axis=-1, keepdims=True) + EPS) * w_ref[...]


def final_norm(h2d, w):
    n, d = h2d.shape
    tm = 1024
    return pl.pallas_call(
        _final_norm_body,
        grid=(n // tm,),
        in_specs=[pl.BlockSpec((tm, d), lambda i: (i, 0)),
                  pl.BlockSpec((1, d), lambda i: (0, 0))],
        out_specs=pl.BlockSpec((tm, d), lambda i: (i, 0)),
        out_shape=jax.ShapeDtypeStruct((n, d), jnp.float32),
        name="final_norm",
    )(h2d, w.reshape(1, d))


def kernel(x, p, norm_mix_w, w_in, conv_ssd_w, conv_ssd_b, dt_bias, a_log, d_skip,
           ssd_norm_w, w_ssd_out, conv_dw_w, conv_dw_b, conv_ln_w, conv_ln_b,
           w_conv_out, b_conv_out, w_o, norm_ffn_w, peer_wq, peer_keys, peer_u, peer_v,
           norm_ple_w, w_ple_gate, w_ple_proj, final_norm_w):
    bsz, s, _ = x.shape
    h = x
    i = 0
    hn = rms_norm(h, norm_mix_w[i])
    proj = hn @ w_in[i]
    z = proj[..., :COL_Z]
    xbc = proj[..., COL_Z:COL_XBC]
    dt_raw = proj[..., COL_XBC:COL_DT]
    glu = proj[..., COL_DT:COL_GLU]
    gate_a = proj[..., COL_GLU:COL_GA]
    gate_b = proj[..., COL_GA:]

    xbc = jax.nn.silu(causal_depthwise_conv(xbc, conv_ssd_w[i], conv_ssd_b[i]))
    x_ssm = xbc[..., :D_INNER].reshape(bsz, s, SSD_HEADS, SSD_HEAD_DIM)
    b_ssm = xbc[..., D_INNER:D_INNER + SSD_GROUPS * SSD_STATE].reshape(bsz, s, SSD_GROUPS, SSD_STATE)
    c_ssm = xbc[..., D_INNER + SSD_GROUPS * SSD_STATE:].reshape(bsz, s, SSD_GROUPS, SSD_STATE)
    dt = jax.nn.softplus(dt_raw + dt_bias[i])
    a = -jnp.exp(a_log[i])
    y_ssm = ssd_chunked_scan(x_ssm, dt, a, b_ssm, c_ssm)
    y_ssm = y_ssm + d_skip[i][:, None] * x_ssm
    y_ssm = y_ssm.reshape(bsz, s, D_INNER)
    y_a = gated_group_rms_norm(y_ssm, z, ssd_norm_w[i]) @ w_ssd_out[i]

    u = glu[..., :CONF_DIM] * jax.nn.sigmoid(glu[..., CONF_DIM:])
    u = causal_depthwise_conv(u, conv_dw_w[i], conv_dw_b[i])
    u = jax.nn.silu(layer_norm(u, conv_ln_w[i], conv_ln_b[i]))
    y_b = u @ w_conv_out[i] + b_conv_out[i]

    merged = jax.nn.sigmoid(gate_a) * y_a + jax.nn.sigmoid(gate_b) * y_b
    h = h + merged @ w_o[i]

    h = h + peer_mixer(rms_norm(h, norm_ffn_w[i]), peer_wq[i], peer_keys[i], peer_u[i], peer_v[i])

    ple_gate = jax.nn.sigmoid(rms_norm(h, norm_ple_w[i]) @ w_ple_gate[i])
    h = h + ple_gate * (p[i] @ w_ple_proj[i])
    return final_norm(h.reshape(bsz * s, D_MODEL), final_norm_w).reshape(bsz, s, D_MODEL)
```

```python
import functools
import jax
import jax.numpy as jnp
from jax import lax
import numpy as np
from jax.experimental import pallas as pl
from jax.experimental.pallas import tpu as pltpu

D_MODEL = 1024
CHUNK = 64
D_INNER = 2 * D_MODEL
SSD_HEAD_DIM = 64
SSD_HEADS = D_INNER // SSD_HEAD_DIM
SSD_GROUPS = 8
SSD_STATE = 128
SSD_CONV = 4
SSD_CONV_DIM = D_INNER + 2 * SSD_GROUPS * SSD_STATE
CONF_DIM = D_MODEL
CONF_KERNEL = 31
PEER_HEADS = 8
PEER_N_KEYS = 128
PEER_TOPK = 16
PEER_KEY_DIM = 256
PEER_HALF = PEER_KEY_DIM // 2
COL_Z = D_INNER
COL_XBC = COL_Z + SSD_CONV_DIM
COL_DT = COL_XBC + SSD_HEADS
COL_GLU = COL_DT + 2 * CONF_DIM
COL_GA = COL_GLU + D_MODEL
EPS = 1e-6
F32 = jnp.float32
BF16 = jnp.bfloat16


def rms_norm(x, w):
    y = x * lax.rsqrt(jnp.mean(x * x, axis=-1, keepdims=True) + EPS)
    return y * w


def layer_norm(x, w, b):
    mu = jnp.mean(x, axis=-1, keepdims=True)
    xc = x - mu
    y = xc * lax.rsqrt(jnp.mean(xc * xc, axis=-1, keepdims=True) + EPS)
    return y * w + b


def gated_group_rms_norm(y, z, w):
    yz = y * jax.nn.silu(z)
    shp = yz.shape
    yg = yz.reshape(shp[:-1] + (SSD_GROUPS, shp[-1] // SSD_GROUPS))
    yg = yg * lax.rsqrt(jnp.mean(yg * yg, axis=-1, keepdims=True) + EPS)
    return yg.reshape(shp) * w


def causal_depthwise_conv(x, w, b):
    width, ch = w.shape
    y = lax.conv_general_dilated(x, w[:, None, :], window_strides=(1,),
                                 padding=[(width - 1, 0)],
                                 dimension_numbers=("NWC", "WIO", "NWC"),
                                 feature_group_count=ch)
    return y + b


def ssd_chunked_scan(xh, dt, a, bm, cm):
    f32 = jnp.float32
    bsz, s, h, pdim = xh.shape
    g, n = bm.shape[2], bm.shape[3]
    r = h // g
    nc = s // CHUNK
    x_dt = (xh * dt[..., None]).reshape(bsz, nc, CHUNK, g, r, pdim)
    log_a = (dt * a).reshape(bsz, nc, CHUNK, g, r)
    bc_all = bm.reshape(bsz, nc, CHUNK, g, n)
    cc_all = cm.reshape(bsz, nc, CHUNK, g, n)
    xs = tuple(jnp.moveaxis(t, 1, 0) for t in (x_dt, log_a, bc_all, cc_all))
    causal = jnp.tril(jnp.ones((CHUNK, CHUNK), dtype=bool))[None, :, :, None, None]

    def step(state, inp):
        xc, lac, bc, cc = inp
        acum = jnp.cumsum(lac, axis=1)
        seg = acum[:, :, None] - acum[:, None, :]
        lmat = jnp.exp(jnp.where(causal, seg, -jnp.inf))
        cb = jnp.einsum("blgn,bsgn->blsg", cc, bc)
        y_diag = jnp.einsum("blsg,blsgr,bsgrp->blgrp", cb, lmat, xc)
        y_off = jnp.einsum("blgn,bgrpn->blgrp", cc, state) * jnp.exp(acum)[..., None]
        decay = jnp.exp(acum[:, -1:] - acum)
        new_state = (state * jnp.exp(acum[:, -1])[..., None, None]
                     + jnp.einsum("bsgn,bsgr,bsgrp->bgrpn", bc, decay, xc))
        return new_state, y_diag + y_off

    state0 = jnp.zeros((bsz, g, r, pdim, n), f32)
    _, ys = lax.scan(step, state0, xs)
    return jnp.moveaxis(ys, 0, 1).reshape(bsz, s, h, pdim)


NK = PEER_N_KEYS
TOPK = PEER_TOPK
PEER_TILE = 256
PEER_EXPERT_BLOCK = 1024
NEG = float("-inf")


def _top16_rows(s, nrows):
    iota = lax.broadcasted_iota(jnp.int32, s.shape, 0)
    vals, idxs = [], []
    for _ in range(TOPK):
        m = jnp.max(s, axis=0, keepdims=True)
        idx = jnp.min(jnp.where(s == m, iota, nrows), axis=0, keepdims=True)
        vals.append(m)
        idxs.append(idx)
        s = jnp.where(iota == idx, NEG, s)
    return jnp.concatenate(vals, axis=0), jnp.concatenate(idxs, axis=0)


def _route_body(h_ref, nw_ref, wqt_ref, keys_ref, xn_ref, e_ref, g_ref, qt_scr, e_scr, g_scr, *, heads):
    h = h_ref[...]
    xn = h * lax.rsqrt(jnp.mean(h * h, axis=-1, keepdims=True) + EPS) * nw_ref[...]
    xnb = xn.astype(BF16)
    xn_ref[...] = xnb
    qt_scr[...] = lax.dot_general(wqt_ref[...], xnb, (((1,), (1,)), ((), ())),
                                  preferred_element_type=F32)
    t = h.shape[0]
    iota8 = lax.broadcasted_iota(jnp.int32, (8, t), 0)

    def head(hd, carry):
        sv, si = [], []
        for c in range(2):
            row0 = pl.multiple_of((hd * 2 + c) * NK, NK)
            q = qt_scr[pl.ds(row0, NK), :].astype(BF16)
            s = jnp.dot(keys_ref[hd * 2 + c], q, preferred_element_type=F32)
            v, i = _top16_rows(s, NK)
            sv.append(v)
            si.append(i)
        sv0, sv1 = sv
        si0, si1 = si
        pv = [sv0[0:1] + sv1, sv0[1:2] + sv1[0:8]]
        pe = [si0[0:1] * NK + si1, si0[1:2] * NK + si1[0:8]]
        for p, n in ((2, 5), (3, 4), (4, 3), (5, 2), (6, 2), (7, 2)):
            pv.append(jnp.where(iota8 < n, sv0[p:p + 1] + sv1[0:8], NEG))
            pe.append(si0[p:p + 1] * NK + si1[0:8])
        pv.append(sv0[8:16] + sv1[0:1])
        pe.append(si0[8:16] * NK + si1[0:1])
        cand = jnp.concatenate(pv, axis=0)
        cande = jnp.concatenate(pe, axis=0)
        nrows = cand.shape[0]
        iota = lax.broadcasted_iota(jnp.int32, cand.shape, 0)
        best, experts = [], []
        for _ in range(TOPK):
            m = jnp.max(cand, axis=0, keepdims=True)
            idx = jnp.min(jnp.where(cand == m, iota, nrows), axis=0, keepdims=True)
            sel = iota == idx
            experts.append(jnp.sum(jnp.where(sel, cande, 0), axis=0, keepdims=True))
            best.append(m)
            cand = jnp.where(sel, NEG, cand)
        best = jnp.concatenate(best, axis=0)
        ex = jnp.exp(best - best[0:1])
        gate = ex / jnp.sum(ex, axis=0, keepdims=True)
        r0 = pl.multiple_of(hd * TOPK, TOPK)
        e_scr[pl.ds(r0, TOPK), :] = jnp.concatenate(experts, axis=0)
        g_scr[pl.ds(r0, TOPK), :] = gate
        return carry

    lax.fori_loop(0, heads, head, 0)
    e_ref[...] = e_scr[...].T
    g_ref[...] = g_scr[...].T


def peer_route(h2d, norm_w, wqt, keys_hc, *, tile):
    n, d = h2d.shape
    heads = keys_hc.shape[0] // 2
    nsel = heads * TOPK
    return pl.pallas_call(
        functools.partial(_route_body, heads=heads),
        grid=(n // tile,),
        in_specs=[pl.BlockSpec((tile, d), lambda i: (i, 0)),
                  pl.BlockSpec((1, d), lambda i: (0, 0)),
                  pl.BlockSpec(wqt.shape, lambda i: (0, 0)),
                  pl.BlockSpec(keys_hc.shape, lambda i: (0, 0, 0))],
        out_specs=[pl.BlockSpec((tile, d), lambda i: (i, 0)),
                   pl.BlockSpec((tile, nsel), lambda i: (i, 0)),
                   pl.BlockSpec((tile, nsel), lambda i: (i, 0))],
        out_shape=[jax.ShapeDtypeStruct((n, d), BF16),
                   jax.ShapeDtypeStruct((n, nsel), jnp.int32),
                   jax.ShapeDtypeStruct((n, nsel), F32)],
        scratch_shapes=[pltpu.VMEM((wqt.shape[0], tile), F32),
                        pltpu.VMEM((nsel, tile), jnp.int32),
                        pltpu.VMEM((nsel, tile), F32)],
        compiler_params=pltpu.CompilerParams(dimension_semantics=("arbitrary",),
                                             vmem_limit_bytes=48 * 1024 * 1024),
        name="peer_route",
    )(h2d, norm_w.reshape(1, d), wqt, keys_hc)


def _gelu(x):
    return 0.5 * x * (1.0 + lax.erf(x * np.float32(0.7071067811865476)))


def _retrieve_body(h_ref, x_ref, e_ref, g_ref, ut_ref, v_ref, o_ref, act_scr, w3_scr, *, nblk, tile):
    k = pl.program_id(1)
    eb = ut_ref.shape[1]
    cpb = eb // NK
    nsel = e_ref.shape[1]

    @pl.when(k == 0)
    def _():
        act_scr[...] = jnp.zeros_like(act_scr)

    @pl.when(k < nblk)
    def _():
        scores = jnp.dot(x_ref[...], ut_ref[...], preferred_element_type=F32)
        e = e_ref[...]
        row = e >> 7
        col = e & (NK - 1)
        act = act_scr[...]
        for cc in range(cpb):
            picked = jnp.take_along_axis(scores[:, cc * NK:(cc + 1) * NK], col, axis=1)
            act = jnp.where(row == k * cpb + cc, picked, act)
        act_scr[...] = act

    @pl.when(k == nblk)
    def _():
        act_scr[...] = g_ref[...] * _gelu(act_scr[...])
        iota = lax.broadcasted_iota(jnp.int32, (NK, nsel), 0)

        def tok(t, carry):
            e_row = e_ref[pl.ds(t, 1), :]
            w_row = act_scr[pl.ds(t, 1), :]
            pm = jnp.where(iota == (e_row >> 7), 1.0, 0.0).astype(BF16)
            qm = jnp.where(iota == (e_row & (NK - 1)), w_row, 0.0).astype(BF16)
            w_t = lax.dot_general(pm, qm, (((1,), (1,)), ((), ())), preferred_element_type=F32)
            w3_scr[pl.ds(t, NK, stride=tile), :] = w_t
            return carry

        lax.fori_loop(0, tile, tok, 0)

    def weights_times_values():
        kk = k - nblk
        parts = []
        for cc in range(cpb):
            r0 = pl.multiple_of((kk * cpb + cc) * tile, tile)
            parts.append(w3_scr[pl.ds(r0, tile), :].astype(BF16))
        return jnp.dot(jnp.concatenate(parts, axis=1), v_ref[...], preferred_element_type=F32)

    @pl.when(k == nblk)
    def _():
        o_ref[...] = h_ref[...] + weights_times_values()

    @pl.when(k > nblk)
    def _():
        o_ref[...] += weights_times_values()


def peer_retrieve(h2d, xn, experts, gates, ut, v, *, tile, eb):
    n, d = h2d.shape
    nsel = experts.shape[1]
    ne = ut.shape[1]
    nblk = ne // eb
    return pl.pallas_call(
        functools.partial(_retrieve_body, nblk=nblk, tile=tile),
        grid=(n // tile, 2 * nblk),
        in_specs=[pl.BlockSpec((tile, d), lambda i, k: (i, 0)),
                  pl.BlockSpec((tile, d), lambda i, k: (i, 0)),
                  pl.BlockSpec((tile, nsel), lambda i, k: (i, 0)),
                  pl.BlockSpec((tile, nsel), lambda i, k: (i, 0)),
                  pl.BlockSpec((d, eb), lambda i, k: (0, jnp.minimum(k, nblk - 1))),
                  pl.BlockSpec((eb, d), lambda i, k: (jnp.maximum(k - nblk, 0), 0))],
        out_specs=pl.BlockSpec((tile, d), lambda i, k: (i, 0)),
        out_shape=jax.ShapeDtypeStruct((n, d), F32),
        scratch_shapes=[pltpu.VMEM((tile, nsel), F32),
                        pltpu.VMEM((ne // NK * tile, NK), F32)],
        compiler_params=pltpu.CompilerParams(dimension_semantics=("arbitrary", "arbitrary"),
                                             vmem_limit_bytes=56 * 1024 * 1024),
        name="peer_retrieve",
    )(h2d, xn, experts, gates, ut, v)


def _final_norm_body(h_ref, w_ref, o_ref):
    h = h_ref[...]
    o_ref[...] = h * lax.rsqrt(jnp.mean(h * h, axis=-1, keepdims=True) + EPS) * w_ref[...]


def final_norm(h2d, w):
    n, d = h2d.shape
    tm = 1024
    return pl.pallas_call(
        _final_norm_body,
        grid=(n // tm,),
        in_specs=[pl.BlockSpec((tm, d), lambda i: (i, 0)),
                  pl.BlockSpec((1, d), lambda i: (0, 0))],
        out_specs=pl.BlockSpec((tm, d), lambda i: (i, 0)),
        out_shape=jax.ShapeDtypeStruct((n, d), jnp.float32),
        name="final_norm",
    )(h2d, w.reshape(1, d))


def kernel(x, p, norm_mix_w, w_in, conv_ssd_w, conv_ssd_b, dt_bias, a_log, d_skip,
           ssd_norm_w, w_ssd_out, conv_dw_w, conv_dw_b, conv_ln_w, conv_ln_b,
           w_conv_out, b_conv_out, w_o, norm_ffn_w, peer_wq, peer_keys, peer_u, peer_v,
           norm_ple_w, w_ple_gate, w_ple_proj, final_norm_w):
    bsz, s, _ = x.shape
    h = x
    i = 0
    hn = rms_norm(h, norm_mix_w[i])
    proj = hn @ w_in[i]
    z = proj[..., :COL_Z]
    xbc = proj[..., COL_Z:COL_XBC]
    dt_raw = proj[..., COL_XBC:COL_DT]
    glu = proj[..., COL_DT:COL_GLU]
    gate_a = proj[..., COL_GLU:COL_GA]
    gate_b = proj[..., COL_GA:]

    xbc = jax.nn.silu(causal_depthwise_conv(xbc, conv_ssd_w[i], conv_ssd_b[i]))
    x_ssm = xbc[..., :D_INNER].reshape(bsz, s, SSD_HEADS, SSD_HEAD_DIM)
    b_ssm = xbc[..., D_INNER:D_INNER + SSD_GROUPS * SSD_STATE].reshape(bsz, s, SSD_GROUPS, SSD_STATE)
    c_ssm = xbc[..., D_INNER + SSD_GROUPS * SSD_STATE:].reshape(bsz, s, SSD_GROUPS, SSD_STATE)
    dt = jax.nn.softplus(dt_raw + dt_bias[i])
    a = -jnp.exp(a_log[i])
    y_ssm = ssd_chunked_scan(x_ssm, dt, a, b_ssm, c_ssm)
    y_ssm = y_ssm + d_skip[i][:, None] * x_ssm
    y_ssm = y_ssm.reshape(bsz, s, D_INNER)
    y_a = gated_group_rms_norm(y_ssm, z, ssd_norm_w[i]) @ w_ssd_out[i]

    u = glu[..., :CONF_DIM] * jax.nn.sigmoid(glu[..., CONF_DIM:])
    u = causal_depthwise_conv(u, conv_dw_w[i], conv_dw_b[i])
    u = jax.nn.silu(layer_norm(u, conv_ln_w[i], conv_ln_b[i]))
    y_b = u @ w_conv_out[i] + b_conv_out[i]

    merged = jax.nn.sigmoid(gate_a) * y_a + jax.nn.sigmoid(gate_b) * y_b
    h = h + merged @ w_o[i]

    h2d = h.reshape(bsz * s, D_MODEL)
    wqt = peer_wq[i].T.astype(BF16)
    keys_hc = peer_keys[i].reshape(PEER_HEADS * 2, PEER_N_KEYS, PEER_HALF).astype(BF16)
    xn, experts, gates = peer_route(h2d, norm_ffn_w[i], wqt, keys_hc, tile=PEER_TILE)
    h2d = peer_retrieve(h2d, xn, experts, gates, peer_u[i].T.astype(BF16), peer_v[i].astype(BF16),
                        tile=PEER_TILE, eb=PEER_EXPERT_BLOCK)
    h = h2d.reshape(bsz, s, D_MODEL)

    ple_gate = jax.nn.sigmoid(rms_norm(h, norm_ple_w[i]) @ w_ple_gate[i])
    h = h + ple_gate * (p[i] @ w_ple_proj[i])
    return final_norm(h.reshape(bsz * s, D_MODEL), final_norm_w).reshape(bsz, s, D_MODEL)
```

```python
import functools
import jax
import jax.numpy as jnp
from jax import lax
import numpy as np
from jax.experimental import pallas as pl
from jax.experimental.pallas import tpu as pltpu

D_MODEL = 1024
CHUNK = 64
D_INNER = 2 * D_MODEL
SSD_HEAD_DIM = 64
SSD_HEADS = D_INNER // SSD_HEAD_DIM
SSD_GROUPS = 8
SSD_STATE = 128
SSD_CONV = 4
SSD_CONV_DIM = D_INNER + 2 * SSD_GROUPS * SSD_STATE
CONF_DIM = D_MODEL
CONF_KERNEL = 31
PEER_HEADS = 8
PEER_N_KEYS = 128
PEER_TOPK = 16
PEER_KEY_DIM = 256
PEER_HALF = PEER_KEY_DIM // 2
COL_Z = D_INNER
COL_XBC = COL_Z + SSD_CONV_DIM
COL_DT = COL_XBC + SSD_HEADS
COL_GLU = COL_DT + 2 * CONF_DIM
COL_GA = COL_GLU + D_MODEL
EPS = 1e-6
F32 = jnp.float32
BF16 = jnp.bfloat16


def rms_norm(x, w):
    y = x * lax.rsqrt(jnp.mean(x * x, axis=-1, keepdims=True) + EPS)
    return y * w


def layer_norm(x, w, b):
    mu = jnp.mean(x, axis=-1, keepdims=True)
    xc = x - mu
    y = xc * lax.rsqrt(jnp.mean(xc * xc, axis=-1, keepdims=True) + EPS)
    return y * w + b


def gated_group_rms_norm(y, z, w):
    yz = y * jax.nn.silu(z)
    shp = yz.shape
    yg = yz.reshape(shp[:-1] + (SSD_GROUPS, shp[-1] // SSD_GROUPS))
    yg = yg * lax.rsqrt(jnp.mean(yg * yg, axis=-1, keepdims=True) + EPS)
    return yg.reshape(shp) * w


def causal_depthwise_conv(x, w, b):
    width, ch = w.shape
    y = lax.conv_general_dilated(x, w[:, None, :], window_strides=(1,),
                                 padding=[(width - 1, 0)],
                                 dimension_numbers=("NWC", "WIO", "NWC"),
                                 feature_group_count=ch)
    return y + b


def ssd_chunked_scan(xh, dt, a, bm, cm):
    f32 = jnp.float32
    bsz, s, h, pdim = xh.shape
    g, n = bm.shape[2], bm.shape[3]
    r = h // g
    nc = s // CHUNK
    x_dt = (xh * dt[..., None]).reshape(bsz, nc, CHUNK, g, r, pdim)
    log_a = (dt * a).reshape(bsz, nc, CHUNK, g, r)
    bc_all = bm.reshape(bsz, nc, CHUNK, g, n)
    cc_all = cm.reshape(bsz, nc, CHUNK, g, n)
    xs = tuple(jnp.moveaxis(t, 1, 0) for t in (x_dt, log_a, bc_all, cc_all))
    causal = jnp.tril(jnp.ones((CHUNK, CHUNK), dtype=bool))[None, :, :, None, None]

    def step(state, inp):
        xc, lac, bc, cc = inp
        acum = jnp.cumsum(lac, axis=1)
        seg = acum[:, :, None] - acum[:, None, :]
        lmat = jnp.exp(jnp.where(causal, seg, -jnp.inf))
        cb = jnp.einsum("blgn,bsgn->blsg", cc, bc)
        y_diag = jnp.einsum("blsg,blsgr,bsgrp->blgrp", cb, lmat, xc)
        y_off = jnp.einsum("blgn,bgrpn->blgrp", cc, state) * jnp.exp(acum)[..., None]
        decay = jnp.exp(acum[:, -1:] - acum)
        new_state = (state * jnp.exp(acum[:, -1])[..., None, None]
                     + jnp.einsum("bsgn,bsgr,bsgrp->bgrpn", bc, decay, xc))
        return new_state, y_diag + y_off

    state0 = jnp.zeros((bsz, g, r, pdim, n), f32)
    _, ys = lax.scan(step, state0, xs)
    return jnp.moveaxis(ys, 0, 1).reshape(bsz, s, h, pdim)


NK = PEER_N_KEYS
TOPK = PEER_TOPK
PEER_TILE = 256
PEER_EXPERT_BLOCK = 1024
NEG = float("-inf")


def _top16_rows(s, nrows):
    iota = lax.broadcasted_iota(jnp.int32, s.shape, 0)
    vals, idxs = [], []
    for _ in range(TOPK):
        m = jnp.max(s, axis=0, keepdims=True)
        idx = jnp.min(jnp.where(s == m, iota, nrows), axis=0, keepdims=True)
        vals.append(m)
        idxs.append(idx)
        s = jnp.where(iota == idx, NEG, s)
    return jnp.concatenate(vals, axis=0), jnp.concatenate(idxs, axis=0)


def _route_body(h_ref, nw_ref, wqt_ref, keys_ref, xn_ref, e_ref, g_ref, qt_scr, e_scr, g_scr, *, heads):
    h = h_ref[...]
    xn = h * lax.rsqrt(jnp.mean(h * h, axis=-1, keepdims=True) + EPS) * nw_ref[...]
    xnb = xn.astype(BF16)
    xn_ref[...] = xnb
    qt_scr[...] = lax.dot_general(wqt_ref[...], xnb, (((1,), (1,)), ((), ())),
                                  preferred_element_type=F32)
    t = h.shape[0]
    iota8 = lax.broadcasted_iota(jnp.int32, (8, t), 0)

    def head(hd, carry):
        sv, si = [], []
        for c in range(2):
            row0 = pl.multiple_of((hd * 2 + c) * NK, NK)
            q = qt_scr[pl.ds(row0, NK), :].astype(BF16)
            s = jnp.dot(keys_ref[hd * 2 + c], q, preferred_element_type=F32)
            v, i = _top16_rows(s, NK)
            sv.append(v)
            si.append(i)
        sv0, sv1 = sv
        si0, si1 = si
        pv = [sv0[0:1] + sv1, sv0[1:2] + sv1[0:8]]
        pe = [si0[0:1] * NK + si1, si0[1:2] * NK + si1[0:8]]
        for p, n in ((2, 5), (3, 4), (4, 3), (5, 2), (6, 2), (7, 2)):
            pv.append(jnp.where(iota8 < n, sv0[p:p + 1] + sv1[0:8], NEG))
            pe.append(si0[p:p + 1] * NK + si1[0:8])
        pv.append(sv0[8:16] + sv1[0:1])
        pe.append(si0[8:16] * NK + si1[0:1])
        cand = jnp.concatenate(pv, axis=0)
        cande = jnp.concatenate(pe, axis=0)
        nrows = cand.shape[0]
        iota = lax.broadcasted_iota(jnp.int32, cand.shape, 0)
        best, experts = [], []
        for _ in range(TOPK):
            m = jnp.max(cand, axis=0, keepdims=True)
            idx = jnp.min(jnp.where(cand == m, iota, nrows), axis=0, keepdims=True)
            sel = iota == idx
            experts.append(jnp.sum(jnp.where(sel, cande, 0), axis=0, keepdims=True))
            best.append(m)
            cand = jnp.where(sel, NEG, cand)
        best = jnp.concatenate(best, axis=0)
        ex = jnp.exp(best - best[0:1])
        gate = ex / jnp.sum(ex, axis=0, keepdims=True)
        r0 = pl.multiple_of(hd * TOPK, TOPK)
        e_scr[pl.ds(r0, TOPK), :] = jnp.concatenate(experts, axis=0)
        g_scr[pl.ds(r0, TOPK), :] = gate
        return carry

    lax.fori_loop(0, heads, head, 0)
    e_ref[...] = e_scr[...].T
    g_ref[...] = g_scr[...].T


def peer_route(h2d, norm_w, wqt, keys_hc, *, tile):
    n, d = h2d.shape
    heads = keys_hc.shape[0] // 2
    nsel = heads * TOPK
    return pl.pallas_call(
        functools.partial(_route_body, heads=heads),
        grid=(n // tile,),
        in_specs=[pl.BlockSpec((tile, d), lambda i: (i, 0)),
                  pl.BlockSpec((1, d), lambda i: (0, 0)),
                  pl.BlockSpec(wqt.shape, lambda i: (0, 0)),
                  pl.BlockSpec(keys_hc.shape, lambda i: (0, 0, 0))],
        out_specs=[pl.BlockSpec((tile, d), lambda i: (i, 0)),
                   pl.BlockSpec((tile, nsel), lambda i: (i, 0)),
                   pl.BlockSpec((tile, nsel), lambda i: (i, 0))],
        out_shape=[jax.ShapeDtypeStruct((n, d), BF16),
                   jax.ShapeDtypeStruct((n, nsel), jnp.int32),
                   jax.ShapeDtypeStruct((n, nsel), F32)],
        scratch_shapes=[pltpu.VMEM((wqt.shape[0], tile), F32),
                        pltpu.VMEM((nsel, tile), jnp.int32),
                        pltpu.VMEM((nsel, tile), F32)],
        compiler_params=pltpu.CompilerParams(dimension_semantics=("arbitrary",),
                                             vmem_limit_bytes=48 * 1024 * 1024),
        name="peer_route",
    )(h2d, norm_w.reshape(1, d), wqt, keys_hc)


def _gelu(x):
    return 0.5 * x * (1.0 + lax.erf(x * np.float32(0.7071067811865476)))


def _retrieve_body(h_ref, x_ref, e_ref, g_ref, ut_ref, v_ref, o_ref, act_scr, w3_scr, *, nblk, tile):
    k = pl.program_id(1)
    eb = ut_ref.shape[1]
    cpb = eb // NK
    nsel = e_ref.shape[1]

    @pl.when(k == 0)
    def _():
        act_scr[...] = jnp.zeros_like(act_scr)

    @pl.when(k < nblk)
    def _():
        x = x_ref[...]
        e = e_ref[...]
        row = e >> 7
        col = e & (NK - 1)
        act = act_scr[...]
        for c2 in range(cpb // 2):
            scores = jnp.dot(x, ut_ref[:, c2 * 2 * NK:(c2 + 1) * 2 * NK], preferred_element_type=F32)
            for half in range(2):
                picked = jnp.take_along_axis(scores[:, half * NK:(half + 1) * NK], col, axis=1)
                act = jnp.where(row == k * cpb + c2 * 2 + half, picked, act)
        act_scr[...] = act

    @pl.when(k == nblk)
    def _():
        act_scr[...] = g_ref[...] * _gelu(act_scr[...])
        iota = lax.broadcasted_iota(jnp.int32, (NK, nsel), 0)

        def tok8(t8, carry):
            t0 = pl.multiple_of(t8 * 8, 8)
            e_rows = e_ref[pl.ds(t0, 8), :]
            w_rows = act_scr[pl.ds(t0, 8), :]
            w_toks = []
            for j in range(8):
                e_row = e_rows[j:j + 1]
                pm = jnp.where(iota == (e_row >> 7), 1.0, 0.0).astype(BF16)
                qm = jnp.where(iota == (e_row & (NK - 1)), w_rows[j:j + 1], 0.0).astype(BF16)
                w_toks.append(lax.dot_general(pm, qm, (((1,), (1,)), ((), ())), preferred_element_type=F32))
            w3_scr[:, pl.ds(t0, 8), :] = jnp.swapaxes(jnp.stack(w_toks, axis=0), 0, 1)
            return carry

        lax.fori_loop(0, tile // 8, tok8, 0, unroll=2)

    def weights_times_values():
        kk = k - nblk
        parts = [w3_scr[kk * cpb + cc].astype(BF16) for cc in range(cpb)]
        return jnp.dot(jnp.concatenate(parts, axis=1), v_ref[...], preferred_element_type=F32)

    @pl.when(k == nblk)
    def _():
        o_ref[...] = h_ref[...] + weights_times_values()

    @pl.when(k > nblk)
    def _():
        o_ref[...] += weights_times_values()


def peer_retrieve(h2d, xn, experts, gates, ut, v, *, tile, eb):
    n, d = h2d.shape
    nsel = experts.shape[1]
    ne = ut.shape[1]
    nblk = ne // eb
    return pl.pallas_call(
        functools.partial(_retrieve_body, nblk=nblk, tile=tile),
        grid=(n // tile, 2 * nblk),
        in_specs=[pl.BlockSpec((tile, d), lambda i, k: (i, 0)),
                  pl.BlockSpec((tile, d), lambda i, k: (i, 0)),
                  pl.BlockSpec((tile, nsel), lambda i, k: (i, 0)),
                  pl.BlockSpec((tile, nsel), lambda i, k: (i, 0)),
                  pl.BlockSpec((d, eb), lambda i, k: (0, jnp.minimum(k, nblk - 1))),
                  pl.BlockSpec((eb, d), lambda i, k: (jnp.maximum(k - nblk, 0), 0))],
        out_specs=pl.BlockSpec((tile, d), lambda i, k: (i, 0)),
        out_shape=jax.ShapeDtypeStruct((n, d), F32),
        scratch_shapes=[pltpu.VMEM((tile, nsel), F32),
                        pltpu.VMEM((ne // NK, tile, NK), F32)],
        compiler_params=pltpu.CompilerParams(dimension_semantics=("arbitrary", "arbitrary"),
                                             vmem_limit_bytes=56 * 1024 * 1024),
        name="peer_retrieve",
    )(h2d, xn, experts, gates, ut, v)


def _final_norm_body(h_ref, w_ref, o_ref):
    h = h_ref[...]
    o_ref[...] = h * lax.rsqrt(jnp.mean(h * h, axis=-1, keepdims=True) + EPS) * w_ref[...]


def final_norm(h2d, w):
    n, d = h2d.shape
    tm = 1024
    return pl.pallas_call(
        _final_norm_body,
        grid=(n // tm,),
        in_specs=[pl.BlockSpec((tm, d), lambda i: (i, 0)),
                  pl.BlockSpec((1, d), lambda i: (0, 0))],
        out_specs=pl.BlockSpec((tm, d), lambda i: (i, 0)),
        out_shape=jax.ShapeDtypeStruct((n, d), jnp.float32),
        name="final_norm",
    )(h2d, w.reshape(1, d))


def kernel(x, p, norm_mix_w, w_in, conv_ssd_w, conv_ssd_b, dt_bias, a_log, d_skip,
           ssd_norm_w, w_ssd_out, conv_dw_w, conv_dw_b, conv_ln_w, conv_ln_b,
           w_conv_out, b_conv_out, w_o, norm_ffn_w, peer_wq, peer_keys, peer_u, peer_v,
           norm_ple_w, w_ple_gate, w_ple_proj, final_norm_w):
    bsz, s, _ = x.shape
    h = x
    i = 0
    hn = rms_norm(h, norm_mix_w[i])
    proj = hn @ w_in[i]
    z = proj[..., :COL_Z]
    xbc = proj[..., COL_Z:COL_XBC]
    dt_raw = proj[..., COL_XBC:COL_DT]
    glu = proj[..., COL_DT:COL_GLU]
    gate_a = proj[..., COL_GLU:COL_GA]
    gate_b = proj[..., COL_GA:]

    xbc = jax.nn.silu(causal_depthwise_conv(xbc, conv_ssd_w[i], conv_ssd_b[i]))
    x_ssm = xbc[..., :D_INNER].reshape(bsz, s, SSD_HEADS, SSD_HEAD_DIM)
    b_ssm = xbc[..., D_INNER:D_INNER + SSD_GROUPS * SSD_STATE].reshape(bsz, s, SSD_GROUPS, SSD_STATE)
    c_ssm = xbc[..., D_INNER + SSD_GROUPS * SSD_STATE:].reshape(bsz, s, SSD_GROUPS, SSD_STATE)
    dt = jax.nn.softplus(dt_raw + dt_bias[i])
    a = -jnp.exp(a_log[i])
    y_ssm = ssd_chunked_scan(x_ssm, dt, a, b_ssm, c_ssm)
    y_ssm = y_ssm + d_skip[i][:, None] * x_ssm
    y_ssm = y_ssm.reshape(bsz, s, D_INNER)
    y_a = gated_group_rms_norm(y_ssm, z, ssd_norm_w[i]) @ w_ssd_out[i]

    u = glu[..., :CONF_DIM] * jax.nn.sigmoid(glu[..., CONF_DIM:])
    u = causal_depthwise_conv(u, conv_dw_w[i], conv_dw_b[i])
    u = jax.nn.silu(layer_norm(u, conv_ln_w[i], conv_ln_b[i]))
    y_b = u @ w_conv_out[i] + b_conv_out[i]

    merged = jax.nn.sigmoid(gate_a) * y_a + jax.nn.sigmoid(gate_b) * y_b
    h = h + merged @ w_o[i]

    h2d = h.reshape(bsz * s, D_MODEL)
    wqt = peer_wq[i].T.astype(BF16)
    keys_hc = peer_keys[i].reshape(PEER_HEADS * 2, PEER_N_KEYS, PEER_HALF).astype(BF16)
    xn, experts, gates = peer_route(h2d, norm_ffn_w[i], wqt, keys_hc, tile=PEER_TILE)
    h2d = peer_retrieve(h2d, xn, experts, gates, peer_u[i].T.astype(BF16), peer_v[i].astype(BF16),
                        tile=PEER_TILE, eb=PEER_EXPERT_BLOCK)
    h = h2d.reshape(bsz, s, D_MODEL)

    ple_gate = jax.nn.sigmoid(rms_norm(h, norm_ple_w[i]) @ w_ple_gate[i])
    h = h + ple_gate * (p[i] @ w_ple_proj[i])
    return final_norm(h.reshape(bsz * s, D_MODEL), final_norm_w).reshape(bsz, s, D_MODEL)
```

```python
import functools
import jax
import jax.numpy as jnp
from jax import lax
import numpy as np
from jax.experimental import pallas as pl
from jax.experimental.pallas import tpu as pltpu

D_MODEL = 1024
D_INNER = 2 * D_MODEL
SSD_HEAD_DIM = 64
SSD_HEADS = D_INNER // SSD_HEAD_DIM
SSD_GROUPS = 8
SSD_STATE = 128
SSD_CONV = 4
HEADS_PER_GROUP = SSD_HEADS // SSD_GROUPS
GROUP_DIM = D_INNER // SSD_GROUPS
CONF_KERNEL = 31
PEER_HEADS = 8
PEER_N_KEYS = 128
PEER_TOPK = 16
PEER_HALF = 128
EPS = 1e-6
F32 = jnp.float32
BF16 = jnp.bfloat16
LANES = 128
HIGHEST = lax.Precision.HIGHEST

INPROJ_TM = 512
INPROJ_TN = 1024
MIX_L = 256
PEER_TILE = 256
PEER_EXPERT_BLOCK = 1024
PLE_TM = 512


def _sigmoid(x):
    return 1.0 / (1.0 + jnp.exp(-x))


def _silu(x):
    return x * _sigmoid(x)


def _rms(x, w):
    return x * lax.rsqrt(jnp.mean(x * x, axis=-1, keepdims=True) + EPS) * w


def _inproj_body(x_ref, nw_ref, w_ref, wdt_ref, bdt_ref, o_ref, dt_ref, hn_scr):
    @pl.when(pl.program_id(1) == 0)
    def _():
        hn = _rms(x_ref[...], nw_ref[...]).astype(BF16)
        hn_scr[...] = hn
        v = jnp.dot(hn, wdt_ref[...], preferred_element_type=F32) + bdt_ref[...]
        dt_ref[...] = jnp.maximum(v, 0.0) + jnp.log(1.0 + jnp.exp(-jnp.abs(v)))

    o_ref[...] = jnp.dot(hn_scr[...], w_ref[...], preferred_element_type=F32).astype(BF16)


def in_projection(x2d, norm_w, w_main, w_dt, b_dt, *, tm, tn):
    n, d = x2d.shape
    c = w_main.shape[1]
    return pl.pallas_call(
        _inproj_body,
        grid=(n // tm, c // tn),
        in_specs=[pl.BlockSpec((tm, d), lambda i, j: (i, 0)),
                  pl.BlockSpec((1, d), lambda i, j: (0, 0)),
                  pl.BlockSpec((d, tn), lambda i, j: (0, j)),
                  pl.BlockSpec((d, LANES), lambda i, j: (0, 0)),
                  pl.BlockSpec((1, LANES), lambda i, j: (0, 0))],
        out_specs=[pl.BlockSpec((tm, tn), lambda i, j: (i, j)),
                   pl.BlockSpec((tm, LANES), lambda i, j: (i, 0))],
        out_shape=[jax.ShapeDtypeStruct((n, c), BF16), jax.ShapeDtypeStruct((n, LANES), F32)],
        scratch_shapes=[pltpu.VMEM((tm, d), BF16)],
        compiler_params=pltpu.CompilerParams(dimension_semantics=("arbitrary", "arbitrary"),
                                             vmem_limit_bytes=40 * 1024 * 1024),
        name="in_projection",
    )(x2d, norm_w.reshape(1, d), w_main, w_dt, b_dt)


SSD_HALO = 8
CONF_HALO = 32
CONV_COLS = 512


def _mixer_body(x_ref, z_ref, xs_ref, b_ref, c_ref, glua_ref, glub_ref, ga_ref, gb_ref, dt_ref,
                cw_ref, cb_ref, a_ref, expand_ref, dsk_ref, nw_ref, wso_ref,
                cdw_ref, cdb_ref, lnw_ref, lnb_ref, wco_ref, bco_ref, wo_ref,
                h_ref,
                ext_scr, act_scr, uext_scr, ushift_scr, state_scr, y_scr, conv_scr, *, L):
    step = pl.program_id(1)

    @pl.when(step == 0)
    def _():
        ext_scr[0:SSD_HALO, :] = jnp.zeros((SSD_HALO, ext_scr.shape[1]), F32)
        uext_scr[0:CONF_HALO, :] = jnp.zeros((CONF_HALO, uext_scr.shape[1]), F32)
        state_scr[...] = jnp.zeros_like(state_scr)

    nx = xs_ref.shape[1]
    nb = b_ref.shape[1]
    ext_scr[SSD_HALO:SSD_HALO + L, 0:nx] = xs_ref[...].astype(F32)
    ext_scr[SSD_HALO:SSD_HALO + L, nx:nx + nb] = b_ref[...].astype(F32)
    ext_scr[SSD_HALO:SSD_HALO + L, nx + nb:nx + 2 * nb] = c_ref[...].astype(F32)
    for j in range(ext_scr.shape[1] // CONV_COLS):
        cs = slice(j * CONV_COLS, (j + 1) * CONV_COLS)
        acc = jnp.broadcast_to(cb_ref[:, cs], (L, CONV_COLS))
        for k in range(SSD_CONV):
            off = SSD_HALO - (SSD_CONV - 1) + k
            acc = acc + cw_ref[k:k + 1, cs] * ext_scr[off:off + L, cs]
        act_scr[:, cs] = _silu(acc)
    ext_scr[0:SSD_HALO, :] = ext_scr[L:L + SSD_HALO, :]

    dt = dt_ref[...]
    la = dt * a_ref[...]
    ri = lax.broadcasted_iota(jnp.int32, (L, L), 0)
    ci = lax.broadcasted_iota(jnp.int32, (L, L), 1)
    causal = ri >= ci
    tril = jnp.where(causal, 1.0, 0.0)
    triu = jnp.where(ri <= ci, 1.0, 0.0)
    acum = jnp.dot(tril, la, precision=HIGHEST, preferred_element_type=F32)
    acum_t = jnp.dot(la.T, triu, precision=HIGHEST, preferred_element_type=F32)
    acum_last = acum[L - 1:L, :]
    stacked = jnp.concatenate([dt, jnp.exp(acum_last - acum), jnp.exp(acum)], axis=0)
    s_hi = stacked.astype(BF16)
    s_lo = (stacked - s_hi.astype(F32)).astype(BF16)
    ex = (jnp.dot(s_hi, expand_ref[...], preferred_element_type=F32)
          + jnp.dot(s_lo, expand_ref[...], preferred_element_type=F32))
    dt_x = ex[0:L]
    dec_x = ex[L:2 * L]
    eac_x = ex[2 * L:3 * L]
    elast_x = eac_x[L - 1:L, :]

    for g in range(SSD_GROUPS):
        gs = slice(g * GROUP_DIM, (g + 1) * GROUP_DIM)
        bg = act_scr[:, nx + g * SSD_STATE:nx + (g + 1) * SSD_STATE]
        cg = act_scr[:, nx + nb + g * SSD_STATE:nx + nb + (g + 1) * SSD_STATE]
        bgb = bg.astype(BF16)
        cgb = cg.astype(BF16)
        cb = lax.dot_general(cgb, bgb, (((1,), (1,)), ((), ())), preferred_element_type=F32)
        xg = act_scr[:, gs]
        xdt = xg * dt_x[:, gs]
        xdtb = xdt.astype(BF16)
        yd = []
        for r in range(HEADS_PER_GROUP):
            hd = g * HEADS_PER_GROUP + r
            seg = acum[:, hd:hd + 1] - acum_t[hd:hd + 1, :]
            lm = jnp.exp(jnp.where(causal, seg, -jnp.inf))
            m = (cb * lm).astype(BF16)
            yd.append(jnp.dot(m, xdtb[:, r * SSD_HEAD_DIM:(r + 1) * SSD_HEAD_DIM], preferred_element_type=F32))
        st = state_scr[g]
        y = (jnp.concatenate(yd, axis=1)
             + jnp.dot(cgb, st.astype(BF16), preferred_element_type=F32) * eac_x[:, gs]
             + dsk_ref[:, gs] * xg)
        state_scr[g] = st * elast_x[:, gs] + jnp.dot(bg.T.astype(BF16), (xdt * dec_x[:, gs]).astype(BF16),
                                                    preferred_element_type=F32)
        yz = y * _silu(z_ref[:, gs].astype(F32))
        y_scr[:, gs] = _rms(yz, nw_ref[:, gs]).astype(BF16)
    y_a = jnp.dot(y_scr[...], wso_ref[...], preferred_element_type=F32)

    uext_scr[CONF_HALO:CONF_HALO + L, :] = glua_ref[...].astype(F32) * _sigmoid(glub_ref[...].astype(F32))
    for j in range(uext_scr.shape[1] // CONV_COLS):
        cs = slice(j * CONV_COLS, (j + 1) * CONV_COLS)
        for sft in range(1, 8):
            ushift_scr[sft - 1] = uext_scr[sft:sft + L + CONF_HALO - 8, cs]
        acc = jnp.broadcast_to(cdb_ref[:, cs], (L, CONV_COLS))
        for k in range(CONF_KERNEL):
            off = CONF_HALO - (CONF_KERNEL - 1) + k
            q8, sft = (off // 8) * 8, off % 8
            tap = uext_scr[q8:q8 + L, cs] if sft == 0 else ushift_scr[sft - 1, q8:q8 + L, :]
            acc = acc + cdw_ref[k:k + 1, cs] * tap
        conv_scr[:, cs] = acc
    uext_scr[0:CONF_HALO, :] = uext_scr[L:L + CONF_HALO, :]
    u = conv_scr[...]
    mu = jnp.mean(u, axis=-1, keepdims=True)
    uc = u - mu
    un = uc * lax.rsqrt(jnp.mean(uc * uc, axis=-1, keepdims=True) + EPS) * lnw_ref[...] + lnb_ref[...]
    y_b = jnp.dot(_silu(un).astype(BF16), wco_ref[...], preferred_element_type=F32) + bco_ref[...]

    merged = _sigmoid(ga_ref[...].astype(F32)) * y_a + _sigmoid(gb_ref[...].astype(F32)) * y_b
    h_ref[...] = x_ref[...] + jnp.dot(merged.astype(BF16), wo_ref[...], preferred_element_type=F32)


def token_mixers(x2d, proj, dt, conv_w, conv_b, a_pad, expand, dskip_x, ssd_norm_w, w_ssd_out,
                 conv_dw_w, conv_dw_b, ln_w, ln_b, w_conv_out, b_conv_out, w_o, *, batch, L):
    n, d = x2d.shape
    spb = n // batch // L
    row = lambda b, c: b * spb + c
    col = lambda k, w=1: pl.BlockSpec((L, w * d), lambda b, c, k=k: (row(b, c), k))
    full = lambda a: pl.BlockSpec(a.shape, lambda b, c: (0,) * a.ndim)
    consts = [conv_w, conv_b, a_pad, expand, dskip_x, ssd_norm_w, w_ssd_out,
              conv_dw_w, conv_dw_b, ln_w, ln_b, w_conv_out, b_conv_out, w_o]
    nxbc = conv_w.shape[1]
    return pl.pallas_call(
        functools.partial(_mixer_body, L=L),
        grid=(batch, spb),
        in_specs=[pl.BlockSpec((L, d), lambda b, c: (row(b, c), 0)),
                  col(0, 2),
                  col(1, 2),
                  col(4), col(5),
                  col(6), col(7),
                  col(8), col(9),
                  pl.BlockSpec((L, LANES), lambda b, c: (row(b, c), 0))] + [full(a) for a in consts],
        out_specs=pl.BlockSpec((L, d), lambda b, c: (row(b, c), 0)),
        out_shape=jax.ShapeDtypeStruct((n, d), F32),
        scratch_shapes=[pltpu.VMEM((L + SSD_HALO, nxbc), F32),
                        pltpu.VMEM((L, nxbc), F32),
                        pltpu.VMEM((L + CONF_HALO, d), F32),
                        pltpu.VMEM((7, L + CONF_HALO - 8, CONV_COLS), F32),
                        pltpu.VMEM((SSD_GROUPS, SSD_STATE, GROUP_DIM), F32),
                        pltpu.VMEM((L, D_INNER), BF16),
                        pltpu.VMEM((L, d), F32)],
        compiler_params=pltpu.CompilerParams(dimension_semantics=("arbitrary", "arbitrary"),
                                             vmem_limit_bytes=56 * 1024 * 1024),
        name="token_mixers",
    )(x2d, proj, proj, proj, proj, proj, proj, proj, proj, dt, *consts)


NK = PEER_N_KEYS
TOPK = PEER_TOPK
NEG = float("-inf")


def _top16_rows(s, nrows):
    iota = lax.broadcasted_iota(jnp.int32, s.shape, 0)
    vals, idxs = [], []
    for _ in range(TOPK):
        m = jnp.max(s, axis=0, keepdims=True)
        idx = jnp.min(jnp.where(s == m, iota, nrows), axis=0, keepdims=True)
        vals.append(m)
        idxs.append(idx)
        s = jnp.where(iota == idx, NEG, s)
    return jnp.concatenate(vals, axis=0), jnp.concatenate(idxs, axis=0)


def _route_body(h_ref, nw_ref, wqt_ref, keys_ref, xn_ref, e_ref, g_ref, qt_scr, e_scr, g_scr, *, heads):
    h = h_ref[...]
    xnb = _rms(h, nw_ref[...]).astype(BF16)
    xn_ref[...] = xnb
    qt_scr[...] = lax.dot_general(wqt_ref[...], xnb, (((1,), (1,)), ((), ())),
                                  preferred_element_type=F32)
    t = h.shape[0]
    iota8 = lax.broadcasted_iota(jnp.int32, (8, t), 0)

    def head(hd, carry):
        sv, si = [], []
        for c in range(2):
            row0 = pl.multiple_of((hd * 2 + c) * NK, NK)
            q = qt_scr[pl.ds(row0, NK), :].astype(BF16)
            s = jnp.dot(keys_ref[hd * 2 + c], q, preferred_element_type=F32)
            v, i = _top16_rows(s, NK)
            sv.append(v)
            si.append(i)
        sv0, sv1 = sv
        si0, si1 = si
        pv = [sv0[0:1] + sv1, sv0[1:2] + sv1[0:8]]
        pe = [si0[0:1] * NK + si1, si0[1:2] * NK + si1[0:8]]
        for p, n in ((2, 5), (3, 4), (4, 3), (5, 2), (6, 2), (7, 2)):
            pv.append(jnp.where(iota8 < n, sv0[p:p + 1] + sv1[0:8], NEG))
            pe.append(si0[p:p + 1] * NK + si1[0:8])
        pv.append(sv0[8:16] + sv1[0:1])
        pe.append(si0[8:16] * NK + si1[0:1])
        cand = jnp.concatenate(pv, axis=0)
        cande = jnp.concatenate(pe, axis=0)
        nrows = cand.shape[0]
        iota = lax.broadcasted_iota(jnp.int32, cand.shape, 0)
        best, experts = [], []
        for _ in range(TOPK):
            m = jnp.max(cand, axis=0, keepdims=True)
            idx = jnp.min(jnp.where(cand == m, iota, nrows), axis=0, keepdims=True)
            sel = iota == idx
            experts.append(jnp.sum(jnp.where(sel, cande, 0), axis=0, keepdims=True))
            best.append(m)
            cand = jnp.where(sel, NEG, cand)
        best = jnp.concatenate(best, axis=0)
        ex = jnp.exp(best - best[0:1])
        gate = ex / jnp.sum(ex, axis=0, keepdims=True)
        r0 = pl.multiple_of(hd * TOPK, TOPK)
        e_scr[pl.ds(r0, TOPK), :] = jnp.concatenate(experts, axis=0)
        g_scr[pl.ds(r0, TOPK), :] = gate
        return carry

    lax.fori_loop(0, heads, head, 0)
    e_ref[...] = e_scr[...].T
    g_ref[...] = g_scr[...].T


def peer_route(h2d, norm_w, wqt, keys_hc, *, tile):
    n, d = h2d.shape
    heads = keys_hc.shape[0] // 2
    nsel = heads * TOPK
    return pl.pallas_call(
        functools.partial(_route_body, heads=heads),
        grid=(n // tile,),
        in_specs=[pl.BlockSpec((tile, d), lambda i: (i, 0)),
                  pl.BlockSpec((1, d), lambda i: (0, 0)),
                  pl.BlockSpec(wqt.shape, lambda i: (0, 0)),
                  pl.BlockSpec(keys_hc.shape, lambda i: (0, 0, 0))],
        out_specs=[pl.BlockSpec((tile, d), lambda i: (i, 0)),
                   pl.BlockSpec((tile, nsel), lambda i: (i, 0)),
                   pl.BlockSpec((tile, nsel), lambda i: (i, 0))],
        out_shape=[jax.ShapeDtypeStruct((n, d), BF16),
                   jax.ShapeDtypeStruct((n, nsel), jnp.int32),
                   jax.ShapeDtypeStruct((n, nsel), F32)],
        scratch_shapes=[pltpu.VMEM((wqt.shape[0], tile), F32),
                        pltpu.VMEM((nsel, tile), jnp.int32),
                        pltpu.VMEM((nsel, tile), F32)],
        compiler_params=pltpu.CompilerParams(dimension_semantics=("arbitrary",),
                                             vmem_limit_bytes=48 * 1024 * 1024),
        name="peer_route",
    )(h2d, norm_w.reshape(1, d), wqt, keys_hc)


def _gelu(x):
    return 0.5 * x * (1.0 + lax.erf(x * np.float32(0.7071067811865476)))


def _retrieve_body(h_ref, x_ref, e_ref, g_ref, ut_ref, v_ref, o_ref, act_scr, w3_scr, *, nblk, tile):
    k = pl.program_id(1)
    eb = ut_ref.shape[1]
    cpb = eb // NK
    nsel = e_ref.shape[1]

    @pl.when(k == 0)
    def _():
        act_scr[...] = jnp.zeros_like(act_scr)

    @pl.when(k < nblk)
    def _():
        x = x_ref[...]
        e = e_ref[...]
        row = e >> 7
        col = e & (NK - 1)
        act = act_scr[...]
        for c2 in range(cpb // 2):
            scores = jnp.dot(x, ut_ref[:, c2 * 2 * NK:(c2 + 1) * 2 * NK], preferred_element_type=F32)
            for half in range(2):
                picked = jnp.take_along_axis(scores[:, half * NK:(half + 1) * NK], col, axis=1)
                act = jnp.where(row == k * cpb + c2 * 2 + half, picked, act)
        act_scr[...] = act

    @pl.when(k == nblk)
    def _():
        act_scr[...] = g_ref[...] * _gelu(act_scr[...])
        iota = lax.broadcasted_iota(jnp.int32, (NK, nsel), 0)

        def tok8(t8, carry):
            t0 = pl.multiple_of(t8 * 8, 8)
            e_rows = e_ref[pl.ds(t0, 8), :]
            w_rows = act_scr[pl.ds(t0, 8), :]
            w_toks = []
            for j in range(8):
                e_row = e_rows[j:j + 1]
                pm = jnp.where(iota == (e_row >> 7), 1.0, 0.0).astype(BF16)
                qm = jnp.where(iota == (e_row & (NK - 1)), w_rows[j:j + 1], 0.0).astype(BF16)
                w_toks.append(lax.dot_general(pm, qm, (((1,), (1,)), ((), ())), preferred_element_type=F32))
            w3_scr[:, pl.ds(t0, 8), :] = jnp.swapaxes(jnp.stack(w_toks, axis=0), 0, 1)
            return carry

        lax.fori_loop(0, tile // 8, tok8, 0, unroll=2)

    def weights_times_values():
        kk = k - nblk
        parts = [w3_scr[kk * cpb + cc].astype(BF16) for cc in range(cpb)]
        return jnp.dot(jnp.concatenate(parts, axis=1), v_ref[...], preferred_element_type=F32)

    @pl.when(k == nblk)
    def _():
        o_ref[...] = h_ref[...] + weights_times_values()

    @pl.when(k > nblk)
    def _():
        o_ref[...] += weights_times_values()


def peer_retrieve(h2d, xn, experts, gates, ut, v, *, tile, eb):
    n, d = h2d.shape
    nsel = experts.shape[1]
    ne = ut.shape[1]
    nblk = ne // eb
    return pl.pallas_call(
        functools.partial(_retrieve_body, nblk=nblk, tile=tile),
        grid=(n // tile, 2 * nblk),
        in_specs=[pl.BlockSpec((tile, d), lambda i, k: (i, 0)),
                  pl.BlockSpec((tile, d), lambda i, k: (i, 0)),
                  pl.BlockSpec((tile, nsel), lambda i, k: (i, 0)),
                  pl.BlockSpec((tile, nsel), lambda i, k: (i, 0)),
                  pl.BlockSpec((d, eb), lambda i, k: (0, jnp.minimum(k, nblk - 1))),
                  pl.BlockSpec((eb, d), lambda i, k: (jnp.maximum(k - nblk, 0), 0))],
        out_specs=pl.BlockSpec((tile, d), lambda i, k: (i, 0)),
        out_shape=jax.ShapeDtypeStruct((n, d), F32),
        scratch_shapes=[pltpu.VMEM((tile, nsel), F32),
                        pltpu.VMEM((ne // NK, tile, NK), F32)],
        compiler_params=pltpu.CompilerParams(dimension_semantics=("arbitrary", "arbitrary"),
                                             vmem_limit_bytes=56 * 1024 * 1024),
        name="peer_retrieve",
    )(h2d, xn, experts, gates, ut, v)


def _ple_body(h_ref, p_ref, nw_ref, wg_ref, wp_ref, fw_ref, o_ref):
    h = h_ref[...]
    gate = _sigmoid(jnp.dot(_rms(h, nw_ref[...]).astype(BF16), wg_ref[...], preferred_element_type=F32))
    h = h + gate * jnp.dot(p_ref[...].astype(BF16), wp_ref[...], preferred_element_type=F32)
    o_ref[...] = _rms(h, fw_ref[...])


def ple_final(h2d, p2d, norm_w, w_gate, w_proj, final_w, *, tm):
    n, d = h2d.shape
    pd = p2d.shape[1]
    return pl.pallas_call(
        _ple_body,
        grid=(n // tm,),
        in_specs=[pl.BlockSpec((tm, d), lambda i: (i, 0)),
                  pl.BlockSpec((tm, pd), lambda i: (i, 0)),
                  pl.BlockSpec((1, d), lambda i: (0, 0)),
                  pl.BlockSpec((d, d), lambda i: (0, 0)),
                  pl.BlockSpec((pd, d), lambda i: (0, 0)),
                  pl.BlockSpec((1, d), lambda i: (0, 0))],
        out_specs=pl.BlockSpec((tm, d), lambda i: (i, 0)),
        out_shape=jax.ShapeDtypeStruct((n, d), F32),
        compiler_params=pltpu.CompilerParams(dimension_semantics=("arbitrary",),
                                             vmem_limit_bytes=40 * 1024 * 1024),
        name="ple_final",
    )(h2d, p2d, norm_w.reshape(1, d), w_gate, w_proj, final_w.reshape(1, d))


def kernel(x, p, norm_mix_w, w_in, conv_ssd_w, conv_ssd_b, dt_bias, a_log, d_skip,
           ssd_norm_w, w_ssd_out, conv_dw_w, conv_dw_b, conv_ln_w, conv_ln_b,
           w_conv_out, b_conv_out, w_o, norm_ffn_w, peer_wq, peer_keys, peer_u, peer_v,
           norm_ple_w, w_ple_gate, w_ple_proj, final_norm_w):
    bsz, s, d = x.shape
    x2d = x.reshape(bsz * s, d)
    i = 0
    r1 = lambda v: v.reshape(1, -1)

    col_xbc = D_INNER + conv_ssd_w.shape[2]
    col_dt = col_xbc + SSD_HEADS
    w_main = jnp.concatenate([w_in[i][:, :col_xbc], w_in[i][:, col_dt:]], axis=1).astype(BF16)
    w_dt = jnp.pad(w_in[i][:, col_xbc:col_dt], ((0, 0), (0, LANES - SSD_HEADS))).astype(BF16)
    b_dt = jnp.pad(dt_bias[i], (0, LANES - SSD_HEADS)).reshape(1, LANES)
    a_pad = jnp.pad(-jnp.exp(a_log[i]), (0, LANES - SSD_HEADS)).reshape(1, LANES)
    dskip_x = jnp.repeat(d_skip[i], SSD_HEAD_DIM).reshape(1, D_INNER)
    expand = (jnp.arange(D_INNER)[None, :] // SSD_HEAD_DIM == jnp.arange(LANES)[:, None]).astype(BF16)

    proj, dt = in_projection(x2d, norm_mix_w[i], w_main, w_dt, b_dt, tm=INPROJ_TM, tn=INPROJ_TN)
    h2d = token_mixers(x2d, proj, dt, conv_ssd_w[i], r1(conv_ssd_b[i]), a_pad, expand, dskip_x, r1(ssd_norm_w[i]),
                       w_ssd_out[i].astype(BF16), conv_dw_w[i], r1(conv_dw_b[i]), r1(conv_ln_w[i]),
                       r1(conv_ln_b[i]), w_conv_out[i].astype(BF16), r1(b_conv_out[i]), w_o[i].astype(BF16),
                       batch=bsz, L=MIX_L)

    wqt = peer_wq[i].T.astype(BF16)
    keys_hc = peer_keys[i].reshape(PEER_HEADS * 2, PEER_N_KEYS, PEER_HALF).astype(BF16)
    xn, experts, gates = peer_route(h2d, norm_ffn_w[i], wqt, keys_hc, tile=PEER_TILE)
    h2d = peer_retrieve(h2d, xn, experts, gates, peer_u[i].T.astype(BF16), peer_v[i].astype(BF16),
                        tile=PEER_TILE, eb=PEER_EXPERT_BLOCK)

    out = ple_final(h2d, p[i].reshape(bsz * s, -1), norm_ple_w[i], w_ple_gate[i].astype(BF16),
                    w_ple_proj[i].astype(BF16), final_norm_w, tm=PLE_TM)
    return out.reshape(bsz, s, d)
```

```python
import functools
import jax
import jax.numpy as jnp
from jax import lax
import numpy as np
from jax.experimental import pallas as pl
from jax.experimental.pallas import tpu as pltpu

D_MODEL = 1024
D_INNER = 2 * D_MODEL
SSD_HEAD_DIM = 64
SSD_HEADS = D_INNER // SSD_HEAD_DIM
SSD_GROUPS = 8
SSD_STATE = 128
SSD_CONV = 4
HEADS_PER_GROUP = SSD_HEADS // SSD_GROUPS
GROUP_DIM = D_INNER // SSD_GROUPS
CONF_KERNEL = 31
PEER_HEADS = 8
PEER_N_KEYS = 128
PEER_TOPK = 16
PEER_HALF = 128
EPS = 1e-6
F32 = jnp.float32
BF16 = jnp.bfloat16
LANES = 128
HIGHEST = lax.Precision.HIGHEST

INPROJ_TM = 1024
INPROJ_TN = 1024
MIX_L = 256
PEER_TILE = 256
PEER_RETRIEVE_TILE = 512
PEER_EXPERT_BLOCK = 1024
PLE_TM = 512


def _sigmoid(x):
    return 1.0 / (1.0 + jnp.exp(-x))


def _silu(x):
    return x * _sigmoid(x)


def _rms(x, w):
    return x * lax.rsqrt(jnp.mean(x * x, axis=-1, keepdims=True) + EPS) * w


def _inproj_body(x_ref, nw_ref, w_ref, wdt_ref, bdt_ref, o_ref, dt_ref, hn_scr):
    @pl.when(pl.program_id(1) == 0)
    def _():
        hn = _rms(x_ref[...], nw_ref[...]).astype(BF16)
        hn_scr[...] = hn
        v = jnp.dot(hn, wdt_ref[...], preferred_element_type=F32) + bdt_ref[...]
        dt_ref[...] = jnp.maximum(v, 0.0) + jnp.log(1.0 + jnp.exp(-jnp.abs(v)))

    o_ref[...] = jnp.dot(hn_scr[...], w_ref[...], preferred_element_type=F32).astype(BF16)


def in_projection(x2d, norm_w, w_main, w_dt, b_dt, *, tm, tn):
    n, d = x2d.shape
    c = w_main.shape[1]
    return pl.pallas_call(
        _inproj_body,
        grid=(n // tm, c // tn),
        in_specs=[pl.BlockSpec((tm, d), lambda i, j: (i, 0)),
                  pl.BlockSpec((1, d), lambda i, j: (0, 0)),
                  pl.BlockSpec((d, tn), lambda i, j: (0, j)),
                  pl.BlockSpec((d, LANES), lambda i, j: (0, 0)),
                  pl.BlockSpec((1, LANES), lambda i, j: (0, 0))],
        out_specs=[pl.BlockSpec((tm, tn), lambda i, j: (i, j)),
                   pl.BlockSpec((tm, LANES), lambda i, j: (i, 0))],
        out_shape=[jax.ShapeDtypeStruct((n, c), BF16), jax.ShapeDtypeStruct((n, LANES), F32)],
        scratch_shapes=[pltpu.VMEM((tm, d), BF16)],
        compiler_params=pltpu.CompilerParams(dimension_semantics=("arbitrary", "arbitrary"),
                                             vmem_limit_bytes=40 * 1024 * 1024),
        name="in_projection",
    )(x2d, norm_w.reshape(1, d), w_main, w_dt, b_dt)


SSD_HALO = 8
CONF_HALO = 32
CONV_COLS = 512


def _mixer_body(x_ref, z_ref, xs_ref, b_ref, c_ref, glua_ref, glub_ref, ga_ref, gb_ref, dt_ref,
                cw_ref, cb_ref, a_ref, expand_ref, dsk_ref, nw_ref, wso_ref,
                cdw_ref, cdb_ref, lnw_ref, lnb_ref, wco_ref, bco_ref, wo_ref,
                h_ref,
                ext_scr, act_scr, uext_scr, ushift_scr, state_scr, y_scr, conv_scr, *, L):
    step = pl.program_id(1)

    @pl.when(step == 0)
    def _():
        ext_scr[0:SSD_HALO, :] = jnp.zeros((SSD_HALO, ext_scr.shape[1]), F32)
        uext_scr[0:CONF_HALO, :] = jnp.zeros((CONF_HALO, uext_scr.shape[1]), F32)
        state_scr[...] = jnp.zeros_like(state_scr)

    nx = xs_ref.shape[1]
    nb = b_ref.shape[1]
    ext_scr[SSD_HALO:SSD_HALO + L, 0:nx] = xs_ref[...].astype(F32)
    ext_scr[SSD_HALO:SSD_HALO + L, nx:nx + nb] = b_ref[...].astype(F32)
    ext_scr[SSD_HALO:SSD_HALO + L, nx + nb:nx + 2 * nb] = c_ref[...].astype(F32)
    for j in range(ext_scr.shape[1] // CONV_COLS):
        cs = slice(j * CONV_COLS, (j + 1) * CONV_COLS)
        acc = jnp.broadcast_to(cb_ref[:, cs], (L, CONV_COLS))
        for k in range(SSD_CONV):
            off = SSD_HALO - (SSD_CONV - 1) + k
            acc = acc + cw_ref[k:k + 1, cs] * ext_scr[off:off + L, cs]
        act_scr[:, cs] = _silu(acc)
    ext_scr[0:SSD_HALO, :] = ext_scr[L:L + SSD_HALO, :]

    dt = dt_ref[...]
    la = dt * a_ref[...]
    ri = lax.broadcasted_iota(jnp.int32, (L, L), 0)
    ci = lax.broadcasted_iota(jnp.int32, (L, L), 1)
    causal = ri >= ci
    tril = jnp.where(causal, 1.0, 0.0)
    triu = jnp.where(ri <= ci, 1.0, 0.0)
    acum = jnp.dot(tril, la, precision=HIGHEST, preferred_element_type=F32)
    acum_t = jnp.dot(la.T, triu, precision=HIGHEST, preferred_element_type=F32)
    acum_last = acum[L - 1:L, :]
    stacked = jnp.concatenate([dt, jnp.exp(acum_last - acum), jnp.exp(acum)], axis=0)
    s_hi = stacked.astype(BF16)
    s_lo = (stacked - s_hi.astype(F32)).astype(BF16)
    ex = (jnp.dot(s_hi, expand_ref[...], preferred_element_type=F32)
          + jnp.dot(s_lo, expand_ref[...], preferred_element_type=F32))
    dt_x = ex[0:L]
    dec_x = ex[L:2 * L]
    eac_x = ex[2 * L:3 * L]
    elast_x = eac_x[L - 1:L, :]

    for g in range(SSD_GROUPS):
        gs = slice(g * GROUP_DIM, (g + 1) * GROUP_DIM)
        bg = act_scr[:, nx + g * SSD_STATE:nx + (g + 1) * SSD_STATE]
        cg = act_scr[:, nx + nb + g * SSD_STATE:nx + nb + (g + 1) * SSD_STATE]
        bgb = bg.astype(BF16)
        cgb = cg.astype(BF16)
        cb = lax.dot_general(cgb, bgb, (((1,), (1,)), ((), ())), preferred_element_type=F32)
        xg = act_scr[:, gs]
        xdt = xg * dt_x[:, gs]
        xdtb = xdt.astype(BF16)
        yd = []
        for r in range(HEADS_PER_GROUP):
            hd = g * HEADS_PER_GROUP + r
            seg = acum[:, hd:hd + 1] - acum_t[hd:hd + 1, :]
            lm = jnp.exp(jnp.where(causal, seg, -jnp.inf))
            m = (cb * lm).astype(BF16)
            yd.append(jnp.dot(m, xdtb[:, r * SSD_HEAD_DIM:(r + 1) * SSD_HEAD_DIM], preferred_element_type=F32))
        st = state_scr[g]
        y = (jnp.concatenate(yd, axis=1)
             + jnp.dot(cgb, st.astype(BF16), preferred_element_type=F32) * eac_x[:, gs]
             + dsk_ref[:, gs] * xg)
        state_scr[g] = st * elast_x[:, gs] + jnp.dot(bg.T.astype(BF16), (xdt * dec_x[:, gs]).astype(BF16),
                                                    preferred_element_type=F32)
        yz = y * _silu(z_ref[:, gs].astype(F32))
        y_scr[:, gs] = _rms(yz, nw_ref[:, gs]).astype(BF16)
    y_a = jnp.dot(y_scr[...], wso_ref[...], preferred_element_type=F32)

    uext_scr[CONF_HALO:CONF_HALO + L, :] = glua_ref[...].astype(F32) * _sigmoid(glub_ref[...].astype(F32))
    for j in range(uext_scr.shape[1] // CONV_COLS):
        cs = slice(j * CONV_COLS, (j + 1) * CONV_COLS)
        for sft in range(1, 8):
            ushift_scr[sft - 1] = uext_scr[sft:sft + L + CONF_HALO - 8, cs]
        acc = jnp.broadcast_to(cdb_ref[:, cs], (L, CONV_COLS))
        for k in range(CONF_KERNEL):
            off = CONF_HALO - (CONF_KERNEL - 1) + k
            q8, sft = (off // 8) * 8, off % 8
            tap = uext_scr[q8:q8 + L, cs] if sft == 0 else ushift_scr[sft - 1, q8:q8 + L, :]
            acc = acc + cdw_ref[k:k + 1, cs] * tap
        conv_scr[:, cs] = acc
    uext_scr[0:CONF_HALO, :] = uext_scr[L:L + CONF_HALO, :]
    u = conv_scr[...]
    mu = jnp.mean(u, axis=-1, keepdims=True)
    uc = u - mu
    un = uc * lax.rsqrt(jnp.mean(uc * uc, axis=-1, keepdims=True) + EPS) * lnw_ref[...] + lnb_ref[...]
    y_b = jnp.dot(_silu(un).astype(BF16), wco_ref[...], preferred_element_type=F32) + bco_ref[...]

    merged = _sigmoid(ga_ref[...].astype(F32)) * y_a + _sigmoid(gb_ref[...].astype(F32)) * y_b
    h_ref[...] = x_ref[...] + jnp.dot(merged.astype(BF16), wo_ref[...], preferred_element_type=F32)


def token_mixers(x2d, proj, dt, conv_w, conv_b, a_pad, expand, dskip_x, ssd_norm_w, w_ssd_out,
                 conv_dw_w, conv_dw_b, ln_w, ln_b, w_conv_out, b_conv_out, w_o, *, batch, L):
    n, d = x2d.shape
    spb = n // batch // L
    row = lambda b, c: b * spb + c
    col = lambda k, w=1: pl.BlockSpec((L, w * d), lambda b, c, k=k: (row(b, c), k))
    full = lambda a: pl.BlockSpec(a.shape, lambda b, c: (0,) * a.ndim)
    consts = [conv_w, conv_b, a_pad, expand, dskip_x, ssd_norm_w, w_ssd_out,
              conv_dw_w, conv_dw_b, ln_w, ln_b, w_conv_out, b_conv_out, w_o]
    nxbc = conv_w.shape[1]
    return pl.pallas_call(
        functools.partial(_mixer_body, L=L),
        grid=(batch, spb),
        in_specs=[pl.BlockSpec((L, d), lambda b, c: (row(b, c), 0)),
                  col(0, 2),
                  col(1, 2),
                  col(4), col(5),
                  col(6), col(7),
                  col(8), col(9),
                  pl.BlockSpec((L, LANES), lambda b, c: (row(b, c), 0))] + [full(a) for a in consts],
        out_specs=pl.BlockSpec((L, d), lambda b, c: (row(b, c), 0)),
        out_shape=jax.ShapeDtypeStruct((n, d), F32),
        scratch_shapes=[pltpu.VMEM((L + SSD_HALO, nxbc), F32),
                        pltpu.VMEM((L, nxbc), F32),
                        pltpu.VMEM((L + CONF_HALO, d), F32),
                        pltpu.VMEM((7, L + CONF_HALO - 8, CONV_COLS), F32),
                        pltpu.VMEM((SSD_GROUPS, SSD_STATE, GROUP_DIM), F32),
                        pltpu.VMEM((L, D_INNER), BF16),
                        pltpu.VMEM((L, d), F32)],
        compiler_params=pltpu.CompilerParams(dimension_semantics=("arbitrary", "arbitrary"),
                                             vmem_limit_bytes=56 * 1024 * 1024),
        name="token_mixers",
    )(x2d, proj, proj, proj, proj, proj, proj, proj, proj, dt, *consts)


NK = PEER_N_KEYS
TOPK = PEER_TOPK
NEG = float("-inf")


def _top16_rows(s, nrows):
    iota = lax.broadcasted_iota(jnp.int32, s.shape, 0)
    vals, idxs = [], []
    for _ in range(TOPK):
        m = jnp.max(s, axis=0, keepdims=True)
        idx = jnp.min(jnp.where(s == m, iota, nrows), axis=0, keepdims=True)
        vals.append(m)
        idxs.append(idx)
        s = jnp.where(iota == idx, NEG, s)
    return jnp.concatenate(vals, axis=0), jnp.concatenate(idxs, axis=0)


def _route_body(h_ref, nw_ref, wqt_ref, keys_ref, xn_ref, e_ref, g_ref, qt_scr, e_scr, g_scr, *, heads):
    h = h_ref[...]
    xnb = _rms(h, nw_ref[...]).astype(BF16)
    xn_ref[...] = xnb
    qt_scr[...] = lax.dot_general(wqt_ref[...], xnb, (((1,), (1,)), ((), ())),
                                  preferred_element_type=F32)
    t = h.shape[0]
    iota8 = lax.broadcasted_iota(jnp.int32, (8, t), 0)

    def head(hd, carry):
        sv, si = [], []
        for c in range(2):
            row0 = pl.multiple_of((hd * 2 + c) * NK, NK)
            q = qt_scr[pl.ds(row0, NK), :].astype(BF16)
            s = jnp.dot(keys_ref[hd * 2 + c], q, preferred_element_type=F32)
            v, i = _top16_rows(s, NK)
            sv.append(v)
            si.append(i)
        sv0, sv1 = sv
        si0, si1 = si
        pv = [sv0[0:1] + sv1, sv0[1:2] + sv1[0:8]]
        pe = [si0[0:1] * NK + si1, si0[1:2] * NK + si1[0:8]]
        for p, n in ((2, 5), (3, 4), (4, 3), (5, 2), (6, 2), (7, 2)):
            pv.append(jnp.where(iota8 < n, sv0[p:p + 1] + sv1[0:8], NEG))
            pe.append(si0[p:p + 1] * NK + si1[0:8])
        pv.append(sv0[8:16] + sv1[0:1])
        pe.append(si0[8:16] * NK + si1[0:1])
        cand = jnp.concatenate(pv, axis=0)
        cande = jnp.concatenate(pe, axis=0)
        nrows = cand.shape[0]
        iota = lax.broadcasted_iota(jnp.int32, cand.shape, 0)
        best, experts = [], []
        for _ in range(TOPK):
            m = jnp.max(cand, axis=0, keepdims=True)
            idx = jnp.min(jnp.where(cand == m, iota, nrows), axis=0, keepdims=True)
            sel = iota == idx
            experts.append(jnp.sum(jnp.where(sel, cande, 0), axis=0, keepdims=True))
            best.append(m)
            cand = jnp.where(sel, NEG, cand)
        best = jnp.concatenate(best, axis=0)
        ex = jnp.exp(best - best[0:1])
        gate = ex / jnp.sum(ex, axis=0, keepdims=True)
        r0 = pl.multiple_of(hd * TOPK, TOPK)
        e_scr[pl.ds(r0, TOPK), :] = jnp.concatenate(experts, axis=0)
        g_scr[pl.ds(r0, TOPK), :] = gate
        return carry

    lax.fori_loop(0, heads, head, 0)
    e_ref[...] = e_scr[...].T
    g_ref[...] = g_scr[...].T


def peer_route(h2d, norm_w, wqt, keys_hc, *, tile):
    n, d = h2d.shape
    heads = keys_hc.shape[0] // 2
    nsel = heads * TOPK
    return pl.pallas_call(
        functools.partial(_route_body, heads=heads),
        grid=(n // tile,),
        in_specs=[pl.BlockSpec((tile, d), lambda i: (i, 0)),
                  pl.BlockSpec((1, d), lambda i: (0, 0)),
                  pl.BlockSpec(wqt.shape, lambda i: (0, 0)),
                  pl.BlockSpec(keys_hc.shape, lambda i: (0, 0, 0))],
        out_specs=[pl.BlockSpec((tile, d), lambda i: (i, 0)),
                   pl.BlockSpec((tile, nsel), lambda i: (i, 0)),
                   pl.BlockSpec((tile, nsel), lambda i: (i, 0))],
        out_shape=[jax.ShapeDtypeStruct((n, d), BF16),
                   jax.ShapeDtypeStruct((n, nsel), jnp.int32),
                   jax.ShapeDtypeStruct((n, nsel), F32)],
        scratch_shapes=[pltpu.VMEM((wqt.shape[0], tile), F32),
                        pltpu.VMEM((nsel, tile), jnp.int32),
                        pltpu.VMEM((nsel, tile), F32)],
        compiler_params=pltpu.CompilerParams(dimension_semantics=("arbitrary",),
                                             vmem_limit_bytes=48 * 1024 * 1024),
        name="peer_route",
    )(h2d, norm_w.reshape(1, d), wqt, keys_hc)


def _gelu(x):
    return 0.5 * x * (1.0 + lax.erf(x * np.float32(0.7071067811865476)))


def _retrieve_body(h_ref, x_ref, e_ref, g_ref, ut_ref, v_ref, o_ref, act_scr, sc0_scr, sc1_scr, w3_scr,
                   *, nblk, tile):
    k = pl.program_id(1)
    eb = ut_ref.shape[1]
    cpb = eb // NK
    nsel = e_ref.shape[1]

    sc_refs = (sc0_scr, sc1_scr)

    def score_block(sc_ref):
        sc_ref[...] = jnp.dot(x_ref[...], ut_ref[...], preferred_element_type=F32)

    def pick_block(sc_ref, kb):
        e = e_ref[...]
        row = e >> 7
        col = e & (NK - 1)
        act = act_scr[...]
        for cc in range(cpb):
            picked = jnp.take_along_axis(sc_ref[:, cc * NK:(cc + 1) * NK], col, axis=1)
            act = jnp.where(row == kb * cpb + cc, picked, act)
        act_scr[...] = act

    @pl.when(k == 0)
    def _():
        act_scr[...] = jnp.zeros_like(act_scr)
        score_block(sc_refs[0])

    for par in range(2):
        @pl.when(jnp.logical_and(jnp.logical_and(k > 0, k < nblk), k % 2 == par))
        def _(par=par):
            score_block(sc_refs[par])
            pick_block(sc_refs[1 - par], k - 1)

    @pl.when(k == nblk)
    def _():
        pick_block(sc_refs[(nblk - 1) % 2], k - 1)
        act_scr[...] = g_ref[...] * _gelu(act_scr[...])
        iota = lax.broadcasted_iota(jnp.int32, (NK, nsel), 0).astype(F32).astype(BF16)
        one = jnp.ones((NK, nsel), BF16)
        zero = jnp.zeros((NK, nsel), BF16)

        def tok16(t16, carry):
            t0 = pl.multiple_of(t16 * 16, 16)
            e_rows = e_ref[pl.ds(t0, 16), :]
            i1_rows = (e_rows >> 7).astype(F32).astype(BF16)
            i2_rows = (e_rows & (NK - 1)).astype(F32).astype(BF16)
            w_rows = act_scr[pl.ds(t0, 16), :].astype(BF16)
            w_toks = []
            for j in range(16):
                pm = jnp.where(iota == i1_rows[j:j + 1], one, zero)
                qm = jnp.where(iota == i2_rows[j:j + 1], jnp.broadcast_to(w_rows[j:j + 1], (NK, nsel)), zero)
                w_toks.append(lax.dot_general(pm, qm, (((1,), (1,)), ((), ())), preferred_element_type=F32))
            w3_scr[:, pl.ds(t0, 16), :] = jnp.swapaxes(jnp.stack(w_toks, axis=0), 0, 1).astype(BF16)
            return carry

        lax.fori_loop(0, tile // 16, tok16, 0)

    def weights_times_values():
        kk = k - nblk
        parts = [w3_scr[kk * cpb + cc] for cc in range(cpb)]
        return jnp.dot(jnp.concatenate(parts, axis=1), v_ref[...], preferred_element_type=F32)

    @pl.when(k == nblk)
    def _():
        o_ref[...] = h_ref[...] + weights_times_values()

    @pl.when(k > nblk)
    def _():
        o_ref[...] += weights_times_values()


def peer_retrieve(h2d, xn, experts, gates, ut, v, *, tile, eb):
    n, d = h2d.shape
    nsel = experts.shape[1]
    ne = ut.shape[1]
    nblk = ne // eb
    return pl.pallas_call(
        functools.partial(_retrieve_body, nblk=nblk, tile=tile),
        grid=(n // tile, 2 * nblk),
        in_specs=[pl.BlockSpec((tile, d), lambda i, k: (i, 0)),
                  pl.BlockSpec((tile, d), lambda i, k: (i, 0)),
                  pl.BlockSpec((tile, nsel), lambda i, k: (i, 0)),
                  pl.BlockSpec((tile, nsel), lambda i, k: (i, 0)),
                  pl.BlockSpec((d, eb), lambda i, k: (0, jnp.minimum(k, nblk - 1))),
                  pl.BlockSpec((eb, d), lambda i, k: (jnp.maximum(k - nblk, 0), 0))],
        out_specs=pl.BlockSpec((tile, d), lambda i, k: (i, 0)),
        out_shape=jax.ShapeDtypeStruct((n, d), F32),
        scratch_shapes=[pltpu.VMEM((tile, nsel), F32),
                        pltpu.VMEM((tile, eb), F32),
                        pltpu.VMEM((tile, eb), F32),
                        pltpu.VMEM((ne // NK, tile, NK), BF16)],
        compiler_params=pltpu.CompilerParams(dimension_semantics=("arbitrary", "arbitrary"),
                                             vmem_limit_bytes=56 * 1024 * 1024),
        name="peer_retrieve",
    )(h2d, xn, experts, gates, ut, v)


def _ple_body(h_ref, p_ref, nw_ref, wg_ref, wp_ref, fw_ref, o_ref):
    h = h_ref[...]
    gate = _sigmoid(jnp.dot(_rms(h, nw_ref[...]).astype(BF16), wg_ref[...], preferred_element_type=F32))
    h = h + gate * jnp.dot(p_ref[...].astype(BF16), wp_ref[...], preferred_element_type=F32)
    o_ref[...] = _rms(h, fw_ref[...])


def ple_final(h2d, p2d, norm_w, w_gate, w_proj, final_w, *, tm):
    n, d = h2d.shape
    pd = p2d.shape[1]
    return pl.pallas_call(
        _ple_body,
        grid=(n // tm,),
        in_specs=[pl.BlockSpec((tm, d), lambda i: (i, 0)),
                  pl.BlockSpec((tm, pd), lambda i: (i, 0)),
                  pl.BlockSpec((1, d), lambda i: (0, 0)),
                  pl.BlockSpec((d, d), lambda i: (0, 0)),
                  pl.BlockSpec((pd, d), lambda i: (0, 0)),
                  pl.BlockSpec((1, d), lambda i: (0, 0))],
        out_specs=pl.BlockSpec((tm, d), lambda i: (i, 0)),
        out_shape=jax.ShapeDtypeStruct((n, d), F32),
        compiler_params=pltpu.CompilerParams(dimension_semantics=("arbitrary",),
                                             vmem_limit_bytes=40 * 1024 * 1024),
        name="ple_final",
    )(h2d, p2d, norm_w.reshape(1, d), w_gate, w_proj, final_w.reshape(1, d))


def kernel(x, p, norm_mix_w, w_in, conv_ssd_w, conv_ssd_b, dt_bias, a_log, d_skip,
           ssd_norm_w, w_ssd_out, conv_dw_w, conv_dw_b, conv_ln_w, conv_ln_b,
           w_conv_out, b_conv_out, w_o, norm_ffn_w, peer_wq, peer_keys, peer_u, peer_v,
           norm_ple_w, w_ple_gate, w_ple_proj, final_norm_w):
    bsz, s, d = x.shape
    x2d = x.reshape(bsz * s, d)
    i = 0
    r1 = lambda v: v.reshape(1, -1)

    col_xbc = D_INNER + conv_ssd_w.shape[2]
    col_dt = col_xbc + SSD_HEADS
    w_main = jnp.concatenate([w_in[i][:, :col_xbc], w_in[i][:, col_dt:]], axis=1).astype(BF16)
    w_dt = jnp.pad(w_in[i][:, col_xbc:col_dt], ((0, 0), (0, LANES - SSD_HEADS))).astype(BF16)
    b_dt = jnp.pad(dt_bias[i], (0, LANES - SSD_HEADS)).reshape(1, LANES)
    a_pad = jnp.pad(-jnp.exp(a_log[i]), (0, LANES - SSD_HEADS)).reshape(1, LANES)
    dskip_x = jnp.repeat(d_skip[i], SSD_HEAD_DIM).reshape(1, D_INNER)
    expand = (jnp.arange(D_INNER)[None, :] // SSD_HEAD_DIM == jnp.arange(LANES)[:, None]).astype(BF16)

    proj, dt = in_projection(x2d, norm_mix_w[i], w_main, w_dt, b_dt, tm=INPROJ_TM, tn=INPROJ_TN)
    h2d = token_mixers(x2d, proj, dt, conv_ssd_w[i], r1(conv_ssd_b[i]), a_pad, expand, dskip_x, r1(ssd_norm_w[i]),
                       w_ssd_out[i].astype(BF16), conv_dw_w[i], r1(conv_dw_b[i]), r1(conv_ln_w[i]),
                       r1(conv_ln_b[i]), w_conv_out[i].astype(BF16), r1(b_conv_out[i]), w_o[i].astype(BF16),
                       batch=bsz, L=MIX_L)

    wqt = peer_wq[i].T.astype(BF16)
    keys_hc = peer_keys[i].reshape(PEER_HEADS * 2, PEER_N_KEYS, PEER_HALF).astype(BF16)
    xn, experts, gates = peer_route(h2d, norm_ffn_w[i], wqt, keys_hc, tile=PEER_TILE)
    h2d = peer_retrieve(h2d, xn, experts, gates, peer_u[i].T.astype(BF16), peer_v[i].astype(BF16),
                        tile=PEER_RETRIEVE_TILE, eb=PEER_EXPERT_BLOCK)

    out = ple_final(h2d, p[i].reshape(bsz * s, -1), norm_ple_w[i], w_ple_gate[i].astype(BF16),
                    w_ple_proj[i].astype(BF16), final_norm_w, tm=PLE_TM)
    return out.reshape(bsz, s, d)
```

```python
import functools
import jax
import jax.numpy as jnp
from jax import lax
import numpy as np
from jax.experimental import pallas as pl
from jax.experimental.pallas import tpu as pltpu

D_MODEL = 1024
D_INNER = 2 * D_MODEL
SSD_HEAD_DIM = 64
SSD_HEADS = D_INNER // SSD_HEAD_DIM
SSD_GROUPS = 8
SSD_STATE = 128
SSD_CONV = 4
HEADS_PER_GROUP = SSD_HEADS // SSD_GROUPS
GROUP_DIM = D_INNER // SSD_GROUPS
CONF_KERNEL = 31
PEER_HEADS = 8
PEER_N_KEYS = 128
PEER_TOPK = 16
PEER_HALF = 128
EPS = 1e-6
F32 = jnp.float32
BF16 = jnp.bfloat16
LANES = 128
HIGHEST = lax.Precision.HIGHEST

INPROJ_TM = 1024
INPROJ_TN = 1024
MIX_L = 256
PEER_TILE = 256
PEER_RETRIEVE_TILE = 512
PEER_EXPERT_BLOCK = 2048
PLE_TM = 512


def _sigmoid(x):
    return 1.0 / (1.0 + jnp.exp(-x))


def _silu(x):
    return x * _sigmoid(x)


def _rms(x, w):
    return x * lax.rsqrt(jnp.mean(x * x, axis=-1, keepdims=True) + EPS) * w


def _inproj_body(x_ref, nw_ref, w_ref, wdt_ref, bdt_ref, o_ref, dt_ref, hn_scr):
    @pl.when(pl.program_id(1) == 0)
    def _():
        hn = _rms(x_ref[...], nw_ref[...]).astype(BF16)
        hn_scr[...] = hn
        v = jnp.dot(hn, wdt_ref[...], preferred_element_type=F32) + bdt_ref[...]
        dt_ref[...] = jnp.maximum(v, 0.0) + jnp.log(1.0 + jnp.exp(-jnp.abs(v)))

    o_ref[...] = jnp.dot(hn_scr[...], w_ref[...], preferred_element_type=F32).astype(BF16)


def in_projection(x2d, norm_w, w_main, w_dt, b_dt, *, tm, tn):
    n, d = x2d.shape
    c = w_main.shape[1]
    return pl.pallas_call(
        _inproj_body,
        grid=(n // tm, c // tn),
        in_specs=[pl.BlockSpec((tm, d), lambda i, j: (i, 0)),
                  pl.BlockSpec((1, d), lambda i, j: (0, 0)),
                  pl.BlockSpec((d, tn), lambda i, j: (0, j)),
                  pl.BlockSpec((d, LANES), lambda i, j: (0, 0)),
                  pl.BlockSpec((1, LANES), lambda i, j: (0, 0))],
        out_specs=[pl.BlockSpec((tm, tn), lambda i, j: (i, j)),
                   pl.BlockSpec((tm, LANES), lambda i, j: (i, 0))],
        out_shape=[jax.ShapeDtypeStruct((n, c), BF16), jax.ShapeDtypeStruct((n, LANES), F32)],
        scratch_shapes=[pltpu.VMEM((tm, d), BF16)],
        compiler_params=pltpu.CompilerParams(dimension_semantics=("arbitrary", "arbitrary"),
                                             vmem_limit_bytes=40 * 1024 * 1024),
        name="in_projection",
    )(x2d, norm_w.reshape(1, d), w_main, w_dt, b_dt)


SSD_HALO = 8
CONF_HALO = 32
CONV_COLS = 512


def _mixer_body(x_ref, z_ref, xs_ref, b_ref, c_ref, glua_ref, glub_ref, ga_ref, gb_ref, dt_ref,
                cw_ref, cb_ref, a_ref, expand_ref, dsk_ref, nw_ref, wso_ref,
                cdw_ref, cdb_ref, lnw_ref, lnb_ref, wco_ref, bco_ref, wo_ref,
                h_ref,
                ext_scr, act_scr, uext_scr, ushift_scr, state_scr, y_scr, conv_scr, *, L):
    step = pl.program_id(1)

    @pl.when(step == 0)
    def _():
        ext_scr[0:SSD_HALO, :] = jnp.zeros((SSD_HALO, ext_scr.shape[1]), F32)
        uext_scr[0:CONF_HALO, :] = jnp.zeros((CONF_HALO, uext_scr.shape[1]), F32)
        state_scr[...] = jnp.zeros_like(state_scr)

    nx = xs_ref.shape[1]
    nb = b_ref.shape[1]
    ext_scr[SSD_HALO:SSD_HALO + L, 0:nx] = xs_ref[...].astype(F32)
    ext_scr[SSD_HALO:SSD_HALO + L, nx:nx + nb] = b_ref[...].astype(F32)
    ext_scr[SSD_HALO:SSD_HALO + L, nx + nb:nx + 2 * nb] = c_ref[...].astype(F32)
    for j in range(ext_scr.shape[1] // CONV_COLS):
        cs = slice(j * CONV_COLS, (j + 1) * CONV_COLS)
        acc = jnp.broadcast_to(cb_ref[:, cs], (L, CONV_COLS))
        for k in range(SSD_CONV):
            off = SSD_HALO - (SSD_CONV - 1) + k
            acc = acc + cw_ref[k:k + 1, cs] * ext_scr[off:off + L, cs]
        act_scr[:, cs] = _silu(acc)
    ext_scr[0:SSD_HALO, :] = ext_scr[L:L + SSD_HALO, :]

    dt = dt_ref[...]
    la = dt * a_ref[...]
    ri = lax.broadcasted_iota(jnp.int32, (L, L), 0)
    ci = lax.broadcasted_iota(jnp.int32, (L, L), 1)
    causal = ri >= ci
    tril = jnp.where(causal, 1.0, 0.0)
    triu = jnp.where(ri <= ci, 1.0, 0.0)
    acum = jnp.dot(tril, la, precision=HIGHEST, preferred_element_type=F32)
    acum_t = jnp.dot(la.T, triu, precision=HIGHEST, preferred_element_type=F32)
    acum_last = acum[L - 1:L, :]
    stacked = jnp.concatenate([dt, jnp.exp(acum_last - acum), jnp.exp(acum)], axis=0)
    s_hi = stacked.astype(BF16)
    s_lo = (stacked - s_hi.astype(F32)).astype(BF16)
    ex = (jnp.dot(s_hi, expand_ref[...], preferred_element_type=F32)
          + jnp.dot(s_lo, expand_ref[...], preferred_element_type=F32))
    dt_x = ex[0:L]
    dec_x = ex[L:2 * L]
    eac_x = ex[2 * L:3 * L]
    elast_x = eac_x[L - 1:L, :]

    for g in range(SSD_GROUPS):
        gs = slice(g * GROUP_DIM, (g + 1) * GROUP_DIM)
        bg = act_scr[:, nx + g * SSD_STATE:nx + (g + 1) * SSD_STATE]
        cg = act_scr[:, nx + nb + g * SSD_STATE:nx + nb + (g + 1) * SSD_STATE]
        bgb = bg.astype(BF16)
        cgb = cg.astype(BF16)
        cb = lax.dot_general(cgb, bgb, (((1,), (1,)), ((), ())), preferred_element_type=F32)
        xg = act_scr[:, gs]
        xdt = xg * dt_x[:, gs]
        xdtb = xdt.astype(BF16)
        yd = []
        for r in range(HEADS_PER_GROUP):
            hd = g * HEADS_PER_GROUP + r
            seg = acum[:, hd:hd + 1] - acum_t[hd:hd + 1, :]
            lm = jnp.exp(jnp.where(causal, seg, -jnp.inf))
            m = (cb * lm).astype(BF16)
            yd.append(jnp.dot(m, xdtb[:, r * SSD_HEAD_DIM:(r + 1) * SSD_HEAD_DIM], preferred_element_type=F32))
        st = state_scr[g]
        y = (jnp.concatenate(yd, axis=1)
             + jnp.dot(cgb, st.astype(BF16), preferred_element_type=F32) * eac_x[:, gs]
             + dsk_ref[:, gs] * xg)
        state_scr[g] = st * elast_x[:, gs] + jnp.dot(bg.T.astype(BF16), (xdt * dec_x[:, gs]).astype(BF16),
                                                    preferred_element_type=F32)
        yz = y * _silu(z_ref[:, gs].astype(F32))
        y_scr[:, gs] = _rms(yz, nw_ref[:, gs]).astype(BF16)
    y_a = jnp.dot(y_scr[...], wso_ref[...], preferred_element_type=F32)

    uext_scr[CONF_HALO:CONF_HALO + L, :] = glua_ref[...].astype(F32) * _sigmoid(glub_ref[...].astype(F32))
    for j in range(uext_scr.shape[1] // CONV_COLS):
        cs = slice(j * CONV_COLS, (j + 1) * CONV_COLS)
        for sft in range(1, 8):
            ushift_scr[sft - 1] = uext_scr[sft:sft + L + CONF_HALO - 8, cs]
        acc = jnp.broadcast_to(cdb_ref[:, cs], (L, CONV_COLS))
        for k in range(CONF_KERNEL):
            off = CONF_HALO - (CONF_KERNEL - 1) + k
            q8, sft = (off // 8) * 8, off % 8
            tap = uext_scr[q8:q8 + L, cs] if sft == 0 else ushift_scr[sft - 1, q8:q8 + L, :]
            acc = acc + cdw_ref[k:k + 1, cs] * tap
        conv_scr[:, cs] = acc
    uext_scr[0:CONF_HALO, :] = uext_scr[L:L + CONF_HALO, :]
    u = conv_scr[...]
    mu = jnp.mean(u, axis=-1, keepdims=True)
    uc = u - mu
    un = uc * lax.rsqrt(jnp.mean(uc * uc, axis=-1, keepdims=True) + EPS) * lnw_ref[...] + lnb_ref[...]
    y_b = jnp.dot(_silu(un).astype(BF16), wco_ref[...], preferred_element_type=F32) + bco_ref[...]

    merged = _sigmoid(ga_ref[...].astype(F32)) * y_a + _sigmoid(gb_ref[...].astype(F32)) * y_b
    h_ref[...] = x_ref[...] + jnp.dot(merged.astype(BF16), wo_ref[...], preferred_element_type=F32)


def token_mixers(x2d, proj, dt, conv_w, conv_b, a_pad, expand, dskip_x, ssd_norm_w, w_ssd_out,
                 conv_dw_w, conv_dw_b, ln_w, ln_b, w_conv_out, b_conv_out, w_o, *, batch, L):
    n, d = x2d.shape
    spb = n // batch // L
    row = lambda b, c: b * spb + c
    col = lambda k, w=1: pl.BlockSpec((L, w * d), lambda b, c, k=k: (row(b, c), k))
    full = lambda a: pl.BlockSpec(a.shape, lambda b, c: (0,) * a.ndim)
    consts = [conv_w, conv_b, a_pad, expand, dskip_x, ssd_norm_w, w_ssd_out,
              conv_dw_w, conv_dw_b, ln_w, ln_b, w_conv_out, b_conv_out, w_o]
    nxbc = conv_w.shape[1]
    return pl.pallas_call(
        functools.partial(_mixer_body, L=L),
        grid=(batch, spb),
        in_specs=[pl.BlockSpec((L, d), lambda b, c: (row(b, c), 0)),
                  col(0, 2),
                  col(1, 2),
                  col(4), col(5),
                  col(6), col(7),
                  col(8), col(9),
                  pl.BlockSpec((L, LANES), lambda b, c: (row(b, c), 0))] + [full(a) for a in consts],
        out_specs=pl.BlockSpec((L, d), lambda b, c: (row(b, c), 0)),
        out_shape=jax.ShapeDtypeStruct((n, d), F32),
        scratch_shapes=[pltpu.VMEM((L + SSD_HALO, nxbc), F32),
                        pltpu.VMEM((L, nxbc), F32),
                        pltpu.VMEM((L + CONF_HALO, d), F32),
                        pltpu.VMEM((7, L + CONF_HALO - 8, CONV_COLS), F32),
                        pltpu.VMEM((SSD_GROUPS, SSD_STATE, GROUP_DIM), F32),
                        pltpu.VMEM((L, D_INNER), BF16),
                        pltpu.VMEM((L, d), F32)],
        compiler_params=pltpu.CompilerParams(dimension_semantics=("arbitrary", "arbitrary"),
                                             vmem_limit_bytes=56 * 1024 * 1024),
        name="token_mixers",
    )(x2d, proj, proj, proj, proj, proj, proj, proj, proj, dt, *consts)


NK = PEER_N_KEYS
TOPK = PEER_TOPK
NEG = float("-inf")


def _top16_rows(s, nrows):
    iota = lax.broadcasted_iota(jnp.int32, s.shape, 0)
    vals, idxs = [], []
    for _ in range(TOPK):
        m = jnp.max(s, axis=0, keepdims=True)
        idx = jnp.min(jnp.where(s == m, iota, nrows), axis=0, keepdims=True)
        vals.append(m)
        idxs.append(idx)
        s = jnp.where(iota == idx, NEG, s)
    return jnp.concatenate(vals, axis=0), jnp.concatenate(idxs, axis=0)


def _route_body(h_ref, nw_ref, wqt_ref, keys_ref, xn_ref, e_ref, g_ref, qt_scr, e_scr, g_scr, *, heads):
    h = h_ref[...]
    xnb = _rms(h, nw_ref[...]).astype(BF16)
    xn_ref[...] = xnb
    qt_scr[...] = lax.dot_general(wqt_ref[...], xnb, (((1,), (1,)), ((), ())),
                                  preferred_element_type=F32)
    t = h.shape[0]
    iota8 = lax.broadcasted_iota(jnp.int32, (8, t), 0)

    def head(hd, carry):
        sv, si = [], []
        for c in range(2):
            row0 = pl.multiple_of((hd * 2 + c) * NK, NK)
            q = qt_scr[pl.ds(row0, NK), :].astype(BF16)
            s = jnp.dot(keys_ref[hd * 2 + c], q, preferred_element_type=F32)
            v, i = _top16_rows(s, NK)
            sv.append(v)
            si.append(i)
        sv0, sv1 = sv
        si0, si1 = si
        pv = [sv0[0:1] + sv1, sv0[1:2] + sv1[0:8]]
        pe = [si0[0:1] * NK + si1, si0[1:2] * NK + si1[0:8]]
        for p, n in ((2, 5), (3, 4), (4, 3), (5, 2), (6, 2), (7, 2)):
            pv.append(jnp.where(iota8 < n, sv0[p:p + 1] + sv1[0:8], NEG))
            pe.append(si0[p:p + 1] * NK + si1[0:8])
        pv.append(sv0[8:16] + sv1[0:1])
        pe.append(si0[8:16] * NK + si1[0:1])
        cand = jnp.concatenate(pv, axis=0)
        cande = jnp.concatenate(pe, axis=0)
        nrows = cand.shape[0]
        iota = lax.broadcasted_iota(jnp.int32, cand.shape, 0)
        best, experts = [], []
        for _ in range(TOPK):
            m = jnp.max(cand, axis=0, keepdims=True)
            idx = jnp.min(jnp.where(cand == m, iota, nrows), axis=0, keepdims=True)
            sel = iota == idx
            experts.append(jnp.sum(jnp.where(sel, cande, 0), axis=0, keepdims=True))
            best.append(m)
            cand = jnp.where(sel, NEG, cand)
        best = jnp.concatenate(best, axis=0)
        ex = jnp.exp(best - best[0:1])
        gate = ex / jnp.sum(ex, axis=0, keepdims=True)
        r0 = pl.multiple_of(hd * TOPK, TOPK)
        e_scr[pl.ds(r0, TOPK), :] = jnp.concatenate(experts, axis=0)
        g_scr[pl.ds(r0, TOPK), :] = gate
        return carry

    lax.fori_loop(0, heads, head, 0)
    e_ref[...] = e_scr[...].T
    g_ref[...] = g_scr[...].T


def peer_route(h2d, norm_w, wqt, keys_hc, *, tile):
    n, d = h2d.shape
    heads = keys_hc.shape[0] // 2
    nsel = heads * TOPK
    return pl.pallas_call(
        functools.partial(_route_body, heads=heads),
        grid=(n // tile,),
        in_specs=[pl.BlockSpec((tile, d), lambda i: (i, 0)),
                  pl.BlockSpec((1, d), lambda i: (0, 0)),
                  pl.BlockSpec(wqt.shape, lambda i: (0, 0)),
                  pl.BlockSpec(keys_hc.shape, lambda i: (0, 0, 0))],
        out_specs=[pl.BlockSpec((tile, d), lambda i: (i, 0)),
                   pl.BlockSpec((tile, nsel), lambda i: (i, 0)),
                   pl.BlockSpec((tile, nsel), lambda i: (i, 0))],
        out_shape=[jax.ShapeDtypeStruct((n, d), BF16),
                   jax.ShapeDtypeStruct((n, nsel), jnp.int32),
                   jax.ShapeDtypeStruct((n, nsel), F32)],
        scratch_shapes=[pltpu.VMEM((wqt.shape[0], tile), F32),
                        pltpu.VMEM((nsel, tile), jnp.int32),
                        pltpu.VMEM((nsel, tile), F32)],
        compiler_params=pltpu.CompilerParams(dimension_semantics=("arbitrary",),
                                             vmem_limit_bytes=48 * 1024 * 1024),
        name="peer_route",
    )(h2d, norm_w.reshape(1, d), wqt, keys_hc)


def _gelu(x):
    return 0.5 * x * (1.0 + lax.erf(x * np.float32(0.7071067811865476)))


def _retrieve_body(h_ref, x_ref, e_ref, g_ref, ut_ref, v_ref, o_ref, act_scr, sc0_scr, sc1_scr, stg0_scr, stg1_scr,
                   w3_scr, *, nblk, tile):
    k = pl.program_id(1)
    eb = ut_ref.shape[1]
    cpb = eb // NK
    nsel = e_ref.shape[1]

    hb = eb // 2
    cph = cpb // 2

    def score_half(sc_ref, half):
        sc_ref[...] = jnp.dot(x_ref[...], ut_ref[:, half * hb:(half + 1) * hb], preferred_element_type=F32)

    def pick_half(sc_ref, chunk0):
        e = e_ref[...]
        row = e >> 7
        col = e & (NK - 1)
        act = act_scr[...]
        for cc in range(cph):
            picked = jnp.take_along_axis(sc_ref[:, cc * NK:(cc + 1) * NK], col, axis=1)
            act = jnp.where(row == chunk0 + cc, picked, act)
        act_scr[...] = act

    @pl.when(k == 0)
    def _():
        act_scr[...] = jnp.zeros_like(act_scr)
        score_half(sc0_scr, 0)

    @pl.when(jnp.logical_and(k > 0, k < nblk))
    def _():
        score_half(sc0_scr, 0)
        pick_half(sc1_scr, (k - 1) * cpb + cph)

    @pl.when(k < nblk)
    def _():
        score_half(sc1_scr, 1)
        pick_half(sc0_scr, k * cpb)

    @pl.when(k == nblk)
    def _():
        pick_half(sc1_scr, (k - 1) * cpb + cph)
        act_scr[...] = g_ref[...] * _gelu(act_scr[...])
        iota = lax.broadcasted_iota(jnp.int32, (NK, nsel), 0).astype(F32).astype(BF16)
        one = jnp.ones((NK, nsel), BF16)
        zero = jnp.zeros((NK, nsel), BF16)

        def scatter_group(grp, stage_ref):
            t0 = pl.multiple_of(grp * 16, 16)
            e_rows = e_ref[pl.ds(t0, 16), :]
            i1_rows = (e_rows >> 7).astype(F32).astype(BF16)
            i2_rows = (e_rows & (NK - 1)).astype(F32).astype(BF16)
            w_rows = act_scr[pl.ds(t0, 16), :].astype(BF16)
            for j in range(16):
                pm = jnp.where(iota == i1_rows[j:j + 1], one, zero)
                qm = jnp.where(iota == i2_rows[j:j + 1], jnp.broadcast_to(w_rows[j:j + 1], (NK, nsel)), zero)
                stage_ref[j] = lax.dot_general(pm, qm, (((1,), (1,)), ((), ())),
                                               preferred_element_type=F32).astype(BF16)

        def swap_group(grp, stage_ref):
            t0 = pl.multiple_of(grp * 16, 16)
            w3_scr[:, pl.ds(t0, 16), :] = jnp.swapaxes(stage_ref[...], 0, 1)

        ngrp = tile // 16
        scatter_group(0, stg0_scr)

        def pair(i2, carry):
            scatter_group(2 * i2 + 1, stg1_scr)
            swap_group(2 * i2, stg0_scr)
            scatter_group(2 * i2 + 2, stg0_scr)
            swap_group(2 * i2 + 1, stg1_scr)
            return carry

        lax.fori_loop(0, ngrp // 2 - 1, pair, 0)
        scatter_group(ngrp - 1, stg1_scr)
        swap_group(ngrp - 2, stg0_scr)
        swap_group(ngrp - 1, stg1_scr)

    def weights_times_values():
        kk = k - nblk
        parts = [w3_scr[kk * cpb + cc] for cc in range(cpb)]
        return jnp.dot(jnp.concatenate(parts, axis=1), v_ref[...], preferred_element_type=F32)

    @pl.when(k == nblk)
    def _():
        o_ref[...] = h_ref[...] + weights_times_values()

    @pl.when(k > nblk)
    def _():
        o_ref[...] += weights_times_values()


def peer_retrieve(h2d, xn, experts, gates, ut, v, *, tile, eb):
    n, d = h2d.shape
    nsel = experts.shape[1]
    ne = ut.shape[1]
    nblk = ne // eb
    return pl.pallas_call(
        functools.partial(_retrieve_body, nblk=nblk, tile=tile),
        grid=(n // tile, 2 * nblk),
        in_specs=[pl.BlockSpec((tile, d), lambda i, k: (i, 0)),
                  pl.BlockSpec((tile, d), lambda i, k: (i, 0)),
                  pl.BlockSpec((tile, nsel), lambda i, k: (i, 0)),
                  pl.BlockSpec((tile, nsel), lambda i, k: (i, 0)),
                  pl.BlockSpec((d, eb), lambda i, k: (0, jnp.minimum(k, nblk - 1))),
                  pl.BlockSpec((eb, d), lambda i, k: (jnp.maximum(k - nblk, 0), 0))],
        out_specs=pl.BlockSpec((tile, d), lambda i, k: (i, 0)),
        out_shape=jax.ShapeDtypeStruct((n, d), F32),
        scratch_shapes=[pltpu.VMEM((tile, nsel), F32),
                        pltpu.VMEM((tile, eb // 2), F32),
                        pltpu.VMEM((tile, eb // 2), F32),
                        pltpu.VMEM((16, NK, NK), BF16),
                        pltpu.VMEM((16, NK, NK), BF16),
                        pltpu.VMEM((ne // NK, tile, NK), BF16)],
        compiler_params=pltpu.CompilerParams(dimension_semantics=("arbitrary", "arbitrary"),
                                             vmem_limit_bytes=56 * 1024 * 1024),
        name="peer_retrieve",
    )(h2d, xn, experts, gates, ut, v)


def _ple_body(h_ref, p_ref, nw_ref, wg_ref, wp_ref, fw_ref, o_ref):
    h = h_ref[...]
    gate = _sigmoid(jnp.dot(_rms(h, nw_ref[...]).astype(BF16), wg_ref[...], preferred_element_type=F32))
    h = h + gate * jnp.dot(p_ref[...].astype(BF16), wp_ref[...], preferred_element_type=F32)
    o_ref[...] = _rms(h, fw_ref[...])


def ple_final(h2d, p2d, norm_w, w_gate, w_proj, final_w, *, tm):
    n, d = h2d.shape
    pd = p2d.shape[1]
    return pl.pallas_call(
        _ple_body,
        grid=(n // tm,),
        in_specs=[pl.BlockSpec((tm, d), lambda i: (i, 0)),
                  pl.BlockSpec((tm, pd), lambda i: (i, 0)),
                  pl.BlockSpec((1, d), lambda i: (0, 0)),
                  pl.BlockSpec((d, d), lambda i: (0, 0)),
                  pl.BlockSpec((pd, d), lambda i: (0, 0)),
                  pl.BlockSpec((1, d), lambda i: (0, 0))],
        out_specs=pl.BlockSpec((tm, d), lambda i: (i, 0)),
        out_shape=jax.ShapeDtypeStruct((n, d), F32),
        compiler_params=pltpu.CompilerParams(dimension_semantics=("arbitrary",),
                                             vmem_limit_bytes=40 * 1024 * 1024),
        name="ple_final",
    )(h2d, p2d, norm_w.reshape(1, d), w_gate, w_proj, final_w.reshape(1, d))


def kernel(x, p, norm_mix_w, w_in, conv_ssd_w, conv_ssd_b, dt_bias, a_log, d_skip,
           ssd_norm_w, w_ssd_out, conv_dw_w, conv_dw_b, conv_ln_w, conv_ln_b,
           w_conv_out, b_conv_out, w_o, norm_ffn_w, peer_wq, peer_keys, peer_u, peer_v,
           norm_ple_w, w_ple_gate, w_ple_proj, final_norm_w):
    bsz, s, d = x.shape
    x2d = x.reshape(bsz * s, d)
    i = 0
    r1 = lambda v: v.reshape(1, -1)

    col_xbc = D_INNER + conv_ssd_w.shape[2]
    col_dt = col_xbc + SSD_HEADS
    w_main = jnp.concatenate([w_in[i][:, :col_xbc], w_in[i][:, col_dt:]], axis=1).astype(BF16)
    w_dt = jnp.pad(w_in[i][:, col_xbc:col_dt], ((0, 0), (0, LANES - SSD_HEADS))).astype(BF16)
    b_dt = jnp.pad(dt_bias[i], (0, LANES - SSD_HEADS)).reshape(1, LANES)
    a_pad = jnp.pad(-jnp.exp(a_log[i]), (0, LANES - SSD_HEADS)).reshape(1, LANES)
    dskip_x = jnp.repeat(d_skip[i], SSD_HEAD_DIM).reshape(1, D_INNER)
    expand = (jnp.arange(D_INNER)[None, :] // SSD_HEAD_DIM == jnp.arange(LANES)[:, None]).astype(BF16)

    proj, dt = in_projection(x2d, norm_mix_w[i], w_main, w_dt, b_dt, tm=INPROJ_TM, tn=INPROJ_TN)
    h2d = token_mixers(x2d, proj, dt, conv_ssd_w[i], r1(conv_ssd_b[i]), a_pad, expand, dskip_x, r1(ssd_norm_w[i]),
                       w_ssd_out[i].astype(BF16), conv_dw_w[i], r1(conv_dw_b[i]), r1(conv_ln_w[i]),
                       r1(conv_ln_b[i]), w_conv_out[i].astype(BF16), r1(b_conv_out[i]), w_o[i].astype(BF16),
                       batch=bsz, L=MIX_L)

    wqt = peer_wq[i].T.astype(BF16)
    keys_hc = peer_keys[i].reshape(PEER_HEADS * 2, PEER_N_KEYS, PEER_HALF).astype(BF16)
    xn, experts, gates = peer_route(h2d, norm_ffn_w[i], wqt, keys_hc, tile=PEER_TILE)
    h2d = peer_retrieve(h2d, xn, experts, gates, peer_u[i].T.astype(BF16), peer_v[i].astype(BF16),
                        tile=PEER_RETRIEVE_TILE, eb=PEER_EXPERT_BLOCK)

    out = ple_final(h2d, p[i].reshape(bsz * s, -1), norm_ple_w[i], w_ple_gate[i].astype(BF16),
                    w_ple_proj[i].astype(BF16), final_norm_w, tm=PLE_TM)
    return out.reshape(bsz, s, d)
```

```python
import functools
import jax
import jax.numpy as jnp
from jax import lax
import numpy as np
from jax.experimental import pallas as pl
from jax.experimental.pallas import tpu as pltpu

D_MODEL = 1024
D_INNER = 2 * D_MODEL
SSD_HEAD_DIM = 64
SSD_HEADS = D_INNER // SSD_HEAD_DIM
SSD_GROUPS = 8
SSD_STATE = 128
SSD_CONV = 4
HEADS_PER_GROUP = SSD_HEADS // SSD_GROUPS
GROUP_DIM = D_INNER // SSD_GROUPS
CONF_KERNEL = 31
PEER_HEADS = 8
PEER_N_KEYS = 128
PEER_TOPK = 16
PEER_HALF = 128
EPS = 1e-6
F32 = jnp.float32
BF16 = jnp.bfloat16
LANES = 128
HIGHEST = lax.Precision.HIGHEST

INPROJ_TM = 1024
INPROJ_TN = 1024
MIX_L = 256
PEER_TILE = 256
PEER_EXPERT_BLOCK = 2048
PLE_TM = 512


def _sigmoid(x):
    return 1.0 / (1.0 + jnp.exp(-x))


def _silu(x):
    return x * _sigmoid(x)


def _rms(x, w):
    return x * lax.rsqrt(jnp.mean(x * x, axis=-1, keepdims=True) + EPS) * w


def _inproj_body(x_ref, nw_ref, w_ref, wdt_ref, bdt_ref, o_ref, dt_ref, hn_scr):
    @pl.when(pl.program_id(1) == 0)
    def _():
        hn = _rms(x_ref[...], nw_ref[...]).astype(BF16)
        hn_scr[...] = hn
        v = jnp.dot(hn, wdt_ref[...], preferred_element_type=F32) + bdt_ref[...]
        dt_ref[...] = jnp.maximum(v, 0.0) + jnp.log(1.0 + jnp.exp(-jnp.abs(v)))

    o_ref[...] = jnp.dot(hn_scr[...], w_ref[...], preferred_element_type=F32).astype(BF16)


def in_projection(x2d, norm_w, w_main, w_dt, b_dt, *, tm, tn):
    n, d = x2d.shape
    c = w_main.shape[1]
    return pl.pallas_call(
        _inproj_body,
        grid=(n // tm, c // tn),
        in_specs=[pl.BlockSpec((tm, d), lambda i, j: (i, 0)),
                  pl.BlockSpec((1, d), lambda i, j: (0, 0)),
                  pl.BlockSpec((d, tn), lambda i, j: (0, j)),
                  pl.BlockSpec((d, LANES), lambda i, j: (0, 0)),
                  pl.BlockSpec((1, LANES), lambda i, j: (0, 0))],
        out_specs=[pl.BlockSpec((tm, tn), lambda i, j: (i, j)),
                   pl.BlockSpec((tm, LANES), lambda i, j: (i, 0))],
        out_shape=[jax.ShapeDtypeStruct((n, c), BF16), jax.ShapeDtypeStruct((n, LANES), F32)],
        scratch_shapes=[pltpu.VMEM((tm, d), BF16)],
        compiler_params=pltpu.CompilerParams(dimension_semantics=("arbitrary", "arbitrary"),
                                             vmem_limit_bytes=40 * 1024 * 1024),
        name="in_projection",
    )(x2d, norm_w.reshape(1, d), w_main, w_dt, b_dt)


SSD_HALO = 8
CONF_HALO = 32
CONV_COLS = 512


def _mixer_body(x_ref, z_ref, xs_ref, b_ref, c_ref, glua_ref, glub_ref, ga_ref, gb_ref, dt_ref,
                cw_ref, cb_ref, a_ref, expand_ref, dsk_ref, nw_ref, wso_ref,
                cdw_ref, cdb_ref, lnw_ref, lnb_ref, wco_ref, bco_ref, wo_ref,
                h_ref,
                ext_scr, act_scr, uext_scr, ushift_scr, state_scr, y_scr, conv_scr, *, L):
    step = pl.program_id(1)

    @pl.when(step == 0)
    def _():
        ext_scr[0:SSD_HALO, :] = jnp.zeros((SSD_HALO, ext_scr.shape[1]), F32)
        uext_scr[0:CONF_HALO, :] = jnp.zeros((CONF_HALO, uext_scr.shape[1]), F32)
        state_scr[...] = jnp.zeros_like(state_scr)

    nx = xs_ref.shape[1]
    nb = b_ref.shape[1]
    ext_scr[SSD_HALO:SSD_HALO + L, 0:nx] = xs_ref[...].astype(F32)
    ext_scr[SSD_HALO:SSD_HALO + L, nx:nx + nb] = b_ref[...].astype(F32)
    ext_scr[SSD_HALO:SSD_HALO + L, nx + nb:nx + 2 * nb] = c_ref[...].astype(F32)
    for j in range(ext_scr.shape[1] // CONV_COLS):
        cs = slice(j * CONV_COLS, (j + 1) * CONV_COLS)
        acc = jnp.broadcast_to(cb_ref[:, cs], (L, CONV_COLS))
        for k in range(SSD_CONV):
            off = SSD_HALO - (SSD_CONV - 1) + k
            acc = acc + cw_ref[k:k + 1, cs] * ext_scr[off:off + L, cs]
        act_scr[:, cs] = _silu(acc)
    ext_scr[0:SSD_HALO, :] = ext_scr[L:L + SSD_HALO, :]

    dt = dt_ref[...]
    la = dt * a_ref[...]
    ri = lax.broadcasted_iota(jnp.int32, (L, L), 0)
    ci = lax.broadcasted_iota(jnp.int32, (L, L), 1)
    causal = ri >= ci
    tril = jnp.where(causal, 1.0, 0.0)
    triu = jnp.where(ri <= ci, 1.0, 0.0)
    acum = jnp.dot(tril, la, precision=HIGHEST, preferred_element_type=F32)
    acum_t = jnp.dot(la.T, triu, precision=HIGHEST, preferred_element_type=F32)
    acum_last = acum[L - 1:L, :]
    stacked = jnp.concatenate([dt, jnp.exp(acum_last - acum), jnp.exp(acum)], axis=0)
    s_hi = stacked.astype(BF16)
    s_lo = (stacked - s_hi.astype(F32)).astype(BF16)
    ex = (jnp.dot(s_hi, expand_ref[...], preferred_element_type=F32)
          + jnp.dot(s_lo, expand_ref[...], preferred_element_type=F32))
    dt_x = ex[0:L]
    dec_x = ex[L:2 * L]
    eac_x = ex[2 * L:3 * L]
    elast_x = eac_x[L - 1:L, :]

    for g in range(SSD_GROUPS):
        gs = slice(g * GROUP_DIM, (g + 1) * GROUP_DIM)
        bg = act_scr[:, nx + g * SSD_STATE:nx + (g + 1) * SSD_STATE]
        cg = act_scr[:, nx + nb + g * SSD_STATE:nx + nb + (g + 1) * SSD_STATE]
        bgb = bg.astype(BF16)
        cgb = cg.astype(BF16)
        cb = lax.dot_general(cgb, bgb, (((1,), (1,)), ((), ())), preferred_element_type=F32)
        xg = act_scr[:, gs]
        xdt = xg * dt_x[:, gs]
        xdtb = xdt.astype(BF16)
        yd = []
        for r in range(HEADS_PER_GROUP):
            hd = g * HEADS_PER_GROUP + r
            seg = acum[:, hd:hd + 1] - acum_t[hd:hd + 1, :]
            lm = jnp.exp(jnp.where(causal, seg, -jnp.inf))
            m = (cb * lm).astype(BF16)
            yd.append(jnp.dot(m, xdtb[:, r * SSD_HEAD_DIM:(r + 1) * SSD_HEAD_DIM], preferred_element_type=F32))
        st = state_scr[g]
        y = (jnp.concatenate(yd, axis=1)
             + jnp.dot(cgb, st.astype(BF16), preferred_element_type=F32) * eac_x[:, gs]
             + dsk_ref[:, gs] * xg)
        state_scr[g] = st * elast_x[:, gs] + jnp.dot(bg.T.astype(BF16), (xdt * dec_x[:, gs]).astype(BF16),
                                                    preferred_element_type=F32)
        yz = y * _silu(z_ref[:, gs].astype(F32))
        y_scr[:, gs] = _rms(yz, nw_ref[:, gs]).astype(BF16)
    y_a = jnp.dot(y_scr[...], wso_ref[...], preferred_element_type=F32)

    uext_scr[CONF_HALO:CONF_HALO + L, :] = glua_ref[...].astype(F32) * _sigmoid(glub_ref[...].astype(F32))
    for j in range(uext_scr.shape[1] // CONV_COLS):
        cs = slice(j * CONV_COLS, (j + 1) * CONV_COLS)
        for sft in range(1, 8):
            ushift_scr[sft - 1] = uext_scr[sft:sft + L + CONF_HALO - 8, cs]
        acc = jnp.broadcast_to(cdb_ref[:, cs], (L, CONV_COLS))
        for k in range(CONF_KERNEL):
            off = CONF_HALO - (CONF_KERNEL - 1) + k
            q8, sft = (off // 8) * 8, off % 8
            tap = uext_scr[q8:q8 + L, cs] if sft == 0 else ushift_scr[sft - 1, q8:q8 + L, :]
            acc = acc + cdw_ref[k:k + 1, cs] * tap
        conv_scr[:, cs] = acc
    uext_scr[0:CONF_HALO, :] = uext_scr[L:L + CONF_HALO, :]
    u = conv_scr[...]
    mu = jnp.mean(u, axis=-1, keepdims=True)
    uc = u - mu
    un = uc * lax.rsqrt(jnp.mean(uc * uc, axis=-1, keepdims=True) + EPS) * lnw_ref[...] + lnb_ref[...]
    y_b = jnp.dot(_silu(un).astype(BF16), wco_ref[...], preferred_element_type=F32) + bco_ref[...]

    merged = _sigmoid(ga_ref[...].astype(F32)) * y_a + _sigmoid(gb_ref[...].astype(F32)) * y_b
    h_ref[...] = x_ref[...] + jnp.dot(merged.astype(BF16), wo_ref[...], preferred_element_type=F32)


def token_mixers(x2d, proj, dt, conv_w, conv_b, a_pad, expand, dskip_x, ssd_norm_w, w_ssd_out,
                 conv_dw_w, conv_dw_b, ln_w, ln_b, w_conv_out, b_conv_out, w_o, *, batch, L):
    n, d = x2d.shape
    spb = n // batch // L
    row = lambda b, c: b * spb + c
    col = lambda k, w=1: pl.BlockSpec((L, w * d), lambda b, c, k=k: (row(b, c), k))
    full = lambda a: pl.BlockSpec(a.shape, lambda b, c: (0,) * a.ndim)
    consts = [conv_w, conv_b, a_pad, expand, dskip_x, ssd_norm_w, w_ssd_out,
              conv_dw_w, conv_dw_b, ln_w, ln_b, w_conv_out, b_conv_out, w_o]
    nxbc = conv_w.shape[1]
    return pl.pallas_call(
        functools.partial(_mixer_body, L=L),
        grid=(batch, spb),
        in_specs=[pl.BlockSpec((L, d), lambda b, c: (row(b, c), 0)),
                  col(0, 2),
                  col(1, 2),
                  col(4), col(5),
                  col(6), col(7),
                  col(8), col(9),
                  pl.BlockSpec((L, LANES), lambda b, c: (row(b, c), 0))] + [full(a) for a in consts],
        out_specs=pl.BlockSpec((L, d), lambda b, c: (row(b, c), 0)),
        out_shape=jax.ShapeDtypeStruct((n, d), F32),
        scratch_shapes=[pltpu.VMEM((L + SSD_HALO, nxbc), F32),
                        pltpu.VMEM((L, nxbc), F32),
                        pltpu.VMEM((L + CONF_HALO, d), F32),
                        pltpu.VMEM((7, L + CONF_HALO - 8, CONV_COLS), F32),
                        pltpu.VMEM((SSD_GROUPS, SSD_STATE, GROUP_DIM), F32),
                        pltpu.VMEM((L, D_INNER), BF16),
                        pltpu.VMEM((L, d), F32)],
        compiler_params=pltpu.CompilerParams(dimension_semantics=("arbitrary", "arbitrary"),
                                             vmem_limit_bytes=56 * 1024 * 1024),
        name="token_mixers",
    )(x2d, proj, proj, proj, proj, proj, proj, proj, proj, dt, *consts)


NK = PEER_N_KEYS
TOPK = PEER_TOPK
NEG = float("-inf")


def _top16_rows(s, nrows):
    iota = lax.broadcasted_iota(jnp.int32, s.shape, 0)
    vals, idxs = [], []
    for _ in range(TOPK):
        m = jnp.max(s, axis=0, keepdims=True)
        idx = jnp.min(jnp.where(s == m, iota, nrows), axis=0, keepdims=True)
        vals.append(m)
        idxs.append(idx)
        s = jnp.where(iota == idx, NEG, s)
    return jnp.concatenate(vals, axis=0), jnp.concatenate(idxs, axis=0)


def _gelu(x):
    return 0.5 * x * (1.0 + lax.erf(x * np.float32(0.7071067811865476)))


def _pair_top16(sv0, si0, sv1, si1):
    t = sv0.shape[1]
    iota8 = lax.broadcasted_iota(jnp.int32, (8, t), 0)
    pv = [sv0[0:1] + sv1, sv0[1:2] + sv1[0:8]]
    pe = [si0[0:1] * NK + si1, si0[1:2] * NK + si1[0:8]]
    for p, n in ((2, 5), (3, 4), (4, 3), (5, 2), (6, 2), (7, 2)):
        pv.append(jnp.where(iota8 < n, sv0[p:p + 1] + sv1[0:8], NEG))
        pe.append(si0[p:p + 1] * NK + si1[0:8])
    pv.append(sv0[8:16] + sv1[0:1])
    pe.append(si0[8:16] * NK + si1[0:1])
    cand = jnp.concatenate(pv, axis=0)
    cande = jnp.concatenate(pe, axis=0)
    nrows = cand.shape[0]
    iota = lax.broadcasted_iota(jnp.int32, cand.shape, 0)
    best, experts = [], []
    for _ in range(TOPK):
        m = jnp.max(cand, axis=0, keepdims=True)
        idx = jnp.min(jnp.where(cand == m, iota, nrows), axis=0, keepdims=True)
        sel = iota == idx
        experts.append(jnp.sum(jnp.where(sel, cande, 0), axis=0, keepdims=True))
        best.append(m)
        cand = jnp.where(sel, NEG, cand)
    best = jnp.concatenate(best, axis=0)
    ex = jnp.exp(best - best[0:1])
    return jnp.concatenate(experts, axis=0), ex / jnp.sum(ex, axis=0, keepdims=True)


def _peer_body(hc_ref, hn_ref, nw_ref, wqt_ref, keys_ref, ut_ref, v_ref, o_ref,
               xn_nxt, xnt_nxt, et_nxt, gt_nxt, xn_cur, e_cur, g_cur, s_scr, sv0_scr, si0_scr,
               act_scr, sc0_scr, sc1_scr, stg0_scr, stg1_scr, w3_scr, *, nblk, tile):
    i = pl.program_id(0)
    k = pl.program_id(1)
    eb = ut_ref.shape[1]
    cpb = eb // NK
    nsel = e_cur.shape[1]
    hb = eb // 2
    cph = cpb // 2

    def key_scores(c):
        q = jnp.dot(wqt_ref[c * NK:(c + 1) * NK, :], xnt_nxt[...], preferred_element_type=F32)
        return jnp.dot(keys_ref[c], q.astype(BF16), preferred_element_type=F32)

    def route_piece(c):
        v, idx = _top16_rows(s_scr[...], NK)
        if c == 0:
            sv0_scr[...] = v
            si0_scr[...] = idx
        else:
            experts, gate = _pair_top16(sv0_scr[...], si0_scr[...], v, idx)
            r0 = pl.multiple_of((k // 2) * TOPK, TOPK)
            et_nxt[pl.ds(r0, TOPK), :] = experts
            gt_nxt[pl.ds(r0, TOPK), :] = gate
        s_scr[...] = key_scores(1 - c)

    @pl.when(k == 0)
    def _():
        @pl.when(i == 0)
        def _():
            xn_nxt[...] = jnp.zeros_like(xn_nxt)
            et_nxt[...] = jnp.zeros_like(et_nxt)
            gt_nxt[...] = jnp.zeros_like(gt_nxt)
            sc1_scr[...] = jnp.zeros_like(sc1_scr)

        xn_cur[...] = xn_nxt[...]
        e_cur[...] = et_nxt[...].T
        g_cur[...] = gt_nxt[...].T
        xn = _rms(hn_ref[...], nw_ref[...])
        xn_nxt[...] = xn.astype(BF16)
        xnt_nxt[...] = xn.T.astype(BF16)
        s_scr[...] = key_scores(0)
        act_scr[...] = jnp.zeros_like(act_scr)

    def score_half(sc_ref, half):
        sc_ref[...] = jnp.dot(xn_cur[...], ut_ref[:, half * hb:(half + 1) * hb], preferred_element_type=F32)

    def pick_half(sc_ref, chunk0):
        e = e_cur[...]
        row = e >> 7
        col = e & (NK - 1)
        act = act_scr[...]
        for cc in range(cph):
            picked = jnp.take_along_axis(sc_ref[:, cc * NK:(cc + 1) * NK], col, axis=1)
            act = jnp.where(row == chunk0 + cc, picked, act)
        act_scr[...] = act

    def scatter_weights():
        act_scr[...] = g_cur[...] * _gelu(act_scr[...])
        iota = lax.broadcasted_iota(jnp.int32, (NK, nsel), 0).astype(F32).astype(BF16)
        one = jnp.ones((NK, nsel), BF16)
        zero = jnp.zeros((NK, nsel), BF16)

        def scatter_group(grp, stage_ref):
            t0 = pl.multiple_of(grp * 16, 16)
            e_rows = e_cur[pl.ds(t0, 16), :]
            i1_rows = (e_rows >> 7).astype(F32).astype(BF16)
            i2_rows = (e_rows & (NK - 1)).astype(F32).astype(BF16)
            w_rows = act_scr[pl.ds(t0, 16), :].astype(BF16)
            for j in range(16):
                pm = jnp.where(iota == i1_rows[j:j + 1], one, zero)
                qm = jnp.where(iota == i2_rows[j:j + 1], jnp.broadcast_to(w_rows[j:j + 1], (NK, nsel)), zero)
                stage_ref[j] = lax.dot_general(pm, qm, (((1,), (1,)), ((), ())),
                                               preferred_element_type=F32).astype(BF16)

        def swap_group(grp, stage_ref):
            t0 = pl.multiple_of(grp * 16, 16)
            w3_scr[:, pl.ds(t0, 16), :] = jnp.swapaxes(stage_ref[...], 0, 1)

        ngrp = tile // 16
        scatter_group(0, stg0_scr)

        def pair(i2, carry):
            scatter_group(2 * i2 + 1, stg1_scr)
            swap_group(2 * i2, stg0_scr)
            scatter_group(2 * i2 + 2, stg0_scr)
            swap_group(2 * i2 + 1, stg1_scr)
            return carry

        lax.fori_loop(0, ngrp // 2 - 1, pair, 0)
        scatter_group(ngrp - 1, stg1_scr)
        swap_group(ngrp - 2, stg0_scr)
        swap_group(ngrp - 1, stg1_scr)

    def weights_times_values():
        kk = k - nblk
        parts = [w3_scr[kk * cpb + cc] for cc in range(cpb)]
        return jnp.dot(jnp.concatenate(parts, axis=1), v_ref[...], preferred_element_type=F32)

    for c in range(2):
        @pl.when(jnp.logical_and(k < nblk, k % 2 == c))
        def _(c=c):
            route_piece(c)
            score_half(sc0_scr, 0)
            pick_half(sc1_scr, (k - 1) * cpb + cph)
            score_half(sc1_scr, 1)
            pick_half(sc0_scr, k * cpb)

    @pl.when(k == nblk)
    def _():
        pick_half(sc1_scr, (k - 1) * cpb + cph)
        scatter_weights()
        route_piece(nblk % 2)
        o_ref[...] = hc_ref[...] + weights_times_values()

    for c in range(2):
        @pl.when(jnp.logical_and(k > nblk, k % 2 == c))
        def _(c=c):
            route_piece(c)
            o_ref[...] += weights_times_values()


def peer_mixer(h2d, norm_w, wqt, keys_hc, ut, v, *, tile, eb):
    n, d = h2d.shape
    heads = keys_hc.shape[0] // 2
    nsel = heads * TOPK
    ne = ut.shape[1]
    nblk = ne // eb
    assert nblk == heads, "one half-head is routed per grid step, so steps per tile = 2 * heads"
    nt = n // tile
    return pl.pallas_call(
        functools.partial(_peer_body, nblk=nblk, tile=tile),
        grid=(nt + 1, 2 * nblk),
        in_specs=[pl.BlockSpec((tile, d), lambda i, k: (jnp.maximum(i - 1, 0), 0)),
                  pl.BlockSpec((tile, d), lambda i, k: (jnp.minimum(i, nt - 1), 0)),
                  pl.BlockSpec((1, d), lambda i, k: (0, 0)),
                  pl.BlockSpec((2 * NK, d), lambda i, k: (jnp.minimum((k + 1) // 2, heads - 1), 0)),
                  pl.BlockSpec((2, NK, NK), lambda i, k: (jnp.minimum((k + 1) // 2, heads - 1), 0, 0)),
                  pl.BlockSpec((d, eb), lambda i, k: (0, jnp.minimum(k, nblk - 1))),
                  pl.BlockSpec((eb, d), lambda i, k: (jnp.maximum(k - nblk, 0), 0))],
        out_specs=pl.BlockSpec((tile, d), lambda i, k: (jnp.maximum(i - 1, 0), 0)),
        out_shape=jax.ShapeDtypeStruct((n, d), F32),
        scratch_shapes=[pltpu.VMEM((tile, d), BF16),
                        pltpu.VMEM((d, tile), BF16),
                        pltpu.VMEM((nsel, tile), jnp.int32),
                        pltpu.VMEM((nsel, tile), F32),
                        pltpu.VMEM((tile, d), BF16),
                        pltpu.VMEM((tile, nsel), jnp.int32),
                        pltpu.VMEM((tile, nsel), F32),
                        pltpu.VMEM((NK, tile), F32),
                        pltpu.VMEM((TOPK, tile), F32),
                        pltpu.VMEM((TOPK, tile), jnp.int32),
                        pltpu.VMEM((tile, nsel), F32),
                        pltpu.VMEM((tile, eb // 2), F32),
                        pltpu.VMEM((tile, eb // 2), F32),
                        pltpu.VMEM((16, NK, NK), BF16),
                        pltpu.VMEM((16, NK, NK), BF16),
                        pltpu.VMEM((ne // NK, tile, NK), BF16)],
        compiler_params=pltpu.CompilerParams(dimension_semantics=("arbitrary", "arbitrary"),
                                             vmem_limit_bytes=56 * 1024 * 1024),
        name="peer_mixer",
    )(h2d, h2d, norm_w.reshape(1, d), wqt, keys_hc, ut, v)


def _ple_body(h_ref, p_ref, nw_ref, wg_ref, wp_ref, fw_ref, o_ref):
    h = h_ref[...]
    gate = _sigmoid(jnp.dot(_rms(h, nw_ref[...]).astype(BF16), wg_ref[...], preferred_element_type=F32))
    h = h + gate * jnp.dot(p_ref[...].astype(BF16), wp_ref[...], preferred_element_type=F32)
    o_ref[...] = _rms(h, fw_ref[...])


def ple_final(h2d, p2d, norm_w, w_gate, w_proj, final_w, *, tm):
    n, d = h2d.shape
    pd = p2d.shape[1]
    return pl.pallas_call(
        _ple_body,
        grid=(n // tm,),
        in_specs=[pl.BlockSpec((tm, d), lambda i: (i, 0)),
                  pl.BlockSpec((tm, pd), lambda i: (i, 0)),
                  pl.BlockSpec((1, d), lambda i: (0, 0)),
                  pl.BlockSpec((d, d), lambda i: (0, 0)),
                  pl.BlockSpec((pd, d), lambda i: (0, 0)),
                  pl.BlockSpec((1, d), lambda i: (0, 0))],
        out_specs=pl.BlockSpec((tm, d), lambda i: (i, 0)),
        out_shape=jax.ShapeDtypeStruct((n, d), F32),
        compiler_params=pltpu.CompilerParams(dimension_semantics=("arbitrary",),
                                             vmem_limit_bytes=40 * 1024 * 1024),
        name="ple_final",
    )(h2d, p2d, norm_w.reshape(1, d), w_gate, w_proj, final_w.reshape(1, d))


def kernel(x, p, norm_mix_w, w_in, conv_ssd_w, conv_ssd_b, dt_bias, a_log, d_skip,
           ssd_norm_w, w_ssd_out, conv_dw_w, conv_dw_b, conv_ln_w, conv_ln_b,
           w_conv_out, b_conv_out, w_o, norm_ffn_w, peer_wq, peer_keys, peer_u, peer_v,
           norm_ple_w, w_ple_gate, w_ple_proj, final_norm_w):
    bsz, s, d = x.shape
    x2d = x.reshape(bsz * s, d)
    i = 0
    r1 = lambda v: v.reshape(1, -1)

    col_xbc = D_INNER + conv_ssd_w.shape[2]
    col_dt = col_xbc + SSD_HEADS
    w_main = jnp.concatenate([w_in[i][:, :col_xbc], w_in[i][:, col_dt:]], axis=1).astype(BF16)
    w_dt = jnp.pad(w_in[i][:, col_xbc:col_dt], ((0, 0), (0, LANES - SSD_HEADS))).astype(BF16)
    b_dt = jnp.pad(dt_bias[i], (0, LANES - SSD_HEADS)).reshape(1, LANES)
    a_pad = jnp.pad(-jnp.exp(a_log[i]), (0, LANES - SSD_HEADS)).reshape(1, LANES)
    dskip_x = jnp.repeat(d_skip[i], SSD_HEAD_DIM).reshape(1, D_INNER)
    expand = (jnp.arange(D_INNER)[None, :] // SSD_HEAD_DIM == jnp.arange(LANES)[:, None]).astype(BF16)

    proj, dt = in_projection(x2d, norm_mix_w[i], w_main, w_dt, b_dt, tm=INPROJ_TM, tn=INPROJ_TN)
    h2d = token_mixers(x2d, proj, dt, conv_ssd_w[i], r1(conv_ssd_b[i]), a_pad, expand, dskip_x, r1(ssd_norm_w[i]),
                       w_ssd_out[i].astype(BF16), conv_dw_w[i], r1(conv_dw_b[i]), r1(conv_ln_w[i]),
                       r1(conv_ln_b[i]), w_conv_out[i].astype(BF16), r1(b_conv_out[i]), w_o[i].astype(BF16),
                       batch=bsz, L=MIX_L)

    wqt = peer_wq[i].T.astype(BF16)
    keys_hc = peer_keys[i].reshape(PEER_HEADS * 2, PEER_N_KEYS, PEER_HALF).astype(BF16)
    h2d = peer_mixer(h2d, norm_ffn_w[i], wqt, keys_hc, peer_u[i].T.astype(BF16), peer_v[i].astype(BF16),
                     tile=PEER_TILE, eb=PEER_EXPERT_BLOCK)

    out = ple_final(h2d, p[i].reshape(bsz * s, -1), norm_ple_w[i], w_ple_gate[i].astype(BF16),
                    w_ple_proj[i].astype(BF16), final_norm_w, tm=PLE_TM)
    return out.reshape(bsz, s, d)
```

```python
import functools
import jax
import jax.numpy as jnp
from jax import lax
import numpy as np
from jax.experimental import pallas as pl
from jax.experimental.pallas import tpu as pltpu

D_MODEL = 1024
D_INNER = 2 * D_MODEL
SSD_HEAD_DIM = 64
SSD_HEADS = D_INNER // SSD_HEAD_DIM
SSD_GROUPS = 8
SSD_STATE = 128
SSD_CONV = 4
HEADS_PER_GROUP = SSD_HEADS // SSD_GROUPS
GROUP_DIM = D_INNER // SSD_GROUPS
CONF_KERNEL = 31
PEER_HEADS = 8
PEER_N_KEYS = 128
PEER_TOPK = 16
PEER_HALF = 128
EPS = 1e-6
F32 = jnp.float32
BF16 = jnp.bfloat16
LANES = 128
HIGHEST = lax.Precision.HIGHEST

INPROJ_TM = 1024
INPROJ_TN = 1024
MIX_L = 256
PEER_TILE = 512
PEER_EXPERT_BLOCK = 2048
PLE_TM = 512


def _sigmoid(x):
    return 1.0 / (1.0 + jnp.exp(-x))


def _silu(x):
    return x * _sigmoid(x)


def _rms(x, w):
    return x * lax.rsqrt(jnp.mean(x * x, axis=-1, keepdims=True) + EPS) * w


def _inproj_body(x_ref, nw_ref, w_ref, wdt_ref, bdt_ref, o_ref, dt_ref, hn_scr):
    @pl.when(pl.program_id(1) == 0)
    def _():
        hn = _rms(x_ref[...], nw_ref[...]).astype(BF16)
        hn_scr[...] = hn
        v = jnp.dot(hn, wdt_ref[...], preferred_element_type=F32) + bdt_ref[...]
        dt_ref[...] = jnp.maximum(v, 0.0) + jnp.log(1.0 + jnp.exp(-jnp.abs(v)))

    o_ref[...] = jnp.dot(hn_scr[...], w_ref[...], preferred_element_type=F32).astype(BF16)


def in_projection(x2d, norm_w, w_main, w_dt, b_dt, *, tm, tn):
    n, d = x2d.shape
    c = w_main.shape[1]
    return pl.pallas_call(
        _inproj_body,
        grid=(n // tm, c // tn),
        in_specs=[pl.BlockSpec((tm, d), lambda i, j: (i, 0)),
                  pl.BlockSpec((1, d), lambda i, j: (0, 0)),
                  pl.BlockSpec((d, tn), lambda i, j: (0, j)),
                  pl.BlockSpec((d, LANES), lambda i, j: (0, 0)),
                  pl.BlockSpec((1, LANES), lambda i, j: (0, 0))],
        out_specs=[pl.BlockSpec((tm, tn), lambda i, j: (i, j)),
                   pl.BlockSpec((tm, LANES), lambda i, j: (i, 0))],
        out_shape=[jax.ShapeDtypeStruct((n, c), BF16), jax.ShapeDtypeStruct((n, LANES), F32)],
        scratch_shapes=[pltpu.VMEM((tm, d), BF16)],
        compiler_params=pltpu.CompilerParams(dimension_semantics=("arbitrary", "arbitrary"),
                                             vmem_limit_bytes=40 * 1024 * 1024),
        name="in_projection",
    )(x2d, norm_w.reshape(1, d), w_main, w_dt, b_dt)


SSD_HALO = 8
CONF_HALO = 32
CONV_COLS = 512


def _mixer_body(x_ref, z_ref, xs_ref, b_ref, c_ref, glua_ref, glub_ref, ga_ref, gb_ref, dt_ref,
                cw_ref, cb_ref, a_ref, expand_ref, dsk_ref, nw_ref, wso_ref,
                cdw_ref, cdb_ref, lnw_ref, lnb_ref, wco_ref, bco_ref, wo_ref,
                h_ref,
                ext_scr, act_scr, uext_scr, ushift_scr, state_scr, y_scr, conv_scr, *, L):
    step = pl.program_id(1)

    @pl.when(step == 0)
    def _():
        ext_scr[0:SSD_HALO, :] = jnp.zeros((SSD_HALO, ext_scr.shape[1]), F32)
        uext_scr[0:CONF_HALO, :] = jnp.zeros((CONF_HALO, uext_scr.shape[1]), F32)
        state_scr[...] = jnp.zeros_like(state_scr)

    nx = xs_ref.shape[1]
    nb = b_ref.shape[1]
    ext_scr[SSD_HALO:SSD_HALO + L, 0:nx] = xs_ref[...].astype(F32)
    ext_scr[SSD_HALO:SSD_HALO + L, nx:nx + nb] = b_ref[...].astype(F32)
    ext_scr[SSD_HALO:SSD_HALO + L, nx + nb:nx + 2 * nb] = c_ref[...].astype(F32)
    for j in range(ext_scr.shape[1] // CONV_COLS):
        cs = slice(j * CONV_COLS, (j + 1) * CONV_COLS)
        acc = jnp.broadcast_to(cb_ref[:, cs], (L, CONV_COLS))
        for k in range(SSD_CONV):
            off = SSD_HALO - (SSD_CONV - 1) + k
            acc = acc + cw_ref[k:k + 1, cs] * ext_scr[off:off + L, cs]
        act_scr[:, cs] = _silu(acc)
    ext_scr[0:SSD_HALO, :] = ext_scr[L:L + SSD_HALO, :]

    dt = dt_ref[...]
    la = dt * a_ref[...]
    ri = lax.broadcasted_iota(jnp.int32, (L, L), 0)
    ci = lax.broadcasted_iota(jnp.int32, (L, L), 1)
    causal = ri >= ci
    tril = jnp.where(causal, 1.0, 0.0)
    triu = jnp.where(ri <= ci, 1.0, 0.0)
    acum = jnp.dot(tril, la, precision=HIGHEST, preferred_element_type=F32)
    acum_t = jnp.dot(la.T, triu, precision=HIGHEST, preferred_element_type=F32)
    acum_last = acum[L - 1:L, :]
    stacked = jnp.concatenate([dt, jnp.exp(acum_last - acum), jnp.exp(acum)], axis=0)
    s_hi = stacked.astype(BF16)
    s_lo = (stacked - s_hi.astype(F32)).astype(BF16)
    ex = (jnp.dot(s_hi, expand_ref[...], preferred_element_type=F32)
          + jnp.dot(s_lo, expand_ref[...], preferred_element_type=F32))
    dt_x = ex[0:L]
    dec_x = ex[L:2 * L]
    eac_x = ex[2 * L:3 * L]
    elast_x = eac_x[L - 1:L, :]

    for g in range(SSD_GROUPS):
        gs = slice(g * GROUP_DIM, (g + 1) * GROUP_DIM)
        bg = act_scr[:, nx + g * SSD_STATE:nx + (g + 1) * SSD_STATE]
        cg = act_scr[:, nx + nb + g * SSD_STATE:nx + nb + (g + 1) * SSD_STATE]
        bgb = bg.astype(BF16)
        cgb = cg.astype(BF16)
        cb = lax.dot_general(cgb, bgb, (((1,), (1,)), ((), ())), preferred_element_type=F32)
        xg = act_scr[:, gs]
        xdt = xg * dt_x[:, gs]
        xdtb = xdt.astype(BF16)
        yd = []
        for r in range(HEADS_PER_GROUP):
            hd = g * HEADS_PER_GROUP + r
            seg = acum[:, hd:hd + 1] - acum_t[hd:hd + 1, :]
            lm = jnp.exp(jnp.where(causal, seg, -jnp.inf))
            m = (cb * lm).astype(BF16)
            yd.append(jnp.dot(m, xdtb[:, r * SSD_HEAD_DIM:(r + 1) * SSD_HEAD_DIM], preferred_element_type=F32))
        st = state_scr[g]
        y = (jnp.concatenate(yd, axis=1)
             + jnp.dot(cgb, st.astype(BF16), preferred_element_type=F32) * eac_x[:, gs]
             + dsk_ref[:, gs] * xg)
        state_scr[g] = st * elast_x[:, gs] + jnp.dot(bg.T.astype(BF16), (xdt * dec_x[:, gs]).astype(BF16),
                                                    preferred_element_type=F32)
        yz = y * _silu(z_ref[:, gs].astype(F32))
        y_scr[:, gs] = _rms(yz, nw_ref[:, gs]).astype(BF16)
    y_a = jnp.dot(y_scr[...], wso_ref[...], preferred_element_type=F32)

    uext_scr[CONF_HALO:CONF_HALO + L, :] = glua_ref[...].astype(F32) * _sigmoid(glub_ref[...].astype(F32))
    for j in range(uext_scr.shape[1] // CONV_COLS):
        cs = slice(j * CONV_COLS, (j + 1) * CONV_COLS)
        for sft in range(1, 8):
            ushift_scr[sft - 1] = uext_scr[sft:sft + L + CONF_HALO - 8, cs]
        acc = jnp.broadcast_to(cdb_ref[:, cs], (L, CONV_COLS))
        for k in range(CONF_KERNEL):
            off = CONF_HALO - (CONF_KERNEL - 1) + k
            q8, sft = (off // 8) * 8, off % 8
            tap = uext_scr[q8:q8 + L, cs] if sft == 0 else ushift_scr[sft - 1, q8:q8 + L, :]
            acc = acc + cdw_ref[k:k + 1, cs] * tap
        conv_scr[:, cs] = acc
    uext_scr[0:CONF_HALO, :] = uext_scr[L:L + CONF_HALO, :]
    u = conv_scr[...]
    mu = jnp.mean(u, axis=-1, keepdims=True)
    uc = u - mu
    un = uc * lax.rsqrt(jnp.mean(uc * uc, axis=-1, keepdims=True) + EPS) * lnw_ref[...] + lnb_ref[...]
    y_b = jnp.dot(_silu(un).astype(BF16), wco_ref[...], preferred_element_type=F32) + bco_ref[...]

    merged = _sigmoid(ga_ref[...].astype(F32)) * y_a + _sigmoid(gb_ref[...].astype(F32)) * y_b
    h_ref[...] = x_ref[...] + jnp.dot(merged.astype(BF16), wo_ref[...], preferred_element_type=F32)


def token_mixers(x2d, proj, dt, conv_w, conv_b, a_pad, expand, dskip_x, ssd_norm_w, w_ssd_out,
                 conv_dw_w, conv_dw_b, ln_w, ln_b, w_conv_out, b_conv_out, w_o, *, batch, L):
    n, d = x2d.shape
    spb = n // batch // L
    row = lambda b, c: b * spb + c
    col = lambda k, w=1: pl.BlockSpec((L, w * d), lambda b, c, k=k: (row(b, c), k))
    full = lambda a: pl.BlockSpec(a.shape, lambda b, c: (0,) * a.ndim)
    consts = [conv_w, conv_b, a_pad, expand, dskip_x, ssd_norm_w, w_ssd_out,
              conv_dw_w, conv_dw_b, ln_w, ln_b, w_conv_out, b_conv_out, w_o]
    nxbc = conv_w.shape[1]
    return pl.pallas_call(
        functools.partial(_mixer_body, L=L),
        grid=(batch, spb),
        in_specs=[pl.BlockSpec((L, d), lambda b, c: (row(b, c), 0)),
                  col(0, 2),
                  col(1, 2),
                  col(4), col(5),
                  col(6), col(7),
                  col(8), col(9),
                  pl.BlockSpec((L, LANES), lambda b, c: (row(b, c), 0))] + [full(a) for a in consts],
        out_specs=pl.BlockSpec((L, d), lambda b, c: (row(b, c), 0)),
        out_shape=jax.ShapeDtypeStruct((n, d), F32),
        scratch_shapes=[pltpu.VMEM((L + SSD_HALO, nxbc), F32),
                        pltpu.VMEM((L, nxbc), F32),
                        pltpu.VMEM((L + CONF_HALO, d), F32),
                        pltpu.VMEM((7, L + CONF_HALO - 8, CONV_COLS), F32),
                        pltpu.VMEM((SSD_GROUPS, SSD_STATE, GROUP_DIM), F32),
                        pltpu.VMEM((L, D_INNER), BF16),
                        pltpu.VMEM((L, d), F32)],
        compiler_params=pltpu.CompilerParams(dimension_semantics=("arbitrary", "arbitrary"),
                                             vmem_limit_bytes=56 * 1024 * 1024),
        name="token_mixers",
    )(x2d, proj, proj, proj, proj, proj, proj, proj, proj, dt, *consts)


NK = PEER_N_KEYS
TOPK = PEER_TOPK
NEG = float("-inf")


def _top16_rows(s, nrows):
    iota = lax.broadcasted_iota(jnp.int32, s.shape, 0)
    vals, idxs = [], []
    for _ in range(TOPK):
        m = jnp.max(s, axis=0, keepdims=True)
        idx = jnp.min(jnp.where(s == m, iota, nrows), axis=0, keepdims=True)
        vals.append(m)
        idxs.append(idx)
        s = jnp.where(iota == idx, NEG, s)
    return jnp.concatenate(vals, axis=0), jnp.concatenate(idxs, axis=0)


def _gelu(x):
    return 0.5 * x * (1.0 + lax.erf(x * np.float32(0.7071067811865476)))


def _pair_top16(sv0, si0, sv1, si1):
    t = sv0.shape[1]
    iota8 = lax.broadcasted_iota(jnp.int32, (8, t), 0)
    pv = [sv0[0:1] + sv1, sv0[1:2] + sv1[0:8]]
    pe = [si0[0:1] * NK + si1, si0[1:2] * NK + si1[0:8]]
    for p, n in ((2, 5), (3, 4), (4, 3), (5, 2), (6, 2), (7, 2)):
        pv.append(jnp.where(iota8 < n, sv0[p:p + 1] + sv1[0:8], NEG))
        pe.append(si0[p:p + 1] * NK + si1[0:8])
    pv.append(sv0[8:16] + sv1[0:1])
    pe.append(si0[8:16] * NK + si1[0:1])
    cand = jnp.concatenate(pv, axis=0)
    cande = jnp.concatenate(pe, axis=0)
    nrows = cand.shape[0]
    iota = lax.broadcasted_iota(jnp.int32, cand.shape, 0)
    best, experts = [], []
    for _ in range(TOPK):
        m = jnp.max(cand, axis=0, keepdims=True)
        idx = jnp.min(jnp.where(cand == m, iota, nrows), axis=0, keepdims=True)
        sel = iota == idx
        experts.append(jnp.sum(jnp.where(sel, cande, 0), axis=0, keepdims=True))
        best.append(m)
        cand = jnp.where(sel, NEG, cand)
    best = jnp.concatenate(best, axis=0)
    ex = jnp.exp(best - best[0:1])
    return jnp.concatenate(experts, axis=0), ex / jnp.sum(ex, axis=0, keepdims=True)


def _peer_body(hn_ref, nw_ref, wqt_ref, keys_ref, ut_ref, v_ref, o_ref,
               xnt_nxt, et_nxt, gt_nxt, xn_cur, e_cur, g_cur, s_scr, sv0_scr, si0_scr,
               act_scr, sc0_scr, sc1_scr, stg0_scr, stg1_scr, w3_scr, *, nblk, tile):
    i = pl.program_id(0)
    k = pl.program_id(1)
    eb = ut_ref.shape[1]
    cpb = eb // NK
    nsel = e_cur.shape[1]
    hb = eb // 2
    cph = cpb // 2

    def key_scores(c):
        q = jnp.dot(wqt_ref[c * NK:(c + 1) * NK, :], xnt_nxt[...], preferred_element_type=F32)
        return jnp.dot(keys_ref[c], q.astype(BF16), preferred_element_type=F32)

    def route_piece(c):
        v, idx = _top16_rows(s_scr[...], NK)
        if c == 0:
            sv0_scr[...] = v
            si0_scr[...] = idx
        else:
            experts, gate = _pair_top16(sv0_scr[...], si0_scr[...], v, idx)
            r0 = pl.multiple_of((k // 2) * TOPK, TOPK)
            et_nxt[pl.ds(r0, TOPK), :] = experts
            gt_nxt[pl.ds(r0, TOPK), :] = gate
        s_scr[...] = key_scores(1 - c)

    @pl.when(k == 0)
    def _():
        @pl.when(i == 0)
        def _():
            xnt_nxt[...] = jnp.zeros_like(xnt_nxt)
            et_nxt[...] = jnp.zeros_like(et_nxt)
            gt_nxt[...] = jnp.zeros_like(gt_nxt)
            sc1_scr[...] = jnp.zeros_like(sc1_scr)

        xn_cur[...] = xnt_nxt[...].T
        e_cur[...] = et_nxt[...].T
        g_cur[...] = gt_nxt[...].T
        xn = _rms(hn_ref[...], nw_ref[...])
        xnt_nxt[...] = xn.astype(BF16).T
        s_scr[...] = key_scores(0)
        act_scr[...] = jnp.zeros_like(act_scr)

    def score_half(sc_ref, half):
        sc_ref[...] = jnp.dot(xn_cur[...], ut_ref[:, half * hb:(half + 1) * hb], preferred_element_type=F32)

    def pick_half(sc_ref, chunk0):
        e = e_cur[...]
        row = e >> 7
        col = e & (NK - 1)
        act = act_scr[...]
        for cc in range(cph):
            picked = jnp.take_along_axis(sc_ref[:, cc * NK:(cc + 1) * NK], col, axis=1)
            act = jnp.where(row == chunk0 + cc, picked, act)
        act_scr[...] = act

    def scatter_weights():
        act_scr[...] = g_cur[...] * _gelu(act_scr[...])
        iota = lax.broadcasted_iota(jnp.int32, (NK, nsel), 0).astype(F32).astype(BF16)
        one = jnp.ones((NK, nsel), BF16)
        zero = jnp.zeros((NK, nsel), BF16)

        def scatter_group(grp, stage_ref):
            t0 = pl.multiple_of(grp * 16, 16)
            e_rows = e_cur[pl.ds(t0, 16), :]
            i1_rows = (e_rows >> 7).astype(F32).astype(BF16)
            i2_rows = (e_rows & (NK - 1)).astype(F32).astype(BF16)
            w_rows = act_scr[pl.ds(t0, 16), :].astype(BF16)
            for j in range(16):
                pm = jnp.where(iota == i1_rows[j:j + 1], one, zero)
                qm = jnp.where(iota == i2_rows[j:j + 1], jnp.broadcast_to(w_rows[j:j + 1], (NK, nsel)), zero)
                stage_ref[j] = lax.dot_general(pm, qm, (((1,), (1,)), ((), ())),
                                               preferred_element_type=F32).astype(BF16)

        def swap_group(grp, stage_ref):
            t0 = pl.multiple_of(grp * 16, 16)
            w3_scr[:, pl.ds(t0, 16), :] = jnp.swapaxes(stage_ref[...], 0, 1)

        ngrp = tile // 16
        scatter_group(0, stg0_scr)

        def pair(i2, carry):
            scatter_group(2 * i2 + 1, stg1_scr)
            swap_group(2 * i2, stg0_scr)
            scatter_group(2 * i2 + 2, stg0_scr)
            swap_group(2 * i2 + 1, stg1_scr)
            return carry

        lax.fori_loop(0, ngrp // 2 - 1, pair, 0)
        scatter_group(ngrp - 1, stg1_scr)
        swap_group(ngrp - 2, stg0_scr)
        swap_group(ngrp - 1, stg1_scr)

    def weights_times_values():
        kk = k - nblk
        parts = [w3_scr[kk * cpb + cc] for cc in range(cpb)]
        return jnp.dot(jnp.concatenate(parts, axis=1), v_ref[...], preferred_element_type=F32)

    for c in range(2):
        @pl.when(jnp.logical_and(k < nblk, k % 2 == c))
        def _(c=c):
            route_piece(c)
            score_half(sc0_scr, 0)
            pick_half(sc1_scr, (k - 1) * cpb + cph)
            score_half(sc1_scr, 1)
            pick_half(sc0_scr, k * cpb)

    @pl.when(k == nblk)
    def _():
        pick_half(sc1_scr, (k - 1) * cpb + cph)
        scatter_weights()
        route_piece(nblk % 2)
        o_ref[...] = weights_times_values()

    for c in range(2):
        @pl.when(jnp.logical_and(k > nblk, k % 2 == c))
        def _(c=c):
            route_piece(c)
            o_ref[...] += weights_times_values()


def peer_mixer(h2d, norm_w, wqt, keys_hc, ut, v, *, tile, eb):
    n, d = h2d.shape
    heads = keys_hc.shape[0] // 2
    nsel = heads * TOPK
    ne = ut.shape[1]
    nblk = ne // eb
    assert nblk == heads, "one half-head is routed per grid step, so steps per tile = 2 * heads"
    nt = n // tile
    return pl.pallas_call(
        functools.partial(_peer_body, nblk=nblk, tile=tile),
        grid=(nt + 1, 2 * nblk),
        in_specs=[pl.BlockSpec((tile, d), lambda i, k: (jnp.minimum(i, nt - 1), 0)),
                  pl.BlockSpec((1, d), lambda i, k: (0, 0)),
                  pl.BlockSpec((2 * NK, d), lambda i, k: (jnp.minimum((k + 1) // 2, heads - 1), 0)),
                  pl.BlockSpec((2, NK, NK), lambda i, k: (jnp.minimum((k + 1) // 2, heads - 1), 0, 0)),
                  pl.BlockSpec((d, eb), lambda i, k: (0, jnp.minimum(k, nblk - 1))),
                  pl.BlockSpec((eb, d), lambda i, k: (jnp.maximum(k - nblk, 0), 0))],
        out_specs=pl.BlockSpec((tile, d), lambda i, k: (jnp.maximum(i - 1, 0), 0)),
        out_shape=jax.ShapeDtypeStruct((n, d), F32),
        scratch_shapes=[pltpu.VMEM((d, tile), BF16),
                        pltpu.VMEM((nsel, tile), jnp.int32),
                        pltpu.VMEM((nsel, tile), F32),
                        pltpu.VMEM((tile, d), BF16),
                        pltpu.VMEM((tile, nsel), jnp.int32),
                        pltpu.VMEM((tile, nsel), F32),
                        pltpu.VMEM((NK, tile), F32),
                        pltpu.VMEM((TOPK, tile), F32),
                        pltpu.VMEM((TOPK, tile), jnp.int32),
                        pltpu.VMEM((tile, nsel), F32),
                        pltpu.VMEM((tile, eb // 2), F32),
                        pltpu.VMEM((tile, eb // 2), F32),
                        pltpu.VMEM((16, NK, NK), BF16),
                        pltpu.VMEM((16, NK, NK), BF16),
                        pltpu.VMEM((ne // NK, tile, NK), BF16)],
        compiler_params=pltpu.CompilerParams(dimension_semantics=("arbitrary", "arbitrary"),
                                             vmem_limit_bytes=56 * 1024 * 1024),
        name="peer_mixer",
    )(h2d, norm_w.reshape(1, d), wqt, keys_hc, ut, v)


def _ple_body(h_ref, dpeer_ref, p_ref, nw_ref, wg_ref, wp_ref, fw_ref, o_ref):
    h = h_ref[...] + dpeer_ref[...]
    gate = _sigmoid(jnp.dot(_rms(h, nw_ref[...]).astype(BF16), wg_ref[...], preferred_element_type=F32))
    h = h + gate * jnp.dot(p_ref[...].astype(BF16), wp_ref[...], preferred_element_type=F32)
    o_ref[...] = _rms(h, fw_ref[...])


def ple_final(h2d, dpeer, p2d, norm_w, w_gate, w_proj, final_w, *, tm):
    n, d = h2d.shape
    pd = p2d.shape[1]
    return pl.pallas_call(
        _ple_body,
        grid=(n // tm,),
        in_specs=[pl.BlockSpec((tm, d), lambda i: (i, 0)),
                  pl.BlockSpec((tm, d), lambda i: (i, 0)),
                  pl.BlockSpec((tm, pd), lambda i: (i, 0)),
                  pl.BlockSpec((1, d), lambda i: (0, 0)),
                  pl.BlockSpec((d, d), lambda i: (0, 0)),
                  pl.BlockSpec((pd, d), lambda i: (0, 0)),
                  pl.BlockSpec((1, d), lambda i: (0, 0))],
        out_specs=pl.BlockSpec((tm, d), lambda i: (i, 0)),
        out_shape=jax.ShapeDtypeStruct((n, d), F32),
        compiler_params=pltpu.CompilerParams(dimension_semantics=("arbitrary",),
                                             vmem_limit_bytes=40 * 1024 * 1024),
        name="ple_final",
    )(h2d, dpeer, p2d, norm_w.reshape(1, d), w_gate, w_proj, final_w.reshape(1, d))


def kernel(x, p, norm_mix_w, w_in, conv_ssd_w, conv_ssd_b, dt_bias, a_log, d_skip,
           ssd_norm_w, w_ssd_out, conv_dw_w, conv_dw_b, conv_ln_w, conv_ln_b,
           w_conv_out, b_conv_out, w_o, norm_ffn_w, peer_wq, peer_keys, peer_u, peer_v,
           norm_ple_w, w_ple_gate, w_ple_proj, final_norm_w):
    bsz, s, d = x.shape
    x2d = x.reshape(bsz * s, d)
    i = 0
    r1 = lambda v: v.reshape(1, -1)

    col_xbc = D_INNER + conv_ssd_w.shape[2]
    col_dt = col_xbc + SSD_HEADS
    w_main = jnp.concatenate([w_in[i][:, :col_xbc], w_in[i][:, col_dt:]], axis=1).astype(BF16)
    w_dt = jnp.pad(w_in[i][:, col_xbc:col_dt], ((0, 0), (0, LANES - SSD_HEADS))).astype(BF16)
    b_dt = jnp.pad(dt_bias[i], (0, LANES - SSD_HEADS)).reshape(1, LANES)
    a_pad = jnp.pad(-jnp.exp(a_log[i]), (0, LANES - SSD_HEADS)).reshape(1, LANES)
    dskip_x = jnp.repeat(d_skip[i], SSD_HEAD_DIM).reshape(1, D_INNER)
    expand = (jnp.arange(D_INNER)[None, :] // SSD_HEAD_DIM == jnp.arange(LANES)[:, None]).astype(BF16)

    proj, dt = in_projection(x2d, norm_mix_w[i], w_main, w_dt, b_dt, tm=INPROJ_TM, tn=INPROJ_TN)
    h2d = token_mixers(x2d, proj, dt, conv_ssd_w[i], r1(conv_ssd_b[i]), a_pad, expand, dskip_x, r1(ssd_norm_w[i]),
                       w_ssd_out[i].astype(BF16), conv_dw_w[i], r1(conv_dw_b[i]), r1(conv_ln_w[i]),
                       r1(conv_ln_b[i]), w_conv_out[i].astype(BF16), r1(b_conv_out[i]), w_o[i].astype(BF16),
                       batch=bsz, L=MIX_L)

    wqt = peer_wq[i].T.astype(BF16)
    keys_hc = peer_keys[i].reshape(PEER_HEADS * 2, PEER_N_KEYS, PEER_HALF).astype(BF16)
    dpeer = peer_mixer(h2d, norm_ffn_w[i], wqt, keys_hc, peer_u[i].T.astype(BF16), peer_v[i].astype(BF16),
                       tile=PEER_TILE, eb=PEER_EXPERT_BLOCK)

    out = ple_final(h2d, dpeer, p[i].reshape(bsz * s, -1), norm_ple_w[i], w_ple_gate[i].astype(BF16),
                    w_ple_proj[i].astype(BF16), final_norm_w, tm=PLE_TM)
    return out.reshape(bsz, s, d)
```

```python
import functools
import jax
import jax.numpy as jnp
from jax import lax
import numpy as np
from jax.experimental import pallas as pl
from jax.experimental.pallas import tpu as pltpu

D_MODEL = 1024
D_INNER = 2 * D_MODEL
SSD_HEAD_DIM = 64
SSD_HEADS = D_INNER // SSD_HEAD_DIM
SSD_GROUPS = 8
SSD_STATE = 128
SSD_CONV = 4
HEADS_PER_GROUP = SSD_HEADS // SSD_GROUPS
GROUP_DIM = D_INNER // SSD_GROUPS
CONF_KERNEL = 31
PEER_HEADS = 8
PEER_N_KEYS = 128
PEER_TOPK = 16
PEER_HALF = 128
EPS = 1e-6
F32 = jnp.float32
BF16 = jnp.bfloat16
LANES = 128
HIGHEST = lax.Precision.HIGHEST

INPROJ_TM = 1024
INPROJ_TN = 1024
MIX_L = 256
PEER_TILE = 512
PEER_EXPERT_BLOCK = 2048
PLE_TM = 512


def _sigmoid(x):
    return 1.0 / (1.0 + jnp.exp(-x))


def _silu(x):
    return x * _sigmoid(x)


def _rms(x, w):
    return x * lax.rsqrt(jnp.mean(x * x, axis=-1, keepdims=True) + EPS) * w


def _inproj_body(x_ref, nw_ref, w_ref, wdt_ref, bdt_ref, o_ref, dt_ref, hn_scr):
    @pl.when(pl.program_id(1) == 0)
    def _():
        hn = _rms(x_ref[...], nw_ref[...]).astype(BF16)
        hn_scr[...] = hn
        v = jnp.dot(hn, wdt_ref[...], preferred_element_type=F32) + bdt_ref[...]
        dt_ref[...] = jnp.maximum(v, 0.0) + jnp.log(1.0 + jnp.exp(-jnp.abs(v)))

    o_ref[...] = jnp.dot(hn_scr[...], w_ref[...], preferred_element_type=F32).astype(BF16)


def in_projection(x2d, norm_w, w_main, w_dt, b_dt, *, tm, tn):
    n, d = x2d.shape
    c = w_main.shape[1]
    return pl.pallas_call(
        _inproj_body,
        grid=(n // tm, c // tn),
        in_specs=[pl.BlockSpec((tm, d), lambda i, j: (i, 0)),
                  pl.BlockSpec((1, d), lambda i, j: (0, 0)),
                  pl.BlockSpec((d, tn), lambda i, j: (0, j)),
                  pl.BlockSpec((d, LANES), lambda i, j: (0, 0)),
                  pl.BlockSpec((1, LANES), lambda i, j: (0, 0))],
        out_specs=[pl.BlockSpec((tm, tn), lambda i, j: (i, j)),
                   pl.BlockSpec((tm, LANES), lambda i, j: (i, 0))],
        out_shape=[jax.ShapeDtypeStruct((n, c), BF16), jax.ShapeDtypeStruct((n, LANES), F32)],
        scratch_shapes=[pltpu.VMEM((tm, d), BF16)],
        compiler_params=pltpu.CompilerParams(dimension_semantics=("arbitrary", "arbitrary"),
                                             vmem_limit_bytes=40 * 1024 * 1024),
        name="in_projection",
    )(x2d, norm_w.reshape(1, d), w_main, w_dt, b_dt)


SSD_HALO = 8
CONF_HALO = 32
CONV_COLS = 512


def _mixer_body(x_ref, z_ref, xs_ref, b_ref, c_ref, glua_ref, glub_ref, ga_ref, gb_ref, dt_ref,
                cw_ref, cb_ref, a_ref, expand_ref, dsk_ref, nw_ref, wso_ref,
                cdw_ref, cdb_ref, lnw_ref, lnb_ref, wco_ref, bco_ref, wo_ref,
                h_ref,
                ext_scr, act_scr, uext_scr, ushift_scr, state_scr, y_scr, conv_scr, *, L):
    step = pl.program_id(1)

    @pl.when(step == 0)
    def _():
        ext_scr[0:SSD_HALO, :] = jnp.zeros((SSD_HALO, ext_scr.shape[1]), F32)
        uext_scr[0:CONF_HALO, :] = jnp.zeros((CONF_HALO, uext_scr.shape[1]), F32)
        state_scr[...] = jnp.zeros_like(state_scr)

    nx = xs_ref.shape[1]
    nb = b_ref.shape[1]
    ext_scr[SSD_HALO:SSD_HALO + L, 0:nx] = xs_ref[...].astype(F32)
    ext_scr[SSD_HALO:SSD_HALO + L, nx:nx + nb] = b_ref[...].astype(F32)
    ext_scr[SSD_HALO:SSD_HALO + L, nx + nb:nx + 2 * nb] = c_ref[...].astype(F32)
    for j in range(ext_scr.shape[1] // CONV_COLS):
        cs = slice(j * CONV_COLS, (j + 1) * CONV_COLS)
        acc = jnp.broadcast_to(cb_ref[:, cs], (L, CONV_COLS))
        for k in range(SSD_CONV):
            off = SSD_HALO - (SSD_CONV - 1) + k
            acc = acc + cw_ref[k:k + 1, cs] * ext_scr[off:off + L, cs]
        act_scr[:, cs] = _silu(acc)
    ext_scr[0:SSD_HALO, :] = ext_scr[L:L + SSD_HALO, :]

    dt = dt_ref[...]
    la = dt * a_ref[...]
    ri = lax.broadcasted_iota(jnp.int32, (L, L), 0)
    ci = lax.broadcasted_iota(jnp.int32, (L, L), 1)
    causal = ri >= ci
    tril = jnp.where(causal, 1.0, 0.0)
    triu = jnp.where(ri <= ci, 1.0, 0.0)
    acum = jnp.dot(tril, la, precision=HIGHEST, preferred_element_type=F32)
    acum_t = jnp.dot(la.T, triu, precision=HIGHEST, preferred_element_type=F32)
    acum_last = acum[L - 1:L, :]
    stacked = jnp.concatenate([dt, jnp.exp(acum_last - acum), jnp.exp(acum)], axis=0)
    s_hi = stacked.astype(BF16)
    s_lo = (stacked - s_hi.astype(F32)).astype(BF16)
    ex = (jnp.dot(s_hi, expand_ref[...], preferred_element_type=F32)
          + jnp.dot(s_lo, expand_ref[...], preferred_element_type=F32))
    dt_x = ex[0:L]
    dec_x = ex[L:2 * L]
    eac_x = ex[2 * L:3 * L]
    elast_x = eac_x[L - 1:L, :]

    for g in range(SSD_GROUPS):
        gs = slice(g * GROUP_DIM, (g + 1) * GROUP_DIM)
        bg = act_scr[:, nx + g * SSD_STATE:nx + (g + 1) * SSD_STATE]
        cg = act_scr[:, nx + nb + g * SSD_STATE:nx + nb + (g + 1) * SSD_STATE]
        bgb = bg.astype(BF16)
        cgb = cg.astype(BF16)
        cb = lax.dot_general(cgb, bgb, (((1,), (1,)), ((), ())), preferred_element_type=F32)
        xg = act_scr[:, gs]
        xdt = xg * dt_x[:, gs]
        xdtb = xdt.astype(BF16)
        yd = []
        for r in range(HEADS_PER_GROUP):
            hd = g * HEADS_PER_GROUP + r
            seg = acum[:, hd:hd + 1] - acum_t[hd:hd + 1, :]
            lm = jnp.exp(jnp.where(causal, seg, -jnp.inf))
            m = (cb * lm).astype(BF16)
            yd.append(jnp.dot(m, xdtb[:, r * SSD_HEAD_DIM:(r + 1) * SSD_HEAD_DIM], preferred_element_type=F32))
        st = state_scr[g]
        y = (jnp.concatenate(yd, axis=1)
             + jnp.dot(cgb, st.astype(BF16), preferred_element_type=F32) * eac_x[:, gs]
             + dsk_ref[:, gs] * xg)
        state_scr[g] = st * elast_x[:, gs] + jnp.dot(bg.T.astype(BF16), (xdt * dec_x[:, gs]).astype(BF16),
                                                    preferred_element_type=F32)
        yz = y * _silu(z_ref[:, gs].astype(F32))
        y_scr[:, gs] = _rms(yz, nw_ref[:, gs]).astype(BF16)
    y_a = jnp.dot(y_scr[...], wso_ref[...], preferred_element_type=F32)

    uext_scr[CONF_HALO:CONF_HALO + L, :] = glua_ref[...].astype(F32) * _sigmoid(glub_ref[...].astype(F32))
    for j in range(uext_scr.shape[1] // CONV_COLS):
        cs = slice(j * CONV_COLS, (j + 1) * CONV_COLS)
        for sft in range(1, 8):
            ushift_scr[sft - 1] = uext_scr[sft:sft + L + CONF_HALO - 8, cs]
        acc = jnp.broadcast_to(cdb_ref[:, cs], (L, CONV_COLS))
        for k in range(CONF_KERNEL):
            off = CONF_HALO - (CONF_KERNEL - 1) + k
            q8, sft = (off // 8) * 8, off % 8
            tap = uext_scr[q8:q8 + L, cs] if sft == 0 else ushift_scr[sft - 1, q8:q8 + L, :]
            acc = acc + cdw_ref[k:k + 1, cs] * tap
        conv_scr[:, cs] = acc
    uext_scr[0:CONF_HALO, :] = uext_scr[L:L + CONF_HALO, :]
    u = conv_scr[...]
    mu = jnp.mean(u, axis=-1, keepdims=True)
    uc = u - mu
    un = uc * lax.rsqrt(jnp.mean(uc * uc, axis=-1, keepdims=True) + EPS) * lnw_ref[...] + lnb_ref[...]
    y_b = jnp.dot(_silu(un).astype(BF16), wco_ref[...], preferred_element_type=F32) + bco_ref[...]

    merged = _sigmoid(ga_ref[...].astype(F32)) * y_a + _sigmoid(gb_ref[...].astype(F32)) * y_b
    h_ref[...] = x_ref[...] + jnp.dot(merged.astype(BF16), wo_ref[...], preferred_element_type=F32)


def token_mixers(x2d, proj, dt, conv_w, conv_b, a_pad, expand, dskip_x, ssd_norm_w, w_ssd_out,
                 conv_dw_w, conv_dw_b, ln_w, ln_b, w_conv_out, b_conv_out, w_o, *, batch, L):
    n, d = x2d.shape
    spb = n // batch // L
    row = lambda b, c: b * spb + c
    col = lambda k, w=1: pl.BlockSpec((L, w * d), lambda b, c, k=k: (row(b, c), k))
    full = lambda a: pl.BlockSpec(a.shape, lambda b, c: (0,) * a.ndim)
    consts = [conv_w, conv_b, a_pad, expand, dskip_x, ssd_norm_w, w_ssd_out,
              conv_dw_w, conv_dw_b, ln_w, ln_b, w_conv_out, b_conv_out, w_o]
    nxbc = conv_w.shape[1]
    return pl.pallas_call(
        functools.partial(_mixer_body, L=L),
        grid=(batch, spb),
        in_specs=[pl.BlockSpec((L, d), lambda b, c: (row(b, c), 0)),
                  col(0, 2),
                  col(1, 2),
                  col(4), col(5),
                  col(6), col(7),
                  col(8), col(9),
                  pl.BlockSpec((L, LANES), lambda b, c: (row(b, c), 0))] + [full(a) for a in consts],
        out_specs=pl.BlockSpec((L, d), lambda b, c: (row(b, c), 0)),
        out_shape=jax.ShapeDtypeStruct((n, d), F32),
        scratch_shapes=[pltpu.VMEM((L + SSD_HALO, nxbc), F32),
                        pltpu.VMEM((L, nxbc), F32),
                        pltpu.VMEM((L + CONF_HALO, d), F32),
                        pltpu.VMEM((7, L + CONF_HALO - 8, CONV_COLS), F32),
                        pltpu.VMEM((SSD_GROUPS, SSD_STATE, GROUP_DIM), F32),
                        pltpu.VMEM((L, D_INNER), BF16),
                        pltpu.VMEM((L, d), F32)],
        compiler_params=pltpu.CompilerParams(dimension_semantics=("arbitrary", "arbitrary"),
                                             vmem_limit_bytes=56 * 1024 * 1024),
        name="token_mixers",
    )(x2d, proj, proj, proj, proj, proj, proj, proj, proj, dt, *consts)


NK = PEER_N_KEYS
TOPK = PEER_TOPK
NEG = float("-inf")


def _sort16_network():
    pairs, p = [], 1
    while p < 16:
        k = p
        while k >= 1:
            for j in range(k % p, 16 - k, 2 * k):
                for i in range(min(k, 16 - j - k)):
                    if (i + j) // (2 * p) == (i + j + k) // (2 * p):
                        pairs.append((i + j, i + j + k))
            k //= 2
        p *= 2
    return pairs


def _top16_rows(s, nrows):
    assert nrows == 16 * 8
    t = s.shape[1]
    iota8 = lax.broadcasted_iota(jnp.int32, (8, t), 0)
    vals = [s[8 * i:8 * i + 8] for i in range(16)]
    rows = [iota8 + 8 * i for i in range(16)]
    for a, b in _sort16_network():
        swap = (vals[b] > vals[a]) | ((vals[b] == vals[a]) & (rows[b] < rows[a]))
        vals[a], vals[b] = jnp.where(swap, vals[b], vals[a]), jnp.where(swap, vals[a], vals[b])
        rows[a], rows[b] = jnp.where(swap, rows[b], rows[a]), jnp.where(swap, rows[a], rows[b])
    top_v, top_r = [], []
    for r in range(TOPK):
        m = jnp.max(vals[0], axis=0, keepdims=True)
        row = jnp.min(jnp.where(vals[0] == m, rows[0], nrows), axis=0, keepdims=True)
        top_v.append(m)
        top_r.append(row)
        pop = rows[0] == row
        for j in range(TOPK - 1 - r):
            vals[j] = jnp.where(pop, vals[j + 1], vals[j])
            rows[j] = jnp.where(pop, rows[j + 1], rows[j])
    return jnp.concatenate(top_v, axis=0), jnp.concatenate(top_r, axis=0)


def _gelu(x):
    return 0.5 * x * (1.0 + lax.erf(x * np.float32(0.7071067811865476)))


def _pair_top16(sv0, si0, sv1, si1):
    t = sv0.shape[1]
    iota8 = lax.broadcasted_iota(jnp.int32, (8, t), 0)
    pv = [sv0[0:1] + sv1, sv0[1:2] + sv1[0:8]]
    pe = [si0[0:1] * NK + si1, si0[1:2] * NK + si1[0:8]]
    for p, n in ((2, 5), (3, 4), (4, 3), (5, 2), (6, 2), (7, 2)):
        pv.append(jnp.where(iota8 < n, sv0[p:p + 1] + sv1[0:8], NEG))
        pe.append(si0[p:p + 1] * NK + si1[0:8])
    pv.append(sv0[8:16] + sv1[0:1])
    pe.append(si0[8:16] * NK + si1[0:1])
    cand = jnp.concatenate(pv, axis=0)
    cande = jnp.concatenate(pe, axis=0)
    nrows = cand.shape[0]
    iota = lax.broadcasted_iota(jnp.int32, cand.shape, 0)
    best, experts = [], []
    for _ in range(TOPK):
        m = jnp.max(cand, axis=0, keepdims=True)
        idx = jnp.min(jnp.where(cand == m, iota, nrows), axis=0, keepdims=True)
        sel = iota == idx
        experts.append(jnp.sum(jnp.where(sel, cande, 0), axis=0, keepdims=True))
        best.append(m)
        cand = jnp.where(sel, NEG, cand)
    best = jnp.concatenate(best, axis=0)
    ex = jnp.exp(best - best[0:1])
    return jnp.concatenate(experts, axis=0), ex / jnp.sum(ex, axis=0, keepdims=True)


def _peer_body(hn_ref, nw_ref, wqt_ref, keys_ref, ut_ref, v_ref, o_ref,
               xnt_nxt, et_nxt, gt_nxt, xn_cur, e_cur, g_cur, s_scr, sv0_scr, si0_scr,
               act_scr, sc0_scr, sc1_scr, stg0_scr, stg1_scr, w3_scr, *, nblk, tile):
    i = pl.program_id(0)
    k = pl.program_id(1)
    eb = ut_ref.shape[1]
    cpb = eb // NK
    nsel = e_cur.shape[1]
    hb = eb // 2
    cph = cpb // 2

    def key_scores(c):
        q = jnp.dot(wqt_ref[c * NK:(c + 1) * NK, :], xnt_nxt[...], preferred_element_type=F32)
        return jnp.dot(keys_ref[c], q.astype(BF16), preferred_element_type=F32)

    def route_piece(c):
        v, idx = _top16_rows(s_scr[...], NK)
        if c == 0:
            sv0_scr[...] = v
            si0_scr[...] = idx
        else:
            experts, gate = _pair_top16(sv0_scr[...], si0_scr[...], v, idx)
            r0 = pl.multiple_of((k // 2) * TOPK, TOPK)
            et_nxt[pl.ds(r0, TOPK), :] = experts
            gt_nxt[pl.ds(r0, TOPK), :] = gate
        s_scr[...] = key_scores(1 - c)

    @pl.when(k == 0)
    def _():
        @pl.when(i == 0)
        def _():
            xnt_nxt[...] = jnp.zeros_like(xnt_nxt)
            et_nxt[...] = jnp.zeros_like(et_nxt)
            gt_nxt[...] = jnp.zeros_like(gt_nxt)
            sc1_scr[...] = jnp.zeros_like(sc1_scr)

        xn_cur[...] = xnt_nxt[...].T
        e_cur[...] = et_nxt[...].T
        g_cur[...] = gt_nxt[...].T
        xn = _rms(hn_ref[...], nw_ref[...])
        xnt_nxt[...] = xn.astype(BF16).T
        s_scr[...] = key_scores(0)
        act_scr[...] = jnp.zeros_like(act_scr)

    def score_half(sc_ref, half):
        sc_ref[...] = jnp.dot(xn_cur[...], ut_ref[:, half * hb:(half + 1) * hb], preferred_element_type=F32)

    def pick_half(sc_ref, chunk0):
        e = e_cur[...]
        row = e >> 7
        col = e & (NK - 1)
        act = act_scr[...]
        for cc in range(cph):
            picked = jnp.take_along_axis(sc_ref[:, cc * NK:(cc + 1) * NK], col, axis=1)
            act = jnp.where(row == chunk0 + cc, picked, act)
        act_scr[...] = act

    def scatter_weights():
        act_scr[...] = g_cur[...] * _gelu(act_scr[...])
        iota = lax.broadcasted_iota(jnp.int32, (NK, nsel), 0).astype(F32).astype(BF16)
        one = jnp.ones((NK, nsel), BF16)
        zero = jnp.zeros((NK, nsel), BF16)

        def scatter_group(grp, stage_ref):
            t0 = pl.multiple_of(grp * 16, 16)
            e_rows = e_cur[pl.ds(t0, 16), :]
            i1_rows = (e_rows >> 7).astype(F32).astype(BF16)
            i2_rows = (e_rows & (NK - 1)).astype(F32).astype(BF16)
            w_rows = act_scr[pl.ds(t0, 16), :].astype(BF16)
            for j in range(16):
                pm = jnp.where(iota == i1_rows[j:j + 1], one, zero)
                qm = jnp.where(iota == i2_rows[j:j + 1], jnp.broadcast_to(w_rows[j:j + 1], (NK, nsel)), zero)
                stage_ref[j] = lax.dot_general(pm, qm, (((1,), (1,)), ((), ())),
                                               preferred_element_type=F32).astype(BF16)

        def swap_group(grp, stage_ref):
            t0 = pl.multiple_of(grp * 16, 16)
            w3_scr[:, pl.ds(t0, 16), :] = jnp.swapaxes(stage_ref[...], 0, 1)

        ngrp = tile // 16
        scatter_group(0, stg0_scr)

        def pair(i2, carry):
            scatter_group(2 * i2 + 1, stg1_scr)
            swap_group(2 * i2, stg0_scr)
            scatter_group(2 * i2 + 2, stg0_scr)
            swap_group(2 * i2 + 1, stg1_scr)
            return carry

        lax.fori_loop(0, ngrp // 2 - 1, pair, 0)
        scatter_group(ngrp - 1, stg1_scr)
        swap_group(ngrp - 2, stg0_scr)
        swap_group(ngrp - 1, stg1_scr)

    def weights_times_values():
        kk = k - nblk
        parts = [w3_scr[kk * cpb + cc] for cc in range(cpb)]
        return jnp.dot(jnp.concatenate(parts, axis=1), v_ref[...], preferred_element_type=F32)

    for c in range(2):
        @pl.when(jnp.logical_and(k < nblk, k % 2 == c))
        def _(c=c):
            route_piece(c)
            score_half(sc0_scr, 0)
            pick_half(sc1_scr, (k - 1) * cpb + cph)
            score_half(sc1_scr, 1)
            pick_half(sc0_scr, k * cpb)

    @pl.when(k == nblk)
    def _():
        pick_half(sc1_scr, (k - 1) * cpb + cph)
        scatter_weights()
        route_piece(nblk % 2)
        o_ref[...] = weights_times_values()

    for c in range(2):
        @pl.when(jnp.logical_and(k > nblk, k % 2 == c))
        def _(c=c):
            route_piece(c)
            o_ref[...] += weights_times_values()


def peer_mixer(h2d, norm_w, wqt, keys_hc, ut, v, *, tile, eb):
    n, d = h2d.shape
    heads = keys_hc.shape[0] // 2
    nsel = heads * TOPK
    ne = ut.shape[1]
    nblk = ne // eb
    assert nblk == heads, "one half-head is routed per grid step, so steps per tile = 2 * heads"
    nt = n // tile
    return pl.pallas_call(
        functools.partial(_peer_body, nblk=nblk, tile=tile),
        grid=(nt + 1, 2 * nblk),
        in_specs=[pl.BlockSpec((tile, d), lambda i, k: (jnp.minimum(i, nt - 1), 0)),
                  pl.BlockSpec((1, d), lambda i, k: (0, 0)),
                  pl.BlockSpec((2 * NK, d), lambda i, k: (jnp.minimum((k + 1) // 2, heads - 1), 0)),
                  pl.BlockSpec((2, NK, NK), lambda i, k: (jnp.minimum((k + 1) // 2, heads - 1), 0, 0)),
                  pl.BlockSpec((d, eb), lambda i, k: (0, jnp.minimum(k, nblk - 1))),
                  pl.BlockSpec((eb, d), lambda i, k: (jnp.maximum(k - nblk, 0), 0))],
        out_specs=pl.BlockSpec((tile, d), lambda i, k: (jnp.maximum(i - 1, 0), 0)),
        out_shape=jax.ShapeDtypeStruct((n, d), F32),
        scratch_shapes=[pltpu.VMEM((d, tile), BF16),
                        pltpu.VMEM((nsel, tile), jnp.int32),
                        pltpu.VMEM((nsel, tile), F32),
                        pltpu.VMEM((tile, d), BF16),
                        pltpu.VMEM((tile, nsel), jnp.int32),
                        pltpu.VMEM((tile, nsel), F32),
                        pltpu.VMEM((NK, tile), F32),
                        pltpu.VMEM((TOPK, tile), F32),
                        pltpu.VMEM((TOPK, tile), jnp.int32),
                        pltpu.VMEM((tile, nsel), F32),
                        pltpu.VMEM((tile, eb // 2), F32),
                        pltpu.VMEM((tile, eb // 2), F32),
                        pltpu.VMEM((16, NK, NK), BF16),
                        pltpu.VMEM((16, NK, NK), BF16),
                        pltpu.VMEM((ne // NK, tile, NK), BF16)],
        compiler_params=pltpu.CompilerParams(dimension_semantics=("arbitrary", "arbitrary"),
                                             vmem_limit_bytes=56 * 1024 * 1024),
        name="peer_mixer",
    )(h2d, norm_w.reshape(1, d), wqt, keys_hc, ut, v)


def _ple_body(h_ref, dpeer_ref, p_ref, nw_ref, wg_ref, wp_ref, fw_ref, o_ref):
    h = h_ref[...] + dpeer_ref[...]
    gate = _sigmoid(jnp.dot(_rms(h, nw_ref[...]).astype(BF16), wg_ref[...], preferred_element_type=F32))
    h = h + gate * jnp.dot(p_ref[...].astype(BF16), wp_ref[...], preferred_element_type=F32)
    o_ref[...] = _rms(h, fw_ref[...])


def ple_final(h2d, dpeer, p2d, norm_w, w_gate, w_proj, final_w, *, tm):
    n, d = h2d.shape
    pd = p2d.shape[1]
    return pl.pallas_call(
        _ple_body,
        grid=(n // tm,),
        in_specs=[pl.BlockSpec((tm, d), lambda i: (i, 0)),
                  pl.BlockSpec((tm, d), lambda i: (i, 0)),
                  pl.BlockSpec((tm, pd), lambda i: (i, 0)),
                  pl.BlockSpec((1, d), lambda i: (0, 0)),
                  pl.BlockSpec((d, d), lambda i: (0, 0)),
                  pl.BlockSpec((pd, d), lambda i: (0, 0)),
                  pl.BlockSpec((1, d), lambda i: (0, 0))],
        out_specs=pl.BlockSpec((tm, d), lambda i: (i, 0)),
        out_shape=jax.ShapeDtypeStruct((n, d), F32),
        compiler_params=pltpu.CompilerParams(dimension_semantics=("arbitrary",),
                                             vmem_limit_bytes=40 * 1024 * 1024),
        name="ple_final",
    )(h2d, dpeer, p2d, norm_w.reshape(1, d), w_gate, w_proj, final_w.reshape(1, d))


def kernel(x, p, norm_mix_w, w_in, conv_ssd_w, conv_ssd_b, dt_bias, a_log, d_skip,
           ssd_norm_w, w_ssd_out, conv_dw_w, conv_dw_b, conv_ln_w, conv_ln_b,
           w_conv_out, b_conv_out, w_o, norm_ffn_w, peer_wq, peer_keys, peer_u, peer_v,
           norm_ple_w, w_ple_gate, w_ple_proj, final_norm_w):
    bsz, s, d = x.shape
    x2d = x.reshape(bsz * s, d)
    i = 0
    r1 = lambda v: v.reshape(1, -1)

    col_xbc = D_INNER + conv_ssd_w.shape[2]
    col_dt = col_xbc + SSD_HEADS
    w_main = jnp.concatenate([w_in[i][:, :col_xbc], w_in[i][:, col_dt:]], axis=1).astype(BF16)
    w_dt = jnp.pad(w_in[i][:, col_xbc:col_dt], ((0, 0), (0, LANES - SSD_HEADS))).astype(BF16)
    b_dt = jnp.pad(dt_bias[i], (0, LANES - SSD_HEADS)).reshape(1, LANES)
    a_pad = jnp.pad(-jnp.exp(a_log[i]), (0, LANES - SSD_HEADS)).reshape(1, LANES)
    dskip_x = jnp.repeat(d_skip[i], SSD_HEAD_DIM).reshape(1, D_INNER)
    expand = (jnp.arange(D_INNER)[None, :] // SSD_HEAD_DIM == jnp.arange(LANES)[:, None]).astype(BF16)

    proj, dt = in_projection(x2d, norm_mix_w[i], w_main, w_dt, b_dt, tm=INPROJ_TM, tn=INPROJ_TN)
    h2d = token_mixers(x2d, proj, dt, conv_ssd_w[i], r1(conv_ssd_b[i]), a_pad, expand, dskip_x, r1(ssd_norm_w[i]),
                       w_ssd_out[i].astype(BF16), conv_dw_w[i], r1(conv_dw_b[i]), r1(conv_ln_w[i]),
                       r1(conv_ln_b[i]), w_conv_out[i].astype(BF16), r1(b_conv_out[i]), w_o[i].astype(BF16),
                       batch=bsz, L=MIX_L)

    wqt = peer_wq[i].T.astype(BF16)
    keys_hc = peer_keys[i].reshape(PEER_HEADS * 2, PEER_N_KEYS, PEER_HALF).astype(BF16)
    dpeer = peer_mixer(h2d, norm_ffn_w[i], wqt, keys_hc, peer_u[i].T.astype(BF16), peer_v[i].astype(BF16),
                       tile=PEER_TILE, eb=PEER_EXPERT_BLOCK)

    out = ple_final(h2d, dpeer, p[i].reshape(bsz * s, -1), norm_ple_w[i], w_ple_gate[i].astype(BF16),
                    w_ple_proj[i].astype(BF16), final_norm_w, tm=PLE_TM)
    return out.reshape(bsz, s, d)
```

```python
import functools
import jax
import jax.numpy as jnp
from jax import lax
import numpy as np
from jax.experimental import pallas as pl
from jax.experimental.pallas import tpu as pltpu

D_MODEL = 1024
D_INNER = 2 * D_MODEL
SSD_HEAD_DIM = 64
SSD_HEADS = D_INNER // SSD_HEAD_DIM
SSD_GROUPS = 8
SSD_STATE = 128
SSD_CONV = 4
HEADS_PER_GROUP = SSD_HEADS // SSD_GROUPS
GROUP_DIM = D_INNER // SSD_GROUPS
CONF_KERNEL = 31
PEER_HEADS = 8
PEER_N_KEYS = 128
PEER_TOPK = 16
PEER_HALF = 128
EPS = 1e-6
F32 = jnp.float32
BF16 = jnp.bfloat16
LANES = 128
HIGHEST = lax.Precision.HIGHEST

INPROJ_TM = 1024
INPROJ_TN = 1024
MIX_L = 256
PEER_TILE = 512
PEER_EXPERT_BLOCK = 2048
PLE_TM = 512


def _sigmoid(x):
    return 1.0 / (1.0 + jnp.exp(-x))


def _silu(x):
    return x * _sigmoid(x)


def _rms(x, w):
    return x * lax.rsqrt(jnp.mean(x * x, axis=-1, keepdims=True) + EPS) * w


def _inproj_body(x_ref, nw_ref, w_ref, wdt_ref, bdt_ref, o_ref, dt_ref, hn_scr):
    @pl.when(pl.program_id(1) == 0)
    def _():
        hn = _rms(x_ref[...], nw_ref[...]).astype(BF16)
        hn_scr[...] = hn
        v = jnp.dot(hn, wdt_ref[...], preferred_element_type=F32) + bdt_ref[...]
        dt_ref[...] = jnp.maximum(v, 0.0) + jnp.log(1.0 + jnp.exp(-jnp.abs(v)))

    o_ref[...] = jnp.dot(hn_scr[...], w_ref[...], preferred_element_type=F32).astype(BF16)


def in_projection(x2d, norm_w, w_main, w_dt, b_dt, *, tm, tn):
    n, d = x2d.shape
    c = w_main.shape[1]
    return pl.pallas_call(
        _inproj_body,
        grid=(n // tm, c // tn),
        in_specs=[pl.BlockSpec((tm, d), lambda i, j: (i, 0)),
                  pl.BlockSpec((1, d), lambda i, j: (0, 0)),
                  pl.BlockSpec((d, tn), lambda i, j: (0, j)),
                  pl.BlockSpec((d, LANES), lambda i, j: (0, 0)),
                  pl.BlockSpec((1, LANES), lambda i, j: (0, 0))],
        out_specs=[pl.BlockSpec((tm, tn), lambda i, j: (i, j)),
                   pl.BlockSpec((tm, LANES), lambda i, j: (i, 0))],
        out_shape=[jax.ShapeDtypeStruct((n, c), BF16), jax.ShapeDtypeStruct((n, LANES), F32)],
        scratch_shapes=[pltpu.VMEM((tm, d), BF16)],
        compiler_params=pltpu.CompilerParams(dimension_semantics=("arbitrary", "arbitrary"),
                                             vmem_limit_bytes=40 * 1024 * 1024),
        name="in_projection",
    )(x2d, norm_w.reshape(1, d), w_main, w_dt, b_dt)


SSD_HALO = 8
CONF_HALO = 32
CONV_COLS = 512


def _mixer_body(x_ref, z_ref, xs_ref, b_ref, c_ref, glua_ref, glub_ref, ga_ref, gb_ref, dt_ref,
                cw_ref, cb_ref, a_ref, expand_ref, dsk_ref, nw_ref, wso_ref,
                cdw_ref, cdb_ref, lnw_ref, lnb_ref, wco_ref, bco_ref, wo_ref,
                h_ref,
                ext_scr, act_scr, uext_scr, ushift_scr, state_scr, y_scr, conv_scr, *, L):
    step = pl.program_id(1)

    @pl.when(step == 0)
    def _():
        ext_scr[0:SSD_HALO, :] = jnp.zeros((SSD_HALO, ext_scr.shape[1]), F32)
        uext_scr[0:CONF_HALO, :] = jnp.zeros((CONF_HALO, uext_scr.shape[1]), F32)
        state_scr[...] = jnp.zeros_like(state_scr)

    nx = xs_ref.shape[1]
    nb = b_ref.shape[1]
    ext_scr[SSD_HALO:SSD_HALO + L, 0:nx] = xs_ref[...].astype(F32)
    ext_scr[SSD_HALO:SSD_HALO + L, nx:nx + nb] = b_ref[...].astype(F32)
    ext_scr[SSD_HALO:SSD_HALO + L, nx + nb:nx + 2 * nb] = c_ref[...].astype(F32)
    for j in range(ext_scr.shape[1] // CONV_COLS):
        cs = slice(j * CONV_COLS, (j + 1) * CONV_COLS)
        acc = jnp.broadcast_to(cb_ref[:, cs], (L, CONV_COLS))
        for k in range(SSD_CONV):
            off = SSD_HALO - (SSD_CONV - 1) + k
            acc = acc + cw_ref[k:k + 1, cs] * ext_scr[off:off + L, cs]
        act_scr[:, cs] = _silu(acc)
    ext_scr[0:SSD_HALO, :] = ext_scr[L:L + SSD_HALO, :]

    dt = dt_ref[...]
    la = dt * a_ref[...]
    ri = lax.broadcasted_iota(jnp.int32, (L, L), 0)
    ci = lax.broadcasted_iota(jnp.int32, (L, L), 1)
    causal = ri >= ci
    tril = jnp.where(causal, 1.0, 0.0)
    triu = jnp.where(ri <= ci, 1.0, 0.0)
    acum = jnp.dot(tril, la, precision=HIGHEST, preferred_element_type=F32)
    acum_t = jnp.dot(la.T, triu, precision=HIGHEST, preferred_element_type=F32)
    acum_last = acum[L - 1:L, :]
    stacked = jnp.concatenate([dt, jnp.exp(acum_last - acum), jnp.exp(acum)], axis=0)
    s_hi = stacked.astype(BF16)
    s_lo = (stacked - s_hi.astype(F32)).astype(BF16)
    ex = (jnp.dot(s_hi, expand_ref[...], preferred_element_type=F32)
          + jnp.dot(s_lo, expand_ref[...], preferred_element_type=F32))
    dt_x = ex[0:L]
    dec_x = ex[L:2 * L]
    eac_x = ex[2 * L:3 * L]
    elast_x = eac_x[L - 1:L, :]

    for g in range(SSD_GROUPS):
        gs = slice(g * GROUP_DIM, (g + 1) * GROUP_DIM)
        bg = act_scr[:, nx + g * SSD_STATE:nx + (g + 1) * SSD_STATE]
        cg = act_scr[:, nx + nb + g * SSD_STATE:nx + nb + (g + 1) * SSD_STATE]
        bgb = bg.astype(BF16)
        cgb = cg.astype(BF16)
        cb = lax.dot_general(cgb, bgb, (((1,), (1,)), ((), ())), preferred_element_type=F32)
        xg = act_scr[:, gs]
        xdt = xg * dt_x[:, gs]
        xdtb = xdt.astype(BF16)
        yd = []
        for r in range(HEADS_PER_GROUP):
            hd = g * HEADS_PER_GROUP + r
            seg = acum[:, hd:hd + 1] - acum_t[hd:hd + 1, :]
            lm = jnp.exp(jnp.where(causal, seg, -jnp.inf))
            m = (cb * lm).astype(BF16)
            yd.append(jnp.dot(m, xdtb[:, r * SSD_HEAD_DIM:(r + 1) * SSD_HEAD_DIM], preferred_element_type=F32))
        st = state_scr[g]
        y = (jnp.concatenate(yd, axis=1)
             + jnp.dot(cgb, st.astype(BF16), preferred_element_type=F32) * eac_x[:, gs]
             + dsk_ref[:, gs] * xg)
        state_scr[g] = st * elast_x[:, gs] + jnp.dot(bg.T.astype(BF16), (xdt * dec_x[:, gs]).astype(BF16),
                                                    preferred_element_type=F32)
        yz = y * _silu(z_ref[:, gs].astype(F32))
        y_scr[:, gs] = _rms(yz, nw_ref[:, gs]).astype(BF16)
    y_a = jnp.dot(y_scr[...], wso_ref[...], preferred_element_type=F32)

    uext_scr[CONF_HALO:CONF_HALO + L, :] = glua_ref[...].astype(F32) * _sigmoid(glub_ref[...].astype(F32))
    for j in range(uext_scr.shape[1] // CONV_COLS):
        cs = slice(j * CONV_COLS, (j + 1) * CONV_COLS)
        for sft in range(1, 8):
            ushift_scr[sft - 1] = uext_scr[sft:sft + L + CONF_HALO - 8, cs]
        acc = jnp.broadcast_to(cdb_ref[:, cs], (L, CONV_COLS))
        for k in range(CONF_KERNEL):
            off = CONF_HALO - (CONF_KERNEL - 1) + k
            q8, sft = (off // 8) * 8, off % 8
            tap = uext_scr[q8:q8 + L, cs] if sft == 0 else ushift_scr[sft - 1, q8:q8 + L, :]
            acc = acc + cdw_ref[k:k + 1, cs] * tap
        conv_scr[:, cs] = acc
    uext_scr[0:CONF_HALO, :] = uext_scr[L:L + CONF_HALO, :]
    u = conv_scr[...]
    mu = jnp.mean(u, axis=-1, keepdims=True)
    uc = u - mu
    un = uc * lax.rsqrt(jnp.mean(uc * uc, axis=-1, keepdims=True) + EPS) * lnw_ref[...] + lnb_ref[...]
    y_b = jnp.dot(_silu(un).astype(BF16), wco_ref[...], preferred_element_type=F32) + bco_ref[...]

    merged = _sigmoid(ga_ref[...].astype(F32)) * y_a + _sigmoid(gb_ref[...].astype(F32)) * y_b
    h_ref[...] = x_ref[...] + jnp.dot(merged.astype(BF16), wo_ref[...], preferred_element_type=F32)


def token_mixers(x2d, proj, dt, conv_w, conv_b, a_pad, expand, dskip_x, ssd_norm_w, w_ssd_out,
                 conv_dw_w, conv_dw_b, ln_w, ln_b, w_conv_out, b_conv_out, w_o, *, batch, L):
    n, d = x2d.shape
    spb = n // batch // L
    row = lambda b, c: b * spb + c
    col = lambda k, w=1: pl.BlockSpec((L, w * d), lambda b, c, k=k: (row(b, c), k))
    full = lambda a: pl.BlockSpec(a.shape, lambda b, c: (0,) * a.ndim)
    consts = [conv_w, conv_b, a_pad, expand, dskip_x, ssd_norm_w, w_ssd_out,
              conv_dw_w, conv_dw_b, ln_w, ln_b, w_conv_out, b_conv_out, w_o]
    nxbc = conv_w.shape[1]
    return pl.pallas_call(
        functools.partial(_mixer_body, L=L),
        grid=(batch, spb),
        in_specs=[pl.BlockSpec((L, d), lambda b, c: (row(b, c), 0)),
                  col(0, 2),
                  col(1, 2),
                  col(4), col(5),
                  col(6), col(7),
                  col(8), col(9),
                  pl.BlockSpec((L, LANES), lambda b, c: (row(b, c), 0))] + [full(a) for a in consts],
        out_specs=pl.BlockSpec((L, d), lambda b, c: (row(b, c), 0)),
        out_shape=jax.ShapeDtypeStruct((n, d), F32),
        scratch_shapes=[pltpu.VMEM((L + SSD_HALO, nxbc), F32),
                        pltpu.VMEM((L, nxbc), F32),
                        pltpu.VMEM((L + CONF_HALO, d), F32),
                        pltpu.VMEM((7, L + CONF_HALO - 8, CONV_COLS), F32),
                        pltpu.VMEM((SSD_GROUPS, SSD_STATE, GROUP_DIM), F32),
                        pltpu.VMEM((L, D_INNER), BF16),
                        pltpu.VMEM((L, d), F32)],
        compiler_params=pltpu.CompilerParams(dimension_semantics=("arbitrary", "arbitrary"),
                                             vmem_limit_bytes=56 * 1024 * 1024),
        name="token_mixers",
    )(x2d, proj, proj, proj, proj, proj, proj, proj, proj, dt, *consts)


NK = PEER_N_KEYS
TOPK = PEER_TOPK
NEG = float("-inf")


def _sort16_network():
    pairs, p = [], 1
    while p < 16:
        k = p
        while k >= 1:
            for j in range(k % p, 16 - k, 2 * k):
                for i in range(min(k, 16 - j - k)):
                    if (i + j) // (2 * p) == (i + j + k) // (2 * p):
                        pairs.append((i + j, i + j + k))
            k //= 2
        p *= 2
    return pairs


def _top16_of_slabs(vals, rows, tags=None):
    n = len(vals)
    lists = [vals, rows] + ([tags] if tags is not None else [])
    for a, b in _sort16_network():
        if b >= n:
            continue
        swap = (vals[b] > vals[a]) | ((vals[b] == vals[a]) & (rows[b] < rows[a]))
        for x in lists:
            x[a], x[b] = jnp.where(swap, x[b], x[a]), jnp.where(swap, x[a], x[b])
    no_row = n * 8
    out = [[] for _ in lists]
    for r in range(TOPK):
        m = jnp.max(vals[0], axis=0, keepdims=True)
        row = jnp.min(jnp.where(vals[0] == m, rows[0], no_row), axis=0, keepdims=True)
        pop = rows[0] == row
        out[0].append(m)
        out[1].append(row)
        if tags is not None:
            out[2].append(jnp.sum(jnp.where(pop, tags[0], 0), axis=0, keepdims=True))
        for j in range(min(n, TOPK - 1 - r)):
            for x in lists:
                below = x[j + 1] if j + 1 < n else (jnp.full_like(x[j], NEG) if x is vals else x[j])
                x[j] = jnp.where(pop, below, x[j])
    return [jnp.concatenate(o, axis=0) for o in out]


def _top16_rows(s):
    iota8 = lax.broadcasted_iota(jnp.int32, (8, s.shape[1]), 0)
    nslab = s.shape[0] // 8
    return _top16_of_slabs([s[8 * i:8 * i + 8] for i in range(nslab)], [iota8 + 8 * i for i in range(nslab)])


def _gelu(x):
    return 0.5 * x * (1.0 + lax.erf(x * np.float32(0.7071067811865476)))


def _pair_top16(sv0, si0, sv1, si1):
    t = sv0.shape[1]
    iota8 = lax.broadcasted_iota(jnp.int32, (8, t), 0)
    pv = [sv0[0:1] + sv1, sv0[1:2] + sv1[0:8]]
    pe = [si0[0:1] * NK + si1, si0[1:2] * NK + si1[0:8]]
    for p, n in ((2, 5), (3, 4), (4, 3), (5, 2), (6, 2), (7, 2)):
        pv.append(jnp.where(iota8 < n, sv0[p:p + 1] + sv1[0:8], NEG))
        pe.append(si0[p:p + 1] * NK + si1[0:8])
    pv.append(sv0[8:16] + sv1[0:1])
    pe.append(si0[8:16] * NK + si1[0:1])
    slabs_v = [pv[0][0:8], pv[0][8:16]] + pv[1:]
    slabs_e = [pe[0][0:8], pe[0][8:16]] + pe[1:]
    slabs_r = [iota8 + 8 * j for j in range(len(slabs_v))]
    best, _, experts = _top16_of_slabs(slabs_v, slabs_r, slabs_e)
    ex = jnp.exp(best - best[0:1])
    return experts, ex / jnp.sum(ex, axis=0, keepdims=True)


def _peer_body(hn_ref, nw_ref, wqt_ref, keys_ref, ut_ref, v_ref, o_ref,
               xnt_nxt, et_nxt, gt_nxt, xn_cur, e_cur, g_cur, s_scr, sv0_scr, si0_scr,
               act_scr, sc0_scr, sc1_scr, stg0_scr, stg1_scr, w3_scr, *, nblk, tile):
    i = pl.program_id(0)
    k = pl.program_id(1)
    eb = ut_ref.shape[1]
    cpb = eb // NK
    nsel = e_cur.shape[1]
    hb = eb // 2
    cph = cpb // 2

    def key_scores(c):
        q = jnp.dot(wqt_ref[c * NK:(c + 1) * NK, :], xnt_nxt[...], preferred_element_type=F32)
        return jnp.dot(keys_ref[c], q.astype(BF16), preferred_element_type=F32)

    def route_piece(c):
        v, idx = _top16_rows(s_scr[...])
        if c == 0:
            sv0_scr[...] = v
            si0_scr[...] = idx
        else:
            experts, gate = _pair_top16(sv0_scr[...], si0_scr[...], v, idx)
            r0 = pl.multiple_of((k // 2) * TOPK, TOPK)
            et_nxt[pl.ds(r0, TOPK), :] = experts
            gt_nxt[pl.ds(r0, TOPK), :] = gate
        s_scr[...] = key_scores(1 - c)

    @pl.when(k == 0)
    def _():
        @pl.when(i == 0)
        def _():
            xnt_nxt[...] = jnp.zeros_like(xnt_nxt)
            et_nxt[...] = jnp.zeros_like(et_nxt)
            gt_nxt[...] = jnp.zeros_like(gt_nxt)
            sc1_scr[...] = jnp.zeros_like(sc1_scr)

        xn_cur[...] = xnt_nxt[...].T
        e_cur[...] = et_nxt[...].T
        g_cur[...] = gt_nxt[...].T
        xn = _rms(hn_ref[...], nw_ref[...])
        xnt_nxt[...] = xn.astype(BF16).T
        s_scr[...] = key_scores(0)
        act_scr[...] = jnp.zeros_like(act_scr)

    def score_half(sc_ref, half):
        sc_ref[...] = jnp.dot(xn_cur[...], ut_ref[:, half * hb:(half + 1) * hb], preferred_element_type=F32)

    def pick_half(sc_ref, chunk0):
        e = e_cur[...]
        row = e >> 7
        col = e & (NK - 1)
        act = act_scr[...]
        for cc in range(cph):
            picked = jnp.take_along_axis(sc_ref[:, cc * NK:(cc + 1) * NK], col, axis=1)
            act = jnp.where(row == chunk0 + cc, picked, act)
        act_scr[...] = act

    def scatter_weights():
        act_scr[...] = g_cur[...] * _gelu(act_scr[...])
        iota = lax.broadcasted_iota(jnp.int32, (NK, nsel), 0).astype(F32).astype(BF16)
        one = jnp.ones((NK, nsel), BF16)
        zero = jnp.zeros((NK, nsel), BF16)

        def scatter_group(grp, stage_ref):
            t0 = pl.multiple_of(grp * 16, 16)
            e_rows = e_cur[pl.ds(t0, 16), :]
            i1_rows = (e_rows >> 7).astype(F32).astype(BF16)
            i2_rows = (e_rows & (NK - 1)).astype(F32).astype(BF16)
            w_rows = act_scr[pl.ds(t0, 16), :].astype(BF16)
            for j in range(16):
                pm = jnp.where(iota == i1_rows[j:j + 1], one, zero)
                qm = jnp.where(iota == i2_rows[j:j + 1], jnp.broadcast_to(w_rows[j:j + 1], (NK, nsel)), zero)
                stage_ref[j] = lax.dot_general(pm, qm, (((1,), (1,)), ((), ())),
                                               preferred_element_type=F32).astype(BF16)

        def swap_group(grp, stage_ref):
            t0 = pl.multiple_of(grp * 16, 16)
            w3_scr[:, pl.ds(t0, 16), :] = jnp.swapaxes(stage_ref[...], 0, 1)

        ngrp = tile // 16
        scatter_group(0, stg0_scr)

        def pair(i2, carry):
            scatter_group(2 * i2 + 1, stg1_scr)
            swap_group(2 * i2, stg0_scr)
            scatter_group(2 * i2 + 2, stg0_scr)
            swap_group(2 * i2 + 1, stg1_scr)
            return carry

        lax.fori_loop(0, ngrp // 2 - 1, pair, 0)
        scatter_group(ngrp - 1, stg1_scr)
        swap_group(ngrp - 2, stg0_scr)
        swap_group(ngrp - 1, stg1_scr)

    def weights_times_values():
        kk = k - nblk
        parts = [w3_scr[kk * cpb + cc] for cc in range(cpb)]
        return jnp.dot(jnp.concatenate(parts, axis=1), v_ref[...], preferred_element_type=F32)

    for c in range(2):
        @pl.when(jnp.logical_and(k < nblk, k % 2 == c))
        def _(c=c):
            route_piece(c)
            score_half(sc0_scr, 0)
            pick_half(sc1_scr, (k - 1) * cpb + cph)
            score_half(sc1_scr, 1)
            pick_half(sc0_scr, k * cpb)

    @pl.when(k == nblk)
    def _():
        pick_half(sc1_scr, (k - 1) * cpb + cph)
        scatter_weights()
        route_piece(nblk % 2)
        o_ref[...] = weights_times_values()

    for c in range(2):
        @pl.when(jnp.logical_and(k > nblk, k % 2 == c))
        def _(c=c):
            route_piece(c)
            o_ref[...] += weights_times_values()


def peer_mixer(h2d, norm_w, wqt, keys_hc, ut, v, *, tile, eb):
    n, d = h2d.shape
    heads = keys_hc.shape[0] // 2
    nsel = heads * TOPK
    ne = ut.shape[1]
    nblk = ne // eb
    assert nblk == heads, "one half-head is routed per grid step, so steps per tile = 2 * heads"
    nt = n // tile
    return pl.pallas_call(
        functools.partial(_peer_body, nblk=nblk, tile=tile),
        grid=(nt + 1, 2 * nblk),
        in_specs=[pl.BlockSpec((tile, d), lambda i, k: (jnp.minimum(i, nt - 1), 0)),
                  pl.BlockSpec((1, d), lambda i, k: (0, 0)),
                  pl.BlockSpec((2 * NK, d), lambda i, k: (jnp.minimum((k + 1) // 2, heads - 1), 0)),
                  pl.BlockSpec((2, NK, NK), lambda i, k: (jnp.minimum((k + 1) // 2, heads - 1), 0, 0)),
                  pl.BlockSpec((d, eb), lambda i, k: (0, jnp.minimum(k, nblk - 1))),
                  pl.BlockSpec((eb, d), lambda i, k: (jnp.maximum(k - nblk, 0), 0))],
        out_specs=pl.BlockSpec((tile, d), lambda i, k: (jnp.maximum(i - 1, 0), 0)),
        out_shape=jax.ShapeDtypeStruct((n, d), F32),
        scratch_shapes=[pltpu.VMEM((d, tile), BF16),
                        pltpu.VMEM((nsel, tile), jnp.int32),
                        pltpu.VMEM((nsel, tile), F32),
                        pltpu.VMEM((tile, d), BF16),
                        pltpu.VMEM((tile, nsel), jnp.int32),
                        pltpu.VMEM((tile, nsel), F32),
                        pltpu.VMEM((NK, tile), F32),
                        pltpu.VMEM((TOPK, tile), F32),
                        pltpu.VMEM((TOPK, tile), jnp.int32),
                        pltpu.VMEM((tile, nsel), F32),
                        pltpu.VMEM((tile, eb // 2), F32),
                        pltpu.VMEM((tile, eb // 2), F32),
                        pltpu.VMEM((16, NK, NK), BF16),
                        pltpu.VMEM((16, NK, NK), BF16),
                        pltpu.VMEM((ne // NK, tile, NK), BF16)],
        compiler_params=pltpu.CompilerParams(dimension_semantics=("arbitrary", "arbitrary"),
                                             vmem_limit_bytes=56 * 1024 * 1024),
        name="peer_mixer",
    )(h2d, norm_w.reshape(1, d), wqt, keys_hc, ut, v)


def _ple_body(h_ref, dpeer_ref, p_ref, nw_ref, wg_ref, wp_ref, fw_ref, o_ref):
    h = h_ref[...] + dpeer_ref[...]
    gate = _sigmoid(jnp.dot(_rms(h, nw_ref[...]).astype(BF16), wg_ref[...], preferred_element_type=F32))
    h = h + gate * jnp.dot(p_ref[...].astype(BF16), wp_ref[...], preferred_element_type=F32)
    o_ref[...] = _rms(h, fw_ref[...])


def ple_final(h2d, dpeer, p2d, norm_w, w_gate, w_proj, final_w, *, tm):
    n, d = h2d.shape
    pd = p2d.shape[1]
    return pl.pallas_call(
        _ple_body,
        grid=(n // tm,),
        in_specs=[pl.BlockSpec((tm, d), lambda i: (i, 0)),
                  pl.BlockSpec((tm, d), lambda i: (i, 0)),
                  pl.BlockSpec((tm, pd), lambda i: (i, 0)),
                  pl.BlockSpec((1, d), lambda i: (0, 0)),
                  pl.BlockSpec((d, d), lambda i: (0, 0)),
                  pl.BlockSpec((pd, d), lambda i: (0, 0)),
                  pl.BlockSpec((1, d), lambda i: (0, 0))],
        out_specs=pl.BlockSpec((tm, d), lambda i: (i, 0)),
        out_shape=jax.ShapeDtypeStruct((n, d), F32),
        compiler_params=pltpu.CompilerParams(dimension_semantics=("arbitrary",),
                                             vmem_limit_bytes=40 * 1024 * 1024),
        name="ple_final",
    )(h2d, dpeer, p2d, norm_w.reshape(1, d), w_gate, w_proj, final_w.reshape(1, d))


def kernel(x, p, norm_mix_w, w_in, conv_ssd_w, conv_ssd_b, dt_bias, a_log, d_skip,
           ssd_norm_w, w_ssd_out, conv_dw_w, conv_dw_b, conv_ln_w, conv_ln_b,
           w_conv_out, b_conv_out, w_o, norm_ffn_w, peer_wq, peer_keys, peer_u, peer_v,
           norm_ple_w, w_ple_gate, w_ple_proj, final_norm_w):
    bsz, s, d = x.shape
    x2d = x.reshape(bsz * s, d)
    i = 0
    r1 = lambda v: v.reshape(1, -1)

    col_xbc = D_INNER + conv_ssd_w.shape[2]
    col_dt = col_xbc + SSD_HEADS
    w_main = jnp.concatenate([w_in[i][:, :col_xbc], w_in[i][:, col_dt:]], axis=1).astype(BF16)
    w_dt = jnp.pad(w_in[i][:, col_xbc:col_dt], ((0, 0), (0, LANES - SSD_HEADS))).astype(BF16)
    b_dt = jnp.pad(dt_bias[i], (0, LANES - SSD_HEADS)).reshape(1, LANES)
    a_pad = jnp.pad(-jnp.exp(a_log[i]), (0, LANES - SSD_HEADS)).reshape(1, LANES)
    dskip_x = jnp.repeat(d_skip[i], SSD_HEAD_DIM).reshape(1, D_INNER)
    expand = (jnp.arange(D_INNER)[None, :] // SSD_HEAD_DIM == jnp.arange(LANES)[:, None]).astype(BF16)

    proj, dt = in_projection(x2d, norm_mix_w[i], w_main, w_dt, b_dt, tm=INPROJ_TM, tn=INPROJ_TN)
    h2d = token_mixers(x2d, proj, dt, conv_ssd_w[i], r1(conv_ssd_b[i]), a_pad, expand, dskip_x, r1(ssd_norm_w[i]),
                       w_ssd_out[i].astype(BF16), conv_dw_w[i], r1(conv_dw_b[i]), r1(conv_ln_w[i]),
                       r1(conv_ln_b[i]), w_conv_out[i].astype(BF16), r1(b_conv_out[i]), w_o[i].astype(BF16),
                       batch=bsz, L=MIX_L)

    wqt = peer_wq[i].T.astype(BF16)
    keys_hc = peer_keys[i].reshape(PEER_HEADS * 2, PEER_N_KEYS, PEER_HALF).astype(BF16)
    dpeer = peer_mixer(h2d, norm_ffn_w[i], wqt, keys_hc, peer_u[i].T.astype(BF16), peer_v[i].astype(BF16),
                       tile=PEER_TILE, eb=PEER_EXPERT_BLOCK)

    out = ple_final(h2d, dpeer, p[i].reshape(bsz * s, -1), norm_ple_w[i], w_ple_gate[i].astype(BF16),
                    w_ple_proj[i].astype(BF16), final_norm_w, tm=PLE_TM)
    return out.reshape(bsz, s, d)
```

```python
import functools
import jax
import jax.numpy as jnp
from jax import lax
import numpy as np
from jax.experimental import pallas as pl
from jax.experimental.pallas import tpu as pltpu

D_MODEL = 1024
D_INNER = 2 * D_MODEL
SSD_HEAD_DIM = 64
SSD_HEADS = D_INNER // SSD_HEAD_DIM
SSD_GROUPS = 8
SSD_STATE = 128
SSD_CONV = 4
HEADS_PER_GROUP = SSD_HEADS // SSD_GROUPS
GROUP_DIM = D_INNER // SSD_GROUPS
CONF_KERNEL = 31
PEER_HEADS = 8
PEER_N_KEYS = 128
PEER_TOPK = 16
PEER_HALF = 128
EPS = 1e-6
F32 = jnp.float32
BF16 = jnp.bfloat16
LANES = 128
HIGHEST = lax.Precision.HIGHEST

INPROJ_TM = 1024
INPROJ_TN = 2048
MIX_L = 256
PEER_TILE = 512
PEER_EXPERT_BLOCK = 2048
PLE_TM = 512


def _sigmoid(x):
    return 0.5 * jnp.tanh(0.5 * x) + 0.5


def _silu(x):
    h = 0.5 * x
    return h + h * jnp.tanh(h)


def _rms(x, w):
    return x * lax.rsqrt(jnp.mean(x * x, axis=-1, keepdims=True) + EPS) * w


def _inproj_body(x_ref, nw_ref, w_ref, wdt_ref, bdt_ref, o_ref, dt_ref, hn_scr):
    @pl.when(pl.program_id(1) == 0)
    def _():
        hn = _rms(x_ref[...], nw_ref[...]).astype(BF16)
        hn_scr[...] = hn
        v = jnp.dot(hn, wdt_ref[...], preferred_element_type=F32) + bdt_ref[...]
        dt_ref[...] = jnp.maximum(v, 0.0) + jnp.log(1.0 + jnp.exp(-jnp.abs(v)))

    o_ref[...] = jnp.dot(hn_scr[...], w_ref[...], preferred_element_type=F32).astype(BF16)


def in_projection(x2d, norm_w, w_main, w_dt, b_dt, *, tm, tn):
    n, d = x2d.shape
    c = w_main.shape[1]
    return pl.pallas_call(
        _inproj_body,
        grid=(n // tm, c // tn),
        in_specs=[pl.BlockSpec((tm, d), lambda i, j: (i, 0)),
                  pl.BlockSpec((1, d), lambda i, j: (0, 0)),
                  pl.BlockSpec((d, tn), lambda i, j: (0, j)),
                  pl.BlockSpec((d, LANES), lambda i, j: (0, 0)),
                  pl.BlockSpec((1, LANES), lambda i, j: (0, 0))],
        out_specs=[pl.BlockSpec((tm, tn), lambda i, j: (i, j)),
                   pl.BlockSpec((tm, LANES), lambda i, j: (i, 0))],
        out_shape=[jax.ShapeDtypeStruct((n, c), BF16), jax.ShapeDtypeStruct((n, LANES), F32)],
        scratch_shapes=[pltpu.VMEM((tm, d), BF16)],
        compiler_params=pltpu.CompilerParams(dimension_semantics=("arbitrary", "arbitrary"),
                                             vmem_limit_bytes=40 * 1024 * 1024),
        name="in_projection",
    )(x2d, norm_w.reshape(1, d), w_main, w_dt, b_dt)


SSD_HALO = 8
CONF_HALO = 32
CONV_COLS = 512


def _mixer_body(x_ref, z_ref, xs_ref, b_ref, c_ref, glua_ref, glub_ref, ga_ref, gb_ref, dt_ref,
                cw_ref, cb_ref, a_ref, expand_ref, dsk_ref, nw_ref, wso_ref,
                cdw_ref, cdb_ref, lnw_ref, lnb_ref, wco_ref, bco_ref, wo_ref,
                h_ref,
                ext_scr, act_scr, uext_scr, ushift_scr, state_scr, y_scr, conv_scr, *, L):
    step = pl.program_id(1)

    @pl.when(step == 0)
    def _():
        ext_scr[0:SSD_HALO, :] = jnp.zeros((SSD_HALO, ext_scr.shape[1]), F32)
        uext_scr[0:CONF_HALO, :] = jnp.zeros((CONF_HALO, uext_scr.shape[1]), F32)
        state_scr[...] = jnp.zeros_like(state_scr)

    nx = xs_ref.shape[1]
    nb = b_ref.shape[1]
    ext_scr[SSD_HALO:SSD_HALO + L, 0:nx] = xs_ref[...].astype(F32)
    ext_scr[SSD_HALO:SSD_HALO + L, nx:nx + nb] = b_ref[...].astype(F32)
    ext_scr[SSD_HALO:SSD_HALO + L, nx + nb:nx + 2 * nb] = c_ref[...].astype(F32)
    for j in range(ext_scr.shape[1] // CONV_COLS):
        cs = slice(j * CONV_COLS, (j + 1) * CONV_COLS)
        acc = jnp.broadcast_to(cb_ref[:, cs], (L, CONV_COLS))
        for k in range(SSD_CONV):
            off = SSD_HALO - (SSD_CONV - 1) + k
            acc = acc + cw_ref[k:k + 1, cs] * ext_scr[off:off + L, cs]
        act_scr[:, cs] = _silu(acc)
    ext_scr[0:SSD_HALO, :] = ext_scr[L:L + SSD_HALO, :]

    dt = dt_ref[...]
    la = dt * a_ref[...]
    ri = lax.broadcasted_iota(jnp.int32, (L, L), 0)
    ci = lax.broadcasted_iota(jnp.int32, (L, L), 1)
    causal = ri >= ci
    tril = jnp.where(causal, 1.0, 0.0)
    triu = jnp.where(ri <= ci, 1.0, 0.0)
    acum = jnp.dot(tril, la, precision=HIGHEST, preferred_element_type=F32)
    acum_t = jnp.dot(la.T, triu, precision=HIGHEST, preferred_element_type=F32)
    acum_last = acum[L - 1:L, :]
    stacked = jnp.concatenate([dt, jnp.exp(acum_last - acum), jnp.exp(acum)], axis=0)
    s_hi = stacked.astype(BF16)
    s_lo = (stacked - s_hi.astype(F32)).astype(BF16)
    ex = (jnp.dot(s_hi, expand_ref[...], preferred_element_type=F32)
          + jnp.dot(s_lo, expand_ref[...], preferred_element_type=F32))
    dt_x = ex[0:L]
    dec_x = ex[L:2 * L]
    eac_x = ex[2 * L:3 * L]
    elast_x = eac_x[L - 1:L, :]

    for g in range(SSD_GROUPS):
        gs = slice(g * GROUP_DIM, (g + 1) * GROUP_DIM)
        bg = act_scr[:, nx + g * SSD_STATE:nx + (g + 1) * SSD_STATE]
        cg = act_scr[:, nx + nb + g * SSD_STATE:nx + nb + (g + 1) * SSD_STATE]
        bgb = bg.astype(BF16)
        cgb = cg.astype(BF16)
        cb = lax.dot_general(cgb, bgb, (((1,), (1,)), ((), ())), preferred_element_type=F32)
        xg = act_scr[:, gs]
        xdt = xg * dt_x[:, gs]
        xdtb = xdt.astype(BF16)
        yd = []
        for r in range(HEADS_PER_GROUP):
            hd = g * HEADS_PER_GROUP + r
            seg = acum[:, hd:hd + 1] - acum_t[hd:hd + 1, :]
            lm = jnp.exp(jnp.where(causal, seg, -jnp.inf))
            m = (cb * lm).astype(BF16)
            yd.append(jnp.dot(m, xdtb[:, r * SSD_HEAD_DIM:(r + 1) * SSD_HEAD_DIM], preferred_element_type=F32))
        st = state_scr[g]
        y = (jnp.concatenate(yd, axis=1)
             + jnp.dot(cgb, st.astype(BF16), preferred_element_type=F32) * eac_x[:, gs]
             + dsk_ref[:, gs] * xg)
        state_scr[g] = st * elast_x[:, gs] + jnp.dot(bg.T.astype(BF16), (xdt * dec_x[:, gs]).astype(BF16),
                                                    preferred_element_type=F32)
        yz = y * _silu(z_ref[:, gs].astype(F32))
        y_scr[:, gs] = _rms(yz, nw_ref[:, gs]).astype(BF16)
    y_a = jnp.dot(y_scr[...], wso_ref[...], preferred_element_type=F32)

    uext_scr[CONF_HALO:CONF_HALO + L, :] = glua_ref[...].astype(F32) * _sigmoid(glub_ref[...].astype(F32))
    for j in range(uext_scr.shape[1] // CONV_COLS):
        cs = slice(j * CONV_COLS, (j + 1) * CONV_COLS)
        for sft in range(1, 8):
            ushift_scr[sft - 1] = uext_scr[sft:sft + L + CONF_HALO - 8, cs]
        acc = jnp.broadcast_to(cdb_ref[:, cs], (L, CONV_COLS))
        for k in range(CONF_KERNEL):
            off = CONF_HALO - (CONF_KERNEL - 1) + k
            q8, sft = (off // 8) * 8, off % 8
            tap = uext_scr[q8:q8 + L, cs] if sft == 0 else ushift_scr[sft - 1, q8:q8 + L, :]
            acc = acc + cdw_ref[k:k + 1, cs] * tap
        conv_scr[:, cs] = acc
    uext_scr[0:CONF_HALO, :] = uext_scr[L:L + CONF_HALO, :]
    u = conv_scr[...]
    mu = jnp.mean(u, axis=-1, keepdims=True)
    uc = u - mu
    un = uc * lax.rsqrt(jnp.mean(uc * uc, axis=-1, keepdims=True) + EPS) * lnw_ref[...] + lnb_ref[...]
    y_b = jnp.dot(_silu(un).astype(BF16), wco_ref[...], preferred_element_type=F32) + bco_ref[...]

    merged = _sigmoid(ga_ref[...].astype(F32)) * y_a + _sigmoid(gb_ref[...].astype(F32)) * y_b
    h_ref[...] = x_ref[...] + jnp.dot(merged.astype(BF16), wo_ref[...], preferred_element_type=F32)


def token_mixers(x2d, proj, dt, conv_w, conv_b, a_pad, expand, dskip_x, ssd_norm_w, w_ssd_out,
                 conv_dw_w, conv_dw_b, ln_w, ln_b, w_conv_out, b_conv_out, w_o, *, batch, L):
    n, d = x2d.shape
    spb = n // batch // L
    row = lambda b, c: b * spb + c
    col = lambda k, w=1: pl.BlockSpec((L, w * d), lambda b, c, k=k: (row(b, c), k))
    full = lambda a: pl.BlockSpec(a.shape, lambda b, c: (0,) * a.ndim)
    consts = [conv_w, conv_b, a_pad, expand, dskip_x, ssd_norm_w, w_ssd_out,
              conv_dw_w, conv_dw_b, ln_w, ln_b, w_conv_out, b_conv_out, w_o]
    nxbc = conv_w.shape[1]
    return pl.pallas_call(
        functools.partial(_mixer_body, L=L),
        grid=(batch, spb),
        in_specs=[pl.BlockSpec((L, d), lambda b, c: (row(b, c), 0)),
                  col(0, 2),
                  col(1, 2),
                  col(4), col(5),
                  col(6), col(7),
                  col(8), col(9),
                  pl.BlockSpec((L, LANES), lambda b, c: (row(b, c), 0))] + [full(a) for a in consts],
        out_specs=pl.BlockSpec((L, d), lambda b, c: (row(b, c), 0)),
        out_shape=jax.ShapeDtypeStruct((n, d), F32),
        scratch_shapes=[pltpu.VMEM((L + SSD_HALO, nxbc), F32),
                        pltpu.VMEM((L, nxbc), F32),
                        pltpu.VMEM((L + CONF_HALO, d), F32),
                        pltpu.VMEM((7, L + CONF_HALO - 8, CONV_COLS), F32),
                        pltpu.VMEM((SSD_GROUPS, SSD_STATE, GROUP_DIM), F32),
                        pltpu.VMEM((L, D_INNER), BF16),
                        pltpu.VMEM((L, d), F32)],
        compiler_params=pltpu.CompilerParams(dimension_semantics=("arbitrary", "arbitrary"),
                                             vmem_limit_bytes=56 * 1024 * 1024),
        name="token_mixers",
    )(x2d, proj, proj, proj, proj, proj, proj, proj, proj, dt, *consts)


NK = PEER_N_KEYS
TOPK = PEER_TOPK
NEG = float("-inf")


def _sort16_network():
    pairs, p = [], 1
    while p < 16:
        k = p
        while k >= 1:
            for j in range(k % p, 16 - k, 2 * k):
                for i in range(min(k, 16 - j - k)):
                    if (i + j) // (2 * p) == (i + j + k) // (2 * p):
                        pairs.append((i + j, i + j + k))
            k //= 2
        p *= 2
    return pairs


def _top16_of_slabs(vals, rows, tags=None):
    n = len(vals)
    lists = [vals, rows] + ([tags] if tags is not None else [])
    for a, b in _sort16_network():
        if b >= n:
            continue
        swap = (vals[b] > vals[a]) | ((vals[b] == vals[a]) & (rows[b] < rows[a]))
        for x in lists:
            x[a], x[b] = jnp.where(swap, x[b], x[a]), jnp.where(swap, x[a], x[b])
    no_row = n * 8
    out = [[] for _ in lists]
    for r in range(TOPK):
        m = jnp.max(vals[0], axis=0, keepdims=True)
        row = jnp.min(jnp.where(vals[0] == m, rows[0], no_row), axis=0, keepdims=True)
        pop = rows[0] == row
        out[0].append(m)
        out[1].append(row)
        if tags is not None:
            out[2].append(jnp.sum(jnp.where(pop, tags[0], 0), axis=0, keepdims=True))
        for j in range(min(n, TOPK - 1 - r)):
            for x in lists:
                below = x[j + 1] if j + 1 < n else (jnp.full_like(x[j], NEG) if x is vals else x[j])
                x[j] = jnp.where(pop, below, x[j])
    return [jnp.concatenate(o, axis=0) for o in out]


def _top16_rows(s):
    iota8 = lax.broadcasted_iota(jnp.int32, (8, s.shape[1]), 0)
    nslab = s.shape[0] // 8
    return _top16_of_slabs([s[8 * i:8 * i + 8] for i in range(nslab)], [iota8 + 8 * i for i in range(nslab)])


def _gelu(x):
    return 0.5 * x * (1.0 + lax.erf(x * np.float32(0.7071067811865476)))


def _pair_top16(sv0, si0, sv1, si1):
    t = sv0.shape[1]
    iota8 = lax.broadcasted_iota(jnp.int32, (8, t), 0)
    pv = [sv0[0:1] + sv1, sv0[1:2] + sv1[0:8]]
    pe = [si0[0:1] * NK + si1, si0[1:2] * NK + si1[0:8]]
    for p, n in ((2, 5), (3, 4), (4, 3), (5, 2), (6, 2), (7, 2)):
        pv.append(jnp.where(iota8 < n, sv0[p:p + 1] + sv1[0:8], NEG))
        pe.append(si0[p:p + 1] * NK + si1[0:8])
    pv.append(sv0[8:16] + sv1[0:1])
    pe.append(si0[8:16] * NK + si1[0:1])
    slabs_v = [pv[0][0:8], pv[0][8:16]] + pv[1:]
    slabs_e = [pe[0][0:8], pe[0][8:16]] + pe[1:]
    slabs_r = [iota8 + 8 * j for j in range(len(slabs_v))]
    best, _, experts = _top16_of_slabs(slabs_v, slabs_r, slabs_e)
    ex = jnp.exp(best - best[0:1])
    return experts, ex / jnp.sum(ex, axis=0, keepdims=True)


def _peer_body(hn_ref, nw_ref, wqt_ref, keys_ref, ut_ref, v_ref, o_ref,
               xnt_nxt, et_nxt, gt_nxt, xn_cur, e_cur, g_cur, s_scr, sv0_scr, si0_scr,
               act_scr, sc0_scr, sc1_scr, stg0_scr, stg1_scr, w3_scr, *, nblk, tile):
    i = pl.program_id(0)
    k = pl.program_id(1)
    eb = ut_ref.shape[1]
    cpb = eb // NK
    nsel = e_cur.shape[1]
    hb = eb // 2
    cph = cpb // 2

    def key_scores(c):
        q = jnp.dot(wqt_ref[c * NK:(c + 1) * NK, :], xnt_nxt[...], preferred_element_type=F32)
        return jnp.dot(keys_ref[c], q.astype(BF16), preferred_element_type=F32)

    def route_piece(c):
        v, idx = _top16_rows(s_scr[...])
        if c == 0:
            sv0_scr[...] = v
            si0_scr[...] = idx
        else:
            experts, gate = _pair_top16(sv0_scr[...], si0_scr[...], v, idx)
            r0 = pl.multiple_of((k // 2) * TOPK, TOPK)
            et_nxt[pl.ds(r0, TOPK), :] = experts
            gt_nxt[pl.ds(r0, TOPK), :] = gate
        s_scr[...] = key_scores(1 - c)

    @pl.when(k == 0)
    def _():
        @pl.when(i == 0)
        def _():
            xnt_nxt[...] = jnp.zeros_like(xnt_nxt)
            et_nxt[...] = jnp.zeros_like(et_nxt)
            gt_nxt[...] = jnp.zeros_like(gt_nxt)
            sc1_scr[...] = jnp.zeros_like(sc1_scr)

        xn_cur[...] = xnt_nxt[...].T
        e_cur[...] = et_nxt[...].T
        g_cur[...] = gt_nxt[...].T
        xn = _rms(hn_ref[...], nw_ref[...])
        xnt_nxt[...] = xn.astype(BF16).T
        s_scr[...] = key_scores(0)
        act_scr[...] = jnp.zeros_like(act_scr)

    def score_half(sc_ref, half):
        sc_ref[...] = jnp.dot(xn_cur[...], ut_ref[:, half * hb:(half + 1) * hb], preferred_element_type=F32)

    def pick_half(sc_ref, chunk0):
        e = e_cur[...]
        row = e >> 7
        col = e & (NK - 1)
        act = act_scr[...]
        for cc in range(cph):
            picked = jnp.take_along_axis(sc_ref[:, cc * NK:(cc + 1) * NK], col, axis=1)
            act = jnp.where(row == chunk0 + cc, picked, act)
        act_scr[...] = act

    def scatter_weights():
        act_scr[...] = g_cur[...] * _gelu(act_scr[...])
        iota = lax.broadcasted_iota(jnp.int32, (NK, nsel), 0).astype(F32).astype(BF16)
        one = jnp.ones((NK, nsel), BF16)
        zero = jnp.zeros((NK, nsel), BF16)

        def scatter_group(grp, stage_ref):
            t0 = pl.multiple_of(grp * 16, 16)
            e_rows = e_cur[pl.ds(t0, 16), :]
            i1_rows = (e_rows >> 7).astype(F32).astype(BF16)
            i2_rows = (e_rows & (NK - 1)).astype(F32).astype(BF16)
            w_rows = act_scr[pl.ds(t0, 16), :].astype(BF16)
            for j in range(16):
                pm = jnp.where(iota == i1_rows[j:j + 1], one, zero)
                qm = jnp.where(iota == i2_rows[j:j + 1], jnp.broadcast_to(w_rows[j:j + 1], (NK, nsel)), zero)
                stage_ref[j] = lax.dot_general(pm, qm, (((1,), (1,)), ((), ())),
                                               preferred_element_type=F32).astype(BF16)

        def swap_group(grp, stage_ref):
            t0 = pl.multiple_of(grp * 16, 16)
            w3_scr[:, pl.ds(t0, 16), :] = jnp.swapaxes(stage_ref[...], 0, 1)

        ngrp = tile // 16
        scatter_group(0, stg0_scr)

        def pair(i2, carry):
            scatter_group(2 * i2 + 1, stg1_scr)
            swap_group(2 * i2, stg0_scr)
            scatter_group(2 * i2 + 2, stg0_scr)
            swap_group(2 * i2 + 1, stg1_scr)
            return carry

        lax.fori_loop(0, ngrp // 2 - 1, pair, 0)
        scatter_group(ngrp - 1, stg1_scr)
        swap_group(ngrp - 2, stg0_scr)
        swap_group(ngrp - 1, stg1_scr)

    def weights_times_values():
        kk = k - nblk
        parts = [w3_scr[kk * cpb + cc] for cc in range(cpb)]
        return jnp.dot(jnp.concatenate(parts, axis=1), v_ref[...], preferred_element_type=F32)

    for c in range(2):
        @pl.when(jnp.logical_and(k < nblk, k % 2 == c))
        def _(c=c):
            route_piece(c)
            score_half(sc0_scr, 0)
            pick_half(sc1_scr, (k - 1) * cpb + cph)
            score_half(sc1_scr, 1)
            pick_half(sc0_scr, k * cpb)

    @pl.when(k == nblk)
    def _():
        pick_half(sc1_scr, (k - 1) * cpb + cph)
        scatter_weights()
        route_piece(nblk % 2)
        o_ref[...] = weights_times_values()

    for c in range(2):
        @pl.when(jnp.logical_and(k > nblk, k % 2 == c))
        def _(c=c):
            route_piece(c)
            o_ref[...] += weights_times_values()


def peer_mixer(h2d, norm_w, wqt, keys_hc, ut, v, *, tile, eb):
    n, d = h2d.shape
    heads = keys_hc.shape[0] // 2
    nsel = heads * TOPK
    ne = ut.shape[1]
    nblk = ne // eb
    assert nblk == heads, "one half-head is routed per grid step, so steps per tile = 2 * heads"
    nt = n // tile
    return pl.pallas_call(
        functools.partial(_peer_body, nblk=nblk, tile=tile),
        grid=(nt + 1, 2 * nblk),
        in_specs=[pl.BlockSpec((tile, d), lambda i, k: (jnp.minimum(i, nt - 1), 0)),
                  pl.BlockSpec((1, d), lambda i, k: (0, 0)),
                  pl.BlockSpec((2 * NK, d), lambda i, k: (jnp.minimum((k + 1) // 2, heads - 1), 0)),
                  pl.BlockSpec((2, NK, NK), lambda i, k: (jnp.minimum((k + 1) // 2, heads - 1), 0, 0)),
                  pl.BlockSpec((d, eb), lambda i, k: (0, jnp.minimum(k, nblk - 1))),
                  pl.BlockSpec((eb, d), lambda i, k: (jnp.maximum(k - nblk, 0), 0))],
        out_specs=pl.BlockSpec((tile, d), lambda i, k: (jnp.maximum(i - 1, 0), 0)),
        out_shape=jax.ShapeDtypeStruct((n, d), F32),
        scratch_shapes=[pltpu.VMEM((d, tile), BF16),
                        pltpu.VMEM((nsel, tile), jnp.int32),
                        pltpu.VMEM((nsel, tile), F32),
                        pltpu.VMEM((tile, d), BF16),
                        pltpu.VMEM((tile, nsel), jnp.int32),
                        pltpu.VMEM((tile, nsel), F32),
                        pltpu.VMEM((NK, tile), F32),
                        pltpu.VMEM((TOPK, tile), F32),
                        pltpu.VMEM((TOPK, tile), jnp.int32),
                        pltpu.VMEM((tile, nsel), F32),
                        pltpu.VMEM((tile, eb // 2), F32),
                        pltpu.VMEM((tile, eb // 2), F32),
                        pltpu.VMEM((16, NK, NK), BF16),
                        pltpu.VMEM((16, NK, NK), BF16),
                        pltpu.VMEM((ne // NK, tile, NK), BF16)],
        compiler_params=pltpu.CompilerParams(dimension_semantics=("arbitrary", "arbitrary"),
                                             vmem_limit_bytes=56 * 1024 * 1024),
        name="peer_mixer",
    )(h2d, norm_w.reshape(1, d), wqt, keys_hc, ut, v)


def _ple_body(h_ref, dpeer_ref, p_ref, nw_ref, wg_ref, wp_ref, fw_ref, o_ref):
    h = h_ref[...] + dpeer_ref[...]
    gate = _sigmoid(jnp.dot(_rms(h, nw_ref[...]).astype(BF16), wg_ref[...], preferred_element_type=F32))
    h = h + gate * jnp.dot(p_ref[...].astype(BF16), wp_ref[...], preferred_element_type=F32)
    o_ref[...] = _rms(h, fw_ref[...])


def ple_final(h2d, dpeer, p2d, norm_w, w_gate, w_proj, final_w, *, tm):
    n, d = h2d.shape
    pd = p2d.shape[1]
    return pl.pallas_call(
        _ple_body,
        grid=(n // tm,),
        in_specs=[pl.BlockSpec((tm, d), lambda i: (i, 0)),
                  pl.BlockSpec((tm, d), lambda i: (i, 0)),
                  pl.BlockSpec((tm, pd), lambda i: (i, 0)),
                  pl.BlockSpec((1, d), lambda i: (0, 0)),
                  pl.BlockSpec((d, d), lambda i: (0, 0)),
                  pl.BlockSpec((pd, d), lambda i: (0, 0)),
                  pl.BlockSpec((1, d), lambda i: (0, 0))],
        out_specs=pl.BlockSpec((tm, d), lambda i: (i, 0)),
        out_shape=jax.ShapeDtypeStruct((n, d), F32),
        compiler_params=pltpu.CompilerParams(dimension_semantics=("arbitrary",),
                                             vmem_limit_bytes=40 * 1024 * 1024),
        name="ple_final",
    )(h2d, dpeer, p2d, norm_w.reshape(1, d), w_gate, w_proj, final_w.reshape(1, d))


def kernel(x, p, norm_mix_w, w_in, conv_ssd_w, conv_ssd_b, dt_bias, a_log, d_skip,
           ssd_norm_w, w_ssd_out, conv_dw_w, conv_dw_b, conv_ln_w, conv_ln_b,
           w_conv_out, b_conv_out, w_o, norm_ffn_w, peer_wq, peer_keys, peer_u, peer_v,
           norm_ple_w, w_ple_gate, w_ple_proj, final_norm_w):
    bsz, s, d = x.shape
    x2d = x.reshape(bsz * s, d)
    i = 0
    r1 = lambda v: v.reshape(1, -1)

    col_xbc = D_INNER + conv_ssd_w.shape[2]
    col_dt = col_xbc + SSD_HEADS
    w_main = jnp.concatenate([w_in[i][:, :col_xbc], w_in[i][:, col_dt:]], axis=1).astype(BF16)
    w_dt = jnp.pad(w_in[i][:, col_xbc:col_dt], ((0, 0), (0, LANES - SSD_HEADS))).astype(BF16)
    b_dt = jnp.pad(dt_bias[i], (0, LANES - SSD_HEADS)).reshape(1, LANES)
    a_pad = jnp.pad(-jnp.exp(a_log[i]), (0, LANES - SSD_HEADS)).reshape(1, LANES)
    dskip_x = jnp.repeat(d_skip[i], SSD_HEAD_DIM).reshape(1, D_INNER)
    expand = (jnp.arange(D_INNER)[None, :] // SSD_HEAD_DIM == jnp.arange(LANES)[:, None]).astype(BF16)

    proj, dt = in_projection(x2d, norm_mix_w[i], w_main, w_dt, b_dt, tm=INPROJ_TM, tn=INPROJ_TN)
    h2d = token_mixers(x2d, proj, dt, conv_ssd_w[i], r1(conv_ssd_b[i]), a_pad, expand, dskip_x, r1(ssd_norm_w[i]),
                       w_ssd_out[i].astype(BF16), conv_dw_w[i], r1(conv_dw_b[i]), r1(conv_ln_w[i]),
                       r1(conv_ln_b[i]), w_conv_out[i].astype(BF16), r1(b_conv_out[i]), w_o[i].astype(BF16),
                       batch=bsz, L=MIX_L)

    wqt = peer_wq[i].T.astype(BF16)
    keys_hc = peer_keys[i].reshape(PEER_HEADS * 2, PEER_N_KEYS, PEER_HALF).astype(BF16)
    dpeer = peer_mixer(h2d, norm_ffn_w[i], wqt, keys_hc, peer_u[i].T.astype(BF16), peer_v[i].astype(BF16),
                       tile=PEER_TILE, eb=PEER_EXPERT_BLOCK)

    out = ple_final(h2d, dpeer, p[i].reshape(bsz * s, -1), norm_ple_w[i], w_ple_gate[i].astype(BF16),
                    w_ple_proj[i].astype(BF16), final_norm_w, tm=PLE_TM)
    return out.reshape(bsz, s, d)
```

```python
import functools
import jax
import jax.numpy as jnp
from jax import lax
import numpy as np
from jax.experimental import pallas as pl
from jax.experimental.pallas import tpu as pltpu

D_MODEL = 1024
D_INNER = 2 * D_MODEL
SSD_HEAD_DIM = 64
SSD_HEADS = D_INNER // SSD_HEAD_DIM
SSD_GROUPS = 8
SSD_STATE = 128
SSD_CONV = 4
HEADS_PER_GROUP = SSD_HEADS // SSD_GROUPS
GROUP_DIM = D_INNER // SSD_GROUPS
CONF_KERNEL = 31
PEER_HEADS = 8
PEER_N_KEYS = 128
PEER_TOPK = 16
PEER_HALF = 128
EPS = 1e-6
F32 = jnp.float32
BF16 = jnp.bfloat16
LANES = 128
HIGHEST = lax.Precision.HIGHEST

INPROJ_TM = 1024
INPROJ_TN = 2048
MIX_L = 256
PEER_TILE = 512
PEER_EXPERT_BLOCK = 2048
PLE_TM = 512


def _sigmoid(x):
    return 0.5 * jnp.tanh(0.5 * x) + 0.5


def _silu(x):
    h = 0.5 * x
    return h + h * jnp.tanh(h)


def _rms(x, w):
    return x * lax.rsqrt(jnp.mean(x * x, axis=-1, keepdims=True) + EPS) * w


def _inproj_body(x_ref, nw_ref, w_ref, wdt_ref, bdt_ref, o_ref, dt_ref, hn_scr):
    @pl.when(pl.program_id(1) == 0)
    def _():
        hn = _rms(x_ref[...], nw_ref[...]).astype(BF16)
        hn_scr[...] = hn
        v = jnp.dot(hn, wdt_ref[...], preferred_element_type=F32) + bdt_ref[...]
        dt_ref[...] = jnp.maximum(v, 0.0) + jnp.log(1.0 + jnp.exp(-jnp.abs(v)))

    o_ref[...] = jnp.dot(hn_scr[...], w_ref[...], preferred_element_type=F32).astype(BF16)


def in_projection(x2d, norm_w, w_main, w_dt, b_dt, *, tm, tn):
    n, d = x2d.shape
    c = w_main.shape[1]
    return pl.pallas_call(
        _inproj_body,
        grid=(n // tm, c // tn),
        in_specs=[pl.BlockSpec((tm, d), lambda i, j: (i, 0)),
                  pl.BlockSpec((1, d), lambda i, j: (0, 0)),
                  pl.BlockSpec((d, tn), lambda i, j: (0, j)),
                  pl.BlockSpec((d, LANES), lambda i, j: (0, 0)),
                  pl.BlockSpec((1, LANES), lambda i, j: (0, 0))],
        out_specs=[pl.BlockSpec((tm, tn), lambda i, j: (i, j)),
                   pl.BlockSpec((tm, LANES), lambda i, j: (i, 0))],
        out_shape=[jax.ShapeDtypeStruct((n, c), BF16), jax.ShapeDtypeStruct((n, LANES), F32)],
        scratch_shapes=[pltpu.VMEM((tm, d), BF16)],
        compiler_params=pltpu.CompilerParams(dimension_semantics=("arbitrary", "arbitrary"),
                                             vmem_limit_bytes=40 * 1024 * 1024),
        name="in_projection",
    )(x2d, norm_w.reshape(1, d), w_main, w_dt, b_dt)


SSD_HALO = 8
CONF_HALO = 32
CONV_COLS = 512


def _mixer_body(x_ref, z_ref, xs_ref, b_ref, c_ref, glua_ref, glub_ref, ga_ref, gb_ref, dt_ref,
                cw_ref, cb_ref, a_ref, expand_ref, dsk_ref, nw_ref, wso_ref,
                cdw_ref, cdb_ref, lnw_ref, lnb_ref, wco_ref, bco_ref, wo_ref,
                h_ref,
                ext_scr, act_scr, uext_scr, ushift_scr, state_scr, y_scr, conv_scr, *, L):
    step = pl.program_id(1)

    @pl.when(step == 0)
    def _():
        ext_scr[...] = jnp.zeros_like(ext_scr)
        uext_scr[0:CONF_HALO, :] = jnp.zeros((CONF_HALO, uext_scr.shape[1]), F32)
        state_scr[...] = jnp.zeros_like(state_scr)

    nx = xs_ref.shape[1]
    nb = b_ref.shape[1]
    ri = lax.broadcasted_iota(jnp.int32, (L, L), 0)
    ci = lax.broadcasted_iota(jnp.int32, (L, L), 1)
    row8 = lax.broadcasted_iota(jnp.int32, (SSD_HALO, CONV_COLS), 0)
    shifts = [jnp.where(ri - ci == d, 1.0, 0.0).astype(BF16) for d in range(1, SSD_CONV)]
    for j in range(ext_scr.shape[1] // CONV_COLS):
        cs = slice(j * CONV_COLS, (j + 1) * CONV_COLS)
        src, c0 = (xs_ref, 0) if cs.start < nx else ((b_ref, nx) if cs.start < nx + nb else (c_ref, nx + nb))
        cur = src[:, cs.start - c0:cs.stop - c0]
        cur32 = cur.astype(F32)
        halo = ext_scr[:, cs]
        acc = cb_ref[:, cs] + cw_ref[SSD_CONV - 1:SSD_CONV, cs] * cur32
        for d in range(1, SSD_CONV):
            sh = jnp.dot(shifts[d - 1], cur, preferred_element_type=F32)
            top = sh[0:SSD_HALO] + jnp.where(row8 < d, pltpu.roll(halo, d, 0), 0.0)
            acc = acc + cw_ref[SSD_CONV - 1 - d:SSD_CONV - d, cs] * jnp.concatenate([top, sh[SSD_HALO:]], axis=0)
        act_scr[:, cs] = _silu(acc)
        ext_scr[:, cs] = cur32[L - SSD_HALO:L]

    dt = dt_ref[...]
    la = dt * a_ref[...]
    causal = ri >= ci
    tril = jnp.where(causal, 1.0, 0.0)
    triu = jnp.where(ri <= ci, 1.0, 0.0)
    acum = jnp.dot(tril, la, precision=HIGHEST, preferred_element_type=F32)
    acum_t = jnp.dot(la.T, triu, precision=HIGHEST, preferred_element_type=F32)
    acum_last = acum[L - 1:L, :]
    stacked = jnp.concatenate([dt, jnp.exp(acum_last - acum), jnp.exp(acum)], axis=0)
    s_hi = stacked.astype(BF16)
    s_lo = (stacked - s_hi.astype(F32)).astype(BF16)
    ex = (jnp.dot(s_hi, expand_ref[...], preferred_element_type=F32)
          + jnp.dot(s_lo, expand_ref[...], preferred_element_type=F32))
    dt_x = ex[0:L]
    dec_x = ex[L:2 * L]
    eac_x = ex[2 * L:3 * L]
    elast_x = eac_x[L - 1:L, :]

    for g in range(SSD_GROUPS):
        gs = slice(g * GROUP_DIM, (g + 1) * GROUP_DIM)
        bg = act_scr[:, nx + g * SSD_STATE:nx + (g + 1) * SSD_STATE]
        cg = act_scr[:, nx + nb + g * SSD_STATE:nx + nb + (g + 1) * SSD_STATE]
        bgb = bg.astype(BF16)
        cgb = cg.astype(BF16)
        cb = lax.dot_general(cgb, bgb, (((1,), (1,)), ((), ())), preferred_element_type=F32)
        xg = act_scr[:, gs]
        xdt = xg * dt_x[:, gs]
        xdtb = xdt.astype(BF16)
        yd = []
        for r in range(HEADS_PER_GROUP):
            hd = g * HEADS_PER_GROUP + r
            seg = acum[:, hd:hd + 1] - acum_t[hd:hd + 1, :]
            lm = jnp.exp(jnp.where(causal, seg, -jnp.inf))
            m = (cb * lm).astype(BF16)
            yd.append(jnp.dot(m, xdtb[:, r * SSD_HEAD_DIM:(r + 1) * SSD_HEAD_DIM], preferred_element_type=F32))
        st = state_scr[g]
        y = (jnp.concatenate(yd, axis=1)
             + jnp.dot(cgb, st.astype(BF16), preferred_element_type=F32) * eac_x[:, gs]
             + dsk_ref[:, gs] * xg)
        state_scr[g] = st * elast_x[:, gs] + jnp.dot(bg.T.astype(BF16), (xdt * dec_x[:, gs]).astype(BF16),
                                                    preferred_element_type=F32)
        yz = y * _silu(z_ref[:, gs].astype(F32))
        y_scr[:, gs] = _rms(yz, nw_ref[:, gs]).astype(BF16)
    y_a = jnp.dot(y_scr[...], wso_ref[...], preferred_element_type=F32)

    uext_scr[CONF_HALO:CONF_HALO + L, :] = glua_ref[...].astype(F32) * _sigmoid(glub_ref[...].astype(F32))
    for j in range(uext_scr.shape[1] // CONV_COLS):
        cs = slice(j * CONV_COLS, (j + 1) * CONV_COLS)
        for sft in range(1, 8):
            ushift_scr[sft - 1] = uext_scr[sft:sft + L + CONF_HALO - 8, cs]
        acc = jnp.broadcast_to(cdb_ref[:, cs], (L, CONV_COLS))
        for k in range(CONF_KERNEL):
            off = CONF_HALO - (CONF_KERNEL - 1) + k
            q8, sft = (off // 8) * 8, off % 8
            tap = uext_scr[q8:q8 + L, cs] if sft == 0 else ushift_scr[sft - 1, q8:q8 + L, :]
            acc = acc + cdw_ref[k:k + 1, cs] * tap
        conv_scr[:, cs] = acc
    uext_scr[0:CONF_HALO, :] = uext_scr[L:L + CONF_HALO, :]
    u = conv_scr[...]
    mu = jnp.mean(u, axis=-1, keepdims=True)
    uc = u - mu
    un = uc * lax.rsqrt(jnp.mean(uc * uc, axis=-1, keepdims=True) + EPS) * lnw_ref[...] + lnb_ref[...]
    y_b = jnp.dot(_silu(un).astype(BF16), wco_ref[...], preferred_element_type=F32) + bco_ref[...]

    merged = _sigmoid(ga_ref[...].astype(F32)) * y_a + _sigmoid(gb_ref[...].astype(F32)) * y_b
    h_ref[...] = x_ref[...] + jnp.dot(merged.astype(BF16), wo_ref[...], preferred_element_type=F32)


def token_mixers(x2d, proj, dt, conv_w, conv_b, a_pad, expand, dskip_x, ssd_norm_w, w_ssd_out,
                 conv_dw_w, conv_dw_b, ln_w, ln_b, w_conv_out, b_conv_out, w_o, *, batch, L):
    n, d = x2d.shape
    spb = n // batch // L
    row = lambda b, c: b * spb + c
    col = lambda k, w=1: pl.BlockSpec((L, w * d), lambda b, c, k=k: (row(b, c), k))
    full = lambda a: pl.BlockSpec(a.shape, lambda b, c: (0,) * a.ndim)
    consts = [conv_w, conv_b, a_pad, expand, dskip_x, ssd_norm_w, w_ssd_out,
              conv_dw_w, conv_dw_b, ln_w, ln_b, w_conv_out, b_conv_out, w_o]
    nxbc = conv_w.shape[1]
    return pl.pallas_call(
        functools.partial(_mixer_body, L=L),
        grid=(batch, spb),
        in_specs=[pl.BlockSpec((L, d), lambda b, c: (row(b, c), 0)),
                  col(0, 2),
                  col(1, 2),
                  col(4), col(5),
                  col(6), col(7),
                  col(8), col(9),
                  pl.BlockSpec((L, LANES), lambda b, c: (row(b, c), 0))] + [full(a) for a in consts],
        out_specs=pl.BlockSpec((L, d), lambda b, c: (row(b, c), 0)),
        out_shape=jax.ShapeDtypeStruct((n, d), F32),
        scratch_shapes=[pltpu.VMEM((SSD_HALO, nxbc), F32),
                        pltpu.VMEM((L, nxbc), F32),
                        pltpu.VMEM((L + CONF_HALO, d), F32),
                        pltpu.VMEM((7, L + CONF_HALO - 8, CONV_COLS), F32),
                        pltpu.VMEM((SSD_GROUPS, SSD_STATE, GROUP_DIM), F32),
                        pltpu.VMEM((L, D_INNER), BF16),
                        pltpu.VMEM((L, d), F32)],
        compiler_params=pltpu.CompilerParams(dimension_semantics=("arbitrary", "arbitrary"),
                                             vmem_limit_bytes=56 * 1024 * 1024),
        name="token_mixers",
    )(x2d, proj, proj, proj, proj, proj, proj, proj, proj, dt, *consts)


NK = PEER_N_KEYS
TOPK = PEER_TOPK
NEG = float("-inf")


def _sort16_network():
    pairs, p = [], 1
    while p < 16:
        k = p
        while k >= 1:
            for j in range(k % p, 16 - k, 2 * k):
                for i in range(min(k, 16 - j - k)):
                    if (i + j) // (2 * p) == (i + j + k) // (2 * p):
                        pairs.append((i + j, i + j + k))
            k //= 2
        p *= 2
    return pairs


def _top16_of_slabs(vals, rows, tags=None):
    n = len(vals)
    lists = [vals, rows] + ([tags] if tags is not None else [])
    for a, b in _sort16_network():
        if b >= n:
            continue
        swap = (vals[b] > vals[a]) | ((vals[b] == vals[a]) & (rows[b] < rows[a]))
        for x in lists:
            x[a], x[b] = jnp.where(swap, x[b], x[a]), jnp.where(swap, x[a], x[b])
    no_row = n * 8
    out = [[] for _ in lists]
    for r in range(TOPK):
        m = jnp.max(vals[0], axis=0, keepdims=True)
        row = jnp.min(jnp.where(vals[0] == m, rows[0], no_row), axis=0, keepdims=True)
        pop = rows[0] == row
        out[0].append(m)
        out[1].append(row)
        if tags is not None:
            out[2].append(jnp.sum(jnp.where(pop, tags[0], 0), axis=0, keepdims=True))
        for j in range(min(n, TOPK - 1 - r)):
            for x in lists:
                below = x[j + 1] if j + 1 < n else (jnp.full_like(x[j], NEG) if x is vals else x[j])
                x[j] = jnp.where(pop, below, x[j])
    return [jnp.concatenate(o, axis=0) for o in out]


def _top16_rows(s):
    iota8 = lax.broadcasted_iota(jnp.int32, (8, s.shape[1]), 0)
    nslab = s.shape[0] // 8
    return _top16_of_slabs([s[8 * i:8 * i + 8] for i in range(nslab)], [iota8 + 8 * i for i in range(nslab)])


def _gelu(x):
    return 0.5 * x * (1.0 + lax.erf(x * np.float32(0.7071067811865476)))


def _pair_top16(sv0, si0, sv1, si1):
    t = sv0.shape[1]
    iota8 = lax.broadcasted_iota(jnp.int32, (8, t), 0)
    pv = [sv0[0:1] + sv1, sv0[1:2] + sv1[0:8]]
    pe = [si0[0:1] * NK + si1, si0[1:2] * NK + si1[0:8]]
    for p, n in ((2, 5), (3, 4), (4, 3), (5, 2), (6, 2), (7, 2)):
        pv.append(jnp.where(iota8 < n, sv0[p:p + 1] + sv1[0:8], NEG))
        pe.append(si0[p:p + 1] * NK + si1[0:8])
    pv.append(sv0[8:16] + sv1[0:1])
    pe.append(si0[8:16] * NK + si1[0:1])
    slabs_v = [pv[0][0:8], pv[0][8:16]] + pv[1:]
    slabs_e = [pe[0][0:8], pe[0][8:16]] + pe[1:]
    slabs_r = [iota8 + 8 * j for j in range(len(slabs_v))]
    best, _, experts = _top16_of_slabs(slabs_v, slabs_r, slabs_e)
    ex = jnp.exp(best - best[0:1])
    return experts, ex / jnp.sum(ex, axis=0, keepdims=True)


def _peer_body(hn_ref, nw_ref, wqt_ref, keys_ref, ut_ref, v_ref, o_ref,
               xnt_nxt, et_nxt, gt_nxt, xn_cur, e_cur, g_cur, s_scr, sv0_scr, si0_scr,
               act_scr, sc0_scr, sc1_scr, stg0_scr, stg1_scr, w3_scr, *, nblk, tile):
    i = pl.program_id(0)
    k = pl.program_id(1)
    eb = ut_ref.shape[1]
    cpb = eb // NK
    nsel = e_cur.shape[1]
    hb = eb // 2
    cph = cpb // 2

    def key_scores(c):
        q = jnp.dot(wqt_ref[c * NK:(c + 1) * NK, :], xnt_nxt[...], preferred_element_type=F32)
        return jnp.dot(keys_ref[c], q.astype(BF16), preferred_element_type=F32)

    def route_piece(c):
        v, idx = _top16_rows(s_scr[...])
        if c == 0:
            sv0_scr[...] = v
            si0_scr[...] = idx
        else:
            experts, gate = _pair_top16(sv0_scr[...], si0_scr[...], v, idx)
            r0 = pl.multiple_of((k // 2) * TOPK, TOPK)
            et_nxt[pl.ds(r0, TOPK), :] = experts
            gt_nxt[pl.ds(r0, TOPK), :] = gate
        s_scr[...] = key_scores(1 - c)

    @pl.when(k == 0)
    def _():
        @pl.when(i == 0)
        def _():
            xnt_nxt[...] = jnp.zeros_like(xnt_nxt)
            et_nxt[...] = jnp.zeros_like(et_nxt)
            gt_nxt[...] = jnp.zeros_like(gt_nxt)
            sc1_scr[...] = jnp.zeros_like(sc1_scr)

        xn_cur[...] = xnt_nxt[...].T
        e_cur[...] = et_nxt[...].T
        g_cur[...] = gt_nxt[...].T
        xn = _rms(hn_ref[...], nw_ref[...])
        xnt_nxt[...] = xn.astype(BF16).T
        s_scr[...] = key_scores(0)
        act_scr[...] = jnp.zeros_like(act_scr)

    def score_half(sc_ref, half):
        sc_ref[...] = jnp.dot(xn_cur[...], ut_ref[:, half * hb:(half + 1) * hb], preferred_element_type=F32)

    def pick_half(sc_ref, chunk0):
        e = e_cur[...]
        row = e >> 7
        col = e & (NK - 1)
        act = act_scr[...]
        for cc in range(cph):
            picked = jnp.take_along_axis(sc_ref[:, cc * NK:(cc + 1) * NK], col, axis=1)
            act = jnp.where(row == chunk0 + cc, picked, act)
        act_scr[...] = act

    def scatter_weights():
        act_scr[...] = g_cur[...] * _gelu(act_scr[...])
        iota = lax.broadcasted_iota(jnp.int32, (NK, nsel), 0).astype(F32).astype(BF16)
        one = jnp.ones((NK, nsel), BF16)
        zero = jnp.zeros((NK, nsel), BF16)

        def scatter_group(grp, stage_ref):
            t0 = pl.multiple_of(grp * 16, 16)
            e_rows = e_cur[pl.ds(t0, 16), :]
            i1_rows = (e_rows >> 7).astype(F32).astype(BF16)
            i2_rows = (e_rows & (NK - 1)).astype(F32).astype(BF16)
            w_rows = act_scr[pl.ds(t0, 16), :].astype(BF16)
            for j in range(16):
                pm = jnp.where(iota == i1_rows[j:j + 1], one, zero)
                qm = jnp.where(iota == i2_rows[j:j + 1], jnp.broadcast_to(w_rows[j:j + 1], (NK, nsel)), zero)
                stage_ref[j] = lax.dot_general(pm, qm, (((1,), (1,)), ((), ())),
                                               preferred_element_type=F32).astype(BF16)

        def swap_group(grp, stage_ref):
            t0 = pl.multiple_of(grp * 16, 16)
            w3_scr[:, pl.ds(t0, 16), :] = jnp.swapaxes(stage_ref[...], 0, 1)

        ngrp = tile // 16
        scatter_group(0, stg0_scr)

        def pair(i2, carry):
            scatter_group(2 * i2 + 1, stg1_scr)
            swap_group(2 * i2, stg0_scr)
            scatter_group(2 * i2 + 2, stg0_scr)
            swap_group(2 * i2 + 1, stg1_scr)
            return carry

        lax.fori_loop(0, ngrp // 2 - 1, pair, 0)
        scatter_group(ngrp - 1, stg1_scr)
        swap_group(ngrp - 2, stg0_scr)
        swap_group(ngrp - 1, stg1_scr)

    def weights_times_values():
        kk = k - nblk
        parts = [w3_scr[kk * cpb + cc] for cc in range(cpb)]
        return jnp.dot(jnp.concatenate(parts, axis=1), v_ref[...], preferred_element_type=F32)

    for c in range(2):
        @pl.when(jnp.logical_and(k < nblk, k % 2 == c))
        def _(c=c):
            route_piece(c)
            score_half(sc0_scr, 0)
            pick_half(sc1_scr, (k - 1) * cpb + cph)
            score_half(sc1_scr, 1)
            pick_half(sc0_scr, k * cpb)

    @pl.when(k == nblk)
    def _():
        pick_half(sc1_scr, (k - 1) * cpb + cph)
        scatter_weights()
        route_piece(nblk % 2)
        o_ref[...] = weights_times_values()

    for c in range(2):
        @pl.when(jnp.logical_and(k > nblk, k % 2 == c))
        def _(c=c):
            route_piece(c)
            o_ref[...] += weights_times_values()


def peer_mixer(h2d, norm_w, wqt, keys_hc, ut, v, *, tile, eb):
    n, d = h2d.shape
    heads = keys_hc.shape[0] // 2
    nsel = heads * TOPK
    ne = ut.shape[1]
    nblk = ne // eb
    assert nblk == heads, "one half-head is routed per grid step, so steps per tile = 2 * heads"
    nt = n // tile
    return pl.pallas_call(
        functools.partial(_peer_body, nblk=nblk, tile=tile),
        grid=(nt + 1, 2 * nblk),
        in_specs=[pl.BlockSpec((tile, d), lambda i, k: (jnp.minimum(i, nt - 1), 0)),
                  pl.BlockSpec((1, d), lambda i, k: (0, 0)),
                  pl.BlockSpec((2 * NK, d), lambda i, k: (jnp.minimum((k + 1) // 2, heads - 1), 0)),
                  pl.BlockSpec((2, NK, NK), lambda i, k: (jnp.minimum((k + 1) // 2, heads - 1), 0, 0)),
                  pl.BlockSpec((d, eb), lambda i, k: (0, jnp.minimum(k, nblk - 1))),
                  pl.BlockSpec((eb, d), lambda i, k: (jnp.maximum(k - nblk, 0), 0))],
        out_specs=pl.BlockSpec((tile, d), lambda i, k: (jnp.maximum(i - 1, 0), 0)),
        out_shape=jax.ShapeDtypeStruct((n, d), F32),
        scratch_shapes=[pltpu.VMEM((d, tile), BF16),
                        pltpu.VMEM((nsel, tile), jnp.int32),
                        pltpu.VMEM((nsel, tile), F32),
                        pltpu.VMEM((tile, d), BF16),
                        pltpu.VMEM((tile, nsel), jnp.int32),
                        pltpu.VMEM((tile, nsel), F32),
                        pltpu.VMEM((NK, tile), F32),
                        pltpu.VMEM((TOPK, tile), F32),
                        pltpu.VMEM((TOPK, tile), jnp.int32),
                        pltpu.VMEM((tile, nsel), F32),
                        pltpu.VMEM((tile, eb // 2), F32),
                        pltpu.VMEM((tile, eb // 2), F32),
                        pltpu.VMEM((16, NK, NK), BF16),
                        pltpu.VMEM((16, NK, NK), BF16),
                        pltpu.VMEM((ne // NK, tile, NK), BF16)],
        compiler_params=pltpu.CompilerParams(dimension_semantics=("arbitrary", "arbitrary"),
                                             vmem_limit_bytes=56 * 1024 * 1024),
        name="peer_mixer",
    )(h2d, norm_w.reshape(1, d), wqt, keys_hc, ut, v)


def _ple_body(h_ref, dpeer_ref, p_ref, nw_ref, wg_ref, wp_ref, fw_ref, o_ref):
    h = h_ref[...] + dpeer_ref[...]
    gate = _sigmoid(jnp.dot(_rms(h, nw_ref[...]).astype(BF16), wg_ref[...], preferred_element_type=F32))
    h = h + gate * jnp.dot(p_ref[...].astype(BF16), wp_ref[...], preferred_element_type=F32)
    o_ref[...] = _rms(h, fw_ref[...])


def ple_final(h2d, dpeer, p2d, norm_w, w_gate, w_proj, final_w, *, tm):
    n, d = h2d.shape
    pd = p2d.shape[1]
    return pl.pallas_call(
        _ple_body,
        grid=(n // tm,),
        in_specs=[pl.BlockSpec((tm, d), lambda i: (i, 0)),
                  pl.BlockSpec((tm, d), lambda i: (i, 0)),
                  pl.BlockSpec((tm, pd), lambda i: (i, 0)),
                  pl.BlockSpec((1, d), lambda i: (0, 0)),
                  pl.BlockSpec((d, d), lambda i: (0, 0)),
                  pl.BlockSpec((pd, d), lambda i: (0, 0)),
                  pl.BlockSpec((1, d), lambda i: (0, 0))],
        out_specs=pl.BlockSpec((tm, d), lambda i: (i, 0)),
        out_shape=jax.ShapeDtypeStruct((n, d), F32),
        compiler_params=pltpu.CompilerParams(dimension_semantics=("arbitrary",),
                                             vmem_limit_bytes=40 * 1024 * 1024),
        name="ple_final",
    )(h2d, dpeer, p2d, norm_w.reshape(1, d), w_gate, w_proj, final_w.reshape(1, d))


def kernel(x, p, norm_mix_w, w_in, conv_ssd_w, conv_ssd_b, dt_bias, a_log, d_skip,
           ssd_norm_w, w_ssd_out, conv_dw_w, conv_dw_b, conv_ln_w, conv_ln_b,
           w_conv_out, b_conv_out, w_o, norm_ffn_w, peer_wq, peer_keys, peer_u, peer_v,
           norm_ple_w, w_ple_gate, w_ple_proj, final_norm_w):
    bsz, s, d = x.shape
    x2d = x.reshape(bsz * s, d)
    i = 0
    r1 = lambda v: v.reshape(1, -1)

    col_xbc = D_INNER + conv_ssd_w.shape[2]
    col_dt = col_xbc + SSD_HEADS
    w_main = jnp.concatenate([w_in[i][:, :col_xbc], w_in[i][:, col_dt:]], axis=1).astype(BF16)
    w_dt = jnp.pad(w_in[i][:, col_xbc:col_dt], ((0, 0), (0, LANES - SSD_HEADS))).astype(BF16)
    b_dt = jnp.pad(dt_bias[i], (0, LANES - SSD_HEADS)).reshape(1, LANES)
    a_pad = jnp.pad(-jnp.exp(a_log[i]), (0, LANES - SSD_HEADS)).reshape(1, LANES)
    dskip_x = jnp.repeat(d_skip[i], SSD_HEAD_DIM).reshape(1, D_INNER)
    expand = (jnp.arange(D_INNER)[None, :] // SSD_HEAD_DIM == jnp.arange(LANES)[:, None]).astype(BF16)

    proj, dt = in_projection(x2d, norm_mix_w[i], w_main, w_dt, b_dt, tm=INPROJ_TM, tn=INPROJ_TN)
    h2d = token_mixers(x2d, proj, dt, conv_ssd_w[i], r1(conv_ssd_b[i]), a_pad, expand, dskip_x, r1(ssd_norm_w[i]),
                       w_ssd_out[i].astype(BF16), conv_dw_w[i], r1(conv_dw_b[i]), r1(conv_ln_w[i]),
                       r1(conv_ln_b[i]), w_conv_out[i].astype(BF16), r1(b_conv_out[i]), w_o[i].astype(BF16),
                       batch=bsz, L=MIX_L)

    wqt = peer_wq[i].T.astype(BF16)
    keys_hc = peer_keys[i].reshape(PEER_HEADS * 2, PEER_N_KEYS, PEER_HALF).astype(BF16)
    dpeer = peer_mixer(h2d, norm_ffn_w[i], wqt, keys_hc, peer_u[i].T.astype(BF16), peer_v[i].astype(BF16),
                       tile=PEER_TILE, eb=PEER_EXPERT_BLOCK)

    out = ple_final(h2d, dpeer, p[i].reshape(bsz * s, -1), norm_ple_w[i], w_ple_gate[i].astype(BF16),
                    w_ple_proj[i].astype(BF16), final_norm_w, tm=PLE_TM)
    return out.reshape(bsz, s, d)
```

```python
import functools
import jax
import jax.numpy as jnp
from jax import lax
import numpy as np
from jax.experimental import pallas as pl
from jax.experimental.pallas import tpu as pltpu

D_MODEL = 1024
D_INNER = 2 * D_MODEL
SSD_HEAD_DIM = 64
SSD_HEADS = D_INNER // SSD_HEAD_DIM
SSD_GROUPS = 8
SSD_STATE = 128
SSD_CONV = 4
HEADS_PER_GROUP = SSD_HEADS // SSD_GROUPS
GROUP_DIM = D_INNER // SSD_GROUPS
CONF_KERNEL = 31
PEER_HEADS = 8
PEER_N_KEYS = 128
PEER_TOPK = 16
PEER_HALF = 128
EPS = 1e-6
F32 = jnp.float32
BF16 = jnp.bfloat16
LANES = 128
HIGHEST = lax.Precision.HIGHEST

INPROJ_TM = 1024
INPROJ_TN = 2048
MIX_L = 256
PEER_TILE = 512
PEER_EXPERT_BLOCK = 2048
PLE_TM = 512


def _sigmoid(x):
    return 0.5 * jnp.tanh(0.5 * x) + 0.5


def _silu(x):
    h = 0.5 * x
    return h + h * jnp.tanh(h)


def _rms(x, w):
    return x * lax.rsqrt(jnp.mean(x * x, axis=-1, keepdims=True) + EPS) * w


def _inproj_body(x_ref, nw_ref, w_ref, wdt_ref, bdt_ref, o_ref, dt_ref, hn_scr):
    @pl.when(pl.program_id(1) == 0)
    def _():
        hn = _rms(x_ref[...], nw_ref[...]).astype(BF16)
        hn_scr[...] = hn
        v = jnp.dot(hn, wdt_ref[...], preferred_element_type=F32) + bdt_ref[...]
        dt_ref[...] = jnp.maximum(v, 0.0) + jnp.log(1.0 + jnp.exp(-jnp.abs(v)))

    o_ref[...] = jnp.dot(hn_scr[...], w_ref[...], preferred_element_type=F32).astype(BF16)


def in_projection(x2d, norm_w, w_main, w_dt, b_dt, *, tm, tn):
    n, d = x2d.shape
    c = w_main.shape[1]
    return pl.pallas_call(
        _inproj_body,
        grid=(n // tm, c // tn),
        in_specs=[pl.BlockSpec((tm, d), lambda i, j: (i, 0)),
                  pl.BlockSpec((1, d), lambda i, j: (0, 0)),
                  pl.BlockSpec((d, tn), lambda i, j: (0, j)),
                  pl.BlockSpec((d, LANES), lambda i, j: (0, 0)),
                  pl.BlockSpec((1, LANES), lambda i, j: (0, 0))],
        out_specs=[pl.BlockSpec((tm, tn), lambda i, j: (i, j)),
                   pl.BlockSpec((tm, LANES), lambda i, j: (i, 0))],
        out_shape=[jax.ShapeDtypeStruct((n, c), BF16), jax.ShapeDtypeStruct((n, LANES), F32)],
        scratch_shapes=[pltpu.VMEM((tm, d), BF16)],
        compiler_params=pltpu.CompilerParams(dimension_semantics=("arbitrary", "arbitrary"),
                                             vmem_limit_bytes=40 * 1024 * 1024),
        name="in_projection",
    )(x2d, norm_w.reshape(1, d), w_main, w_dt, b_dt)


SSD_HALO = 8
CONF_HALO = 32
CONV_COLS = 512


def _mixer_body(x_ref, z_ref, xs_ref, b_ref, c_ref, glua_ref, glub_ref, ga_ref, gb_ref, dt_ref,
                cw_ref, cb_ref, a_ref, expand_ref, dsk_ref, nw_ref, wso_ref,
                cdw_ref, cdb_ref, lnw_ref, lnb_ref, wco_ref, bco_ref, wo_ref,
                h_ref,
                ext_scr, act_scr, uext_scr, ushift_scr, state_scr, y_scr, conv_scr, *, L):
    step = pl.program_id(1)

    @pl.when(step == 0)
    def _():
        ext_scr[...] = jnp.zeros_like(ext_scr)
        uext_scr[0:CONF_HALO, :] = jnp.zeros((CONF_HALO, uext_scr.shape[1]), F32)
        state_scr[...] = jnp.zeros_like(state_scr)

    nx = xs_ref.shape[1]
    nb = b_ref.shape[1]
    ri = lax.broadcasted_iota(jnp.int32, (L, L), 0)
    ci = lax.broadcasted_iota(jnp.int32, (L, L), 1)
    row8 = lax.broadcasted_iota(jnp.int32, (SSD_HALO, CONV_COLS), 0)
    shifts = [jnp.where(ri - ci == d, 1.0, 0.0).astype(BF16) for d in range(1, SSD_CONV)]
    for j in range(ext_scr.shape[1] // CONV_COLS):
        cs = slice(j * CONV_COLS, (j + 1) * CONV_COLS)
        src, c0 = (xs_ref, 0) if cs.start < nx else ((b_ref, nx) if cs.start < nx + nb else (c_ref, nx + nb))
        cur = src[:, cs.start - c0:cs.stop - c0]
        cur32 = cur.astype(F32)
        halo = ext_scr[:, cs]
        acc = cb_ref[:, cs] + cw_ref[SSD_CONV - 1:SSD_CONV, cs] * cur32
        for d in range(1, SSD_CONV):
            sh = jnp.dot(shifts[d - 1], cur, preferred_element_type=F32)
            top = sh[0:SSD_HALO] + jnp.where(row8 < d, pltpu.roll(halo, d, 0), 0.0)
            acc = acc + cw_ref[SSD_CONV - 1 - d:SSD_CONV - d, cs] * jnp.concatenate([top, sh[SSD_HALO:]], axis=0)
        act_scr[:, cs] = _silu(acc)
        ext_scr[:, cs] = cur32[L - SSD_HALO:L]

    dt = dt_ref[...]
    la = dt * a_ref[...]
    causal = ri >= ci
    tril = jnp.where(causal, 1.0, 0.0)
    triu = jnp.where(ri <= ci, 1.0, 0.0)
    acum = jnp.dot(tril, la, precision=HIGHEST, preferred_element_type=F32)
    acum_t = jnp.dot(la.T, triu, precision=HIGHEST, preferred_element_type=F32)
    acum_last = acum[L - 1:L, :]
    stacked = jnp.concatenate([dt, jnp.exp(acum_last - acum), jnp.exp(acum)], axis=0)
    s_hi = stacked.astype(BF16)
    s_lo = (stacked - s_hi.astype(F32)).astype(BF16)
    ex = (jnp.dot(s_hi, expand_ref[...], preferred_element_type=F32)
          + jnp.dot(s_lo, expand_ref[...], preferred_element_type=F32))
    dt_x = ex[0:L]
    dec_x = ex[L:2 * L]
    eac_x = ex[2 * L:3 * L]
    elast_x = eac_x[L - 1:L, :]

    for g in range(SSD_GROUPS):
        gs = slice(g * GROUP_DIM, (g + 1) * GROUP_DIM)
        bg = act_scr[:, nx + g * SSD_STATE:nx + (g + 1) * SSD_STATE]
        cg = act_scr[:, nx + nb + g * SSD_STATE:nx + nb + (g + 1) * SSD_STATE]
        bgb = bg.astype(BF16)
        cgb = cg.astype(BF16)
        cb = lax.dot_general(cgb, bgb, (((1,), (1,)), ((), ())), preferred_element_type=F32)
        xg = act_scr[:, gs]
        xdt = xg * dt_x[:, gs]
        xdtb = xdt.astype(BF16)
        yd = []
        for r in range(HEADS_PER_GROUP):
            hd = g * HEADS_PER_GROUP + r
            seg = acum[:, hd:hd + 1] - acum_t[hd:hd + 1, :]
            lm = jnp.exp(jnp.where(causal, seg, -jnp.inf))
            m = (cb * lm).astype(BF16)
            yd.append(jnp.dot(m, xdtb[:, r * SSD_HEAD_DIM:(r + 1) * SSD_HEAD_DIM], preferred_element_type=F32))
        st = state_scr[g]
        y = (jnp.concatenate(yd, axis=1)
             + jnp.dot(cgb, st.astype(BF16), preferred_element_type=F32) * eac_x[:, gs]
             + dsk_ref[:, gs] * xg)
        state_scr[g] = st * elast_x[:, gs] + jnp.dot(bg.T.astype(BF16), (xdt * dec_x[:, gs]).astype(BF16),
                                                    preferred_element_type=F32)
        yz = y * _silu(z_ref[:, gs].astype(F32))
        y_scr[:, gs] = _rms(yz, nw_ref[:, gs]).astype(BF16)
    y_a = jnp.dot(y_scr[...], wso_ref[...], preferred_element_type=F32)

    uext_scr[CONF_HALO:CONF_HALO + L, :] = glua_ref[...].astype(F32) * _sigmoid(glub_ref[...].astype(F32))
    for j in range(uext_scr.shape[1] // CONV_COLS):
        cs = slice(j * CONV_COLS, (j + 1) * CONV_COLS)
        for sft in range(1, 8):
            ushift_scr[sft - 1] = uext_scr[sft:sft + L + CONF_HALO - 8, cs]
        acc = jnp.broadcast_to(cdb_ref[:, cs], (L, CONV_COLS))
        for k in range(CONF_KERNEL):
            off = CONF_HALO - (CONF_KERNEL - 1) + k
            q8, sft = (off // 8) * 8, off % 8
            tap = uext_scr[q8:q8 + L, cs] if sft == 0 else ushift_scr[sft - 1, q8:q8 + L, :]
            acc = acc + cdw_ref[k:k + 1, cs] * tap
        conv_scr[:, cs] = acc
    uext_scr[0:CONF_HALO, :] = uext_scr[L:L + CONF_HALO, :]
    u = conv_scr[...]
    mu = jnp.mean(u, axis=-1, keepdims=True)
    uc = u - mu
    un = uc * lax.rsqrt(jnp.mean(uc * uc, axis=-1, keepdims=True) + EPS) * lnw_ref[...] + lnb_ref[...]
    y_b = jnp.dot(_silu(un).astype(BF16), wco_ref[...], preferred_element_type=F32) + bco_ref[...]

    merged = _sigmoid(ga_ref[...].astype(F32)) * y_a + _sigmoid(gb_ref[...].astype(F32)) * y_b
    h_ref[...] = x_ref[...] + jnp.dot(merged.astype(BF16), wo_ref[...], preferred_element_type=F32)


def token_mixers(x2d, proj, dt, conv_w, conv_b, a_pad, expand, dskip_x, ssd_norm_w, w_ssd_out,
                 conv_dw_w, conv_dw_b, ln_w, ln_b, w_conv_out, b_conv_out, w_o, *, batch, L):
    n, d = x2d.shape
    spb = n // batch // L
    row = lambda b, c: b * spb + c
    col = lambda k, w=1: pl.BlockSpec((L, w * d), lambda b, c, k=k: (row(b, c), k))
    full = lambda a: pl.BlockSpec(a.shape, lambda b, c: (0,) * a.ndim)
    consts = [conv_w, conv_b, a_pad, expand, dskip_x, ssd_norm_w, w_ssd_out,
              conv_dw_w, conv_dw_b, ln_w, ln_b, w_conv_out, b_conv_out, w_o]
    nxbc = conv_w.shape[1]
    return pl.pallas_call(
        functools.partial(_mixer_body, L=L),
        grid=(batch, spb),
        in_specs=[pl.BlockSpec((L, d), lambda b, c: (row(b, c), 0)),
                  col(0, 2),
                  col(1, 2),
                  col(4), col(5),
                  col(6), col(7),
                  col(8), col(9),
                  pl.BlockSpec((L, LANES), lambda b, c: (row(b, c), 0))] + [full(a) for a in consts],
        out_specs=pl.BlockSpec((L, d), lambda b, c: (row(b, c), 0)),
        out_shape=jax.ShapeDtypeStruct((n, d), F32),
        scratch_shapes=[pltpu.VMEM((SSD_HALO, nxbc), F32),
                        pltpu.VMEM((L, nxbc), F32),
                        pltpu.VMEM((L + CONF_HALO, d), F32),
                        pltpu.VMEM((7, L + CONF_HALO - 8, CONV_COLS), F32),
                        pltpu.VMEM((SSD_GROUPS, SSD_STATE, GROUP_DIM), F32),
                        pltpu.VMEM((L, D_INNER), BF16),
                        pltpu.VMEM((L, d), F32)],
        compiler_params=pltpu.CompilerParams(dimension_semantics=("arbitrary", "arbitrary"),
                                             vmem_limit_bytes=56 * 1024 * 1024),
        name="token_mixers",
    )(x2d, proj, proj, proj, proj, proj, proj, proj, proj, dt, *consts)


NK = PEER_N_KEYS
NK_BITS = NK.bit_length() - 1
TOPK = PEER_TOPK
NEG = float("-inf")
BF16_ROWS = 16


def _sort16_network():
    pairs, p = [], 1
    while p < 16:
        k = p
        while k >= 1:
            for j in range(k % p, 16 - k, 2 * k):
                for i in range(min(k, 16 - j - k)):
                    if (i + j) // (2 * p) == (i + j + k) // (2 * p):
                        pairs.append((i + j, i + j + k))
            k //= 2
        p *= 2
    return pairs


def _top16_of_slabs(vals, rows, tags=None):
    n = len(vals)
    lists = [vals, rows] + ([tags] if tags is not None else [])
    for a, b in _sort16_network():
        if b >= n:
            continue
        swap = (vals[b] > vals[a]) | ((vals[b] == vals[a]) & (rows[b] < rows[a]))
        for x in lists:
            x[a], x[b] = jnp.where(swap, x[b], x[a]), jnp.where(swap, x[a], x[b])
    no_row = n * 8
    out = [[] for _ in lists]
    for r in range(TOPK):
        m = jnp.max(vals[0], axis=0, keepdims=True)
        row = jnp.min(jnp.where(vals[0] == m, rows[0], no_row), axis=0, keepdims=True)
        pop = rows[0] == row
        out[0].append(m)
        out[1].append(row)
        if tags is not None:
            out[2].append(jnp.sum(jnp.where(pop, tags[0], 0), axis=0, keepdims=True))
        for j in range(min(n, TOPK - 1 - r)):
            for x in lists:
                below = x[j + 1] if j + 1 < n else (jnp.full_like(x[j], NEG) if x is vals else x[j])
                x[j] = jnp.where(pop, below, x[j])
    return [jnp.concatenate(o, axis=0) for o in out]


def _top16_rows(s):
    iota8 = lax.broadcasted_iota(jnp.int32, (8, s.shape[1]), 0)
    nslab = s.shape[0] // 8
    return _top16_of_slabs([s[8 * i:8 * i + 8] for i in range(nslab)], [iota8 + 8 * i for i in range(nslab)])


def _gelu(x):
    return 0.5 * x * (1.0 + lax.erf(x * np.float32(0.7071067811865476)))


def _pair_top16(sv0, si0, sv1, si1):
    t = sv0.shape[1]
    iota8 = lax.broadcasted_iota(jnp.int32, (8, t), 0)
    pv = [sv0[0:1] + sv1, sv0[1:2] + sv1[0:8]]
    pe = [si0[0:1] * NK + si1, si0[1:2] * NK + si1[0:8]]
    for p, n in ((2, 5), (3, 4), (4, 3), (5, 2), (6, 2), (7, 2)):
        pv.append(jnp.where(iota8 < n, sv0[p:p + 1] + sv1[0:8], NEG))
        pe.append(si0[p:p + 1] * NK + si1[0:8])
    pv.append(sv0[8:16] + sv1[0:1])
    pe.append(si0[8:16] * NK + si1[0:1])
    slabs_v = [pv[0][0:8], pv[0][8:16]] + pv[1:]
    slabs_e = [pe[0][0:8], pe[0][8:16]] + pe[1:]
    slabs_r = [iota8 + 8 * j for j in range(len(slabs_v))]
    best, _, experts = _top16_of_slabs(slabs_v, slabs_r, slabs_e)
    ex = jnp.exp(best - best[0:1])
    return experts, ex / jnp.sum(ex, axis=0, keepdims=True)


def _peer_body(hn_ref, nw_ref, wqt_ref, keys_ref, ut_ref, v_ref, o_ref,
               xnt_nxt, et_nxt, gt_nxt, xn_cur, e_cur, g_cur, s_scr, sv0_scr, si0_scr,
               act_scr, sc0_scr, sc1_scr, stg0_scr, stg1_scr, w3_scr, *, nblk, tile):
    i = pl.program_id(0)
    k = pl.program_id(1)
    eb = ut_ref.shape[1]
    cpb = eb // NK
    nsel = e_cur.shape[1]
    hb = eb // 2
    cph = cpb // 2

    def key_scores(c):
        q = jnp.dot(wqt_ref[c * NK:(c + 1) * NK, :], xnt_nxt[...], preferred_element_type=F32)
        return jnp.dot(keys_ref[c], q.astype(BF16), preferred_element_type=F32)

    def route_piece(c):
        v, idx = _top16_rows(s_scr[...])
        if c == 0:
            sv0_scr[...] = v
            si0_scr[...] = idx
        else:
            experts, gate = _pair_top16(sv0_scr[...], si0_scr[...], v, idx)
            r0 = pl.multiple_of((k // 2) * TOPK, TOPK)
            et_nxt[pl.ds(r0, TOPK), :] = experts
            gt_nxt[pl.ds(r0, TOPK), :] = gate
        s_scr[...] = key_scores(1 - c)

    @pl.when(k == 0)
    def _():
        @pl.when(i == 0)
        def _():
            xnt_nxt[...] = jnp.zeros_like(xnt_nxt)
            et_nxt[...] = jnp.zeros_like(et_nxt)
            gt_nxt[...] = jnp.zeros_like(gt_nxt)
            sc1_scr[...] = jnp.zeros_like(sc1_scr)

        xn_cur[...] = xnt_nxt[...].T
        e_cur[...] = et_nxt[...].T
        g_cur[...] = gt_nxt[...].T
        xn = _rms(hn_ref[...], nw_ref[...])
        xnt_nxt[...] = xn.astype(BF16).T
        s_scr[...] = key_scores(0)
        act_scr[...] = jnp.zeros_like(act_scr)

    def score_half(sc_ref, half):
        sc_ref[...] = jnp.dot(xn_cur[...], ut_ref[:, half * hb:(half + 1) * hb], preferred_element_type=F32)

    def pick_half(sc_ref, chunk0):
        e = e_cur[...]
        row = e >> NK_BITS
        col = e & (NK - 1)
        act = act_scr[...]
        for cc in range(cph):
            picked = jnp.take_along_axis(sc_ref[:, cc * NK:(cc + 1) * NK], col, axis=1)
            act = jnp.where(row == chunk0 + cc, picked, act)
        act_scr[...] = act

    def scatter_weights():
        act_scr[...] = g_cur[...] * _gelu(act_scr[...])
        iota = lax.broadcasted_iota(jnp.int32, (NK, nsel), 0).astype(F32).astype(BF16)
        one = jnp.ones((NK, nsel), BF16)
        zero = jnp.zeros((NK, nsel), BF16)

        def scatter_group(grp, stage_ref):
            t0 = pl.multiple_of(grp * BF16_ROWS, BF16_ROWS)
            e_rows = e_cur[pl.ds(t0, BF16_ROWS), :]
            i1_rows = (e_rows >> NK_BITS).astype(F32).astype(BF16)
            i2_rows = (e_rows & (NK - 1)).astype(F32).astype(BF16)
            w_rows = act_scr[pl.ds(t0, BF16_ROWS), :].astype(BF16)
            for j in range(BF16_ROWS):
                pm = jnp.where(iota == i1_rows[j:j + 1], one, zero)
                qm = jnp.where(iota == i2_rows[j:j + 1], jnp.broadcast_to(w_rows[j:j + 1], (NK, nsel)), zero)
                stage_ref[j] = lax.dot_general(pm, qm, (((1,), (1,)), ((), ())),
                                               preferred_element_type=F32).astype(BF16)

        def swap_group(grp, stage_ref):
            t0 = pl.multiple_of(grp * BF16_ROWS, BF16_ROWS)
            w3_scr[:, pl.ds(t0, BF16_ROWS), :] = jnp.swapaxes(stage_ref[...], 0, 1)

        ngrp = tile // BF16_ROWS
        scatter_group(0, stg0_scr)

        def pair(i2, carry):
            scatter_group(2 * i2 + 1, stg1_scr)
            swap_group(2 * i2, stg0_scr)
            scatter_group(2 * i2 + 2, stg0_scr)
            swap_group(2 * i2 + 1, stg1_scr)
            return carry

        lax.fori_loop(0, ngrp // 2 - 1, pair, 0)
        scatter_group(ngrp - 1, stg1_scr)
        swap_group(ngrp - 2, stg0_scr)
        swap_group(ngrp - 1, stg1_scr)

    def weights_times_values():
        kk = k - nblk
        parts = [w3_scr[kk * cpb + cc] for cc in range(cpb)]
        return jnp.dot(jnp.concatenate(parts, axis=1), v_ref[...], preferred_element_type=F32)

    for c in range(2):
        @pl.when(jnp.logical_and(k < nblk, k % 2 == c))
        def _(c=c):
            route_piece(c)
            score_half(sc0_scr, 0)
            pick_half(sc1_scr, (k - 1) * cpb + cph)
            score_half(sc1_scr, 1)
            pick_half(sc0_scr, k * cpb)

    @pl.when(k == nblk)
    def _():
        pick_half(sc1_scr, (k - 1) * cpb + cph)
        scatter_weights()
        route_piece(nblk % 2)
        o_ref[...] = weights_times_values()

    for c in range(2):
        @pl.when(jnp.logical_and(k > nblk, k % 2 == c))
        def _(c=c):
            route_piece(c)
            o_ref[...] += weights_times_values()


def peer_mixer(h2d, norm_w, wqt, keys_hc, ut, v, *, tile, eb):
    n, d = h2d.shape
    heads = keys_hc.shape[0] // 2
    nsel = heads * TOPK
    ne = ut.shape[1]
    nblk = ne // eb
    assert nblk == heads, "one half-head is routed per grid step, so steps per tile = 2 * heads"
    nt = n // tile
    return pl.pallas_call(
        functools.partial(_peer_body, nblk=nblk, tile=tile),
        grid=(nt + 1, 2 * nblk),
        in_specs=[pl.BlockSpec((tile, d), lambda i, k: (jnp.minimum(i, nt - 1), 0)),
                  pl.BlockSpec((1, d), lambda i, k: (0, 0)),
                  pl.BlockSpec((2 * NK, d), lambda i, k: (jnp.minimum((k + 1) // 2, heads - 1), 0)),
                  pl.BlockSpec((2, NK, NK), lambda i, k: (jnp.minimum((k + 1) // 2, heads - 1), 0, 0)),
                  pl.BlockSpec((d, eb), lambda i, k: (0, jnp.minimum(k, nblk - 1))),
                  pl.BlockSpec((eb, d), lambda i, k: (jnp.maximum(k - nblk, 0), 0))],
        out_specs=pl.BlockSpec((tile, d), lambda i, k: (jnp.maximum(i - 1, 0), 0)),
        out_shape=jax.ShapeDtypeStruct((n, d), F32),
        scratch_shapes=[pltpu.VMEM((d, tile), BF16),
                        pltpu.VMEM((nsel, tile), jnp.int32),
                        pltpu.VMEM((nsel, tile), F32),
                        pltpu.VMEM((tile, d), BF16),
                        pltpu.VMEM((tile, nsel), jnp.int32),
                        pltpu.VMEM((tile, nsel), F32),
                        pltpu.VMEM((NK, tile), F32),
                        pltpu.VMEM((TOPK, tile), F32),
                        pltpu.VMEM((TOPK, tile), jnp.int32),
                        pltpu.VMEM((tile, nsel), F32),
                        pltpu.VMEM((tile, eb // 2), F32),
                        pltpu.VMEM((tile, eb // 2), F32),
                        pltpu.VMEM((BF16_ROWS, NK, NK), BF16),
                        pltpu.VMEM((BF16_ROWS, NK, NK), BF16),
                        pltpu.VMEM((ne // NK, tile, NK), BF16)],
        compiler_params=pltpu.CompilerParams(dimension_semantics=("arbitrary", "arbitrary"),
                                             vmem_limit_bytes=56 * 1024 * 1024),
        name="peer_mixer",
    )(h2d, norm_w.reshape(1, d), wqt, keys_hc, ut, v)


def _ple_body(h_ref, dpeer_ref, p_ref, nw_ref, wg_ref, wp_ref, fw_ref, o_ref):
    h = h_ref[...] + dpeer_ref[...]
    gate = _sigmoid(jnp.dot(_rms(h, nw_ref[...]).astype(BF16), wg_ref[...], preferred_element_type=F32))
    h = h + gate * jnp.dot(p_ref[...].astype(BF16), wp_ref[...], preferred_element_type=F32)
    o_ref[...] = _rms(h, fw_ref[...])


def ple_final(h2d, dpeer, p2d, norm_w, w_gate, w_proj, final_w, *, tm):
    n, d = h2d.shape
    pd = p2d.shape[1]
    return pl.pallas_call(
        _ple_body,
        grid=(n // tm,),
        in_specs=[pl.BlockSpec((tm, d), lambda i: (i, 0)),
                  pl.BlockSpec((tm, d), lambda i: (i, 0)),
                  pl.BlockSpec((tm, pd), lambda i: (i, 0)),
                  pl.BlockSpec((1, d), lambda i: (0, 0)),
                  pl.BlockSpec((d, d), lambda i: (0, 0)),
                  pl.BlockSpec((pd, d), lambda i: (0, 0)),
                  pl.BlockSpec((1, d), lambda i: (0, 0))],
        out_specs=pl.BlockSpec((tm, d), lambda i: (i, 0)),
        out_shape=jax.ShapeDtypeStruct((n, d), F32),
        compiler_params=pltpu.CompilerParams(dimension_semantics=("arbitrary",),
                                             vmem_limit_bytes=40 * 1024 * 1024),
        name="ple_final",
    )(h2d, dpeer, p2d, norm_w.reshape(1, d), w_gate, w_proj, final_w.reshape(1, d))


def kernel(x, p, norm_mix_w, w_in, conv_ssd_w, conv_ssd_b, dt_bias, a_log, d_skip,
           ssd_norm_w, w_ssd_out, conv_dw_w, conv_dw_b, conv_ln_w, conv_ln_b,
           w_conv_out, b_conv_out, w_o, norm_ffn_w, peer_wq, peer_keys, peer_u, peer_v,
           norm_ple_w, w_ple_gate, w_ple_proj, final_norm_w):
    bsz, s, d = x.shape
    x2d = x.reshape(bsz * s, d)
    i = 0
    r1 = lambda v: v.reshape(1, -1)

    col_xbc = D_INNER + conv_ssd_w.shape[2]
    col_dt = col_xbc + SSD_HEADS
    w_main = jnp.concatenate([w_in[i][:, :col_xbc], w_in[i][:, col_dt:]], axis=1).astype(BF16)
    w_dt = jnp.pad(w_in[i][:, col_xbc:col_dt], ((0, 0), (0, LANES - SSD_HEADS))).astype(BF16)
    b_dt = jnp.pad(dt_bias[i], (0, LANES - SSD_HEADS)).reshape(1, LANES)
    a_pad = jnp.pad(-jnp.exp(a_log[i]), (0, LANES - SSD_HEADS)).reshape(1, LANES)
    dskip_x = jnp.repeat(d_skip[i], SSD_HEAD_DIM).reshape(1, D_INNER)
    expand = (jnp.arange(D_INNER)[None, :] // SSD_HEAD_DIM == jnp.arange(LANES)[:, None]).astype(BF16)

    proj, dt = in_projection(x2d, norm_mix_w[i], w_main, w_dt, b_dt, tm=INPROJ_TM, tn=INPROJ_TN)
    h2d = token_mixers(x2d, proj, dt, conv_ssd_w[i], r1(conv_ssd_b[i]), a_pad, expand, dskip_x, r1(ssd_norm_w[i]),
                       w_ssd_out[i].astype(BF16), conv_dw_w[i], r1(conv_dw_b[i]), r1(conv_ln_w[i]),
                       r1(conv_ln_b[i]), w_conv_out[i].astype(BF16), r1(b_conv_out[i]), w_o[i].astype(BF16),
                       batch=bsz, L=MIX_L)

    wqt = peer_wq[i].T.astype(BF16)
    keys_hc = peer_keys[i].reshape(PEER_HEADS * 2, PEER_N_KEYS, PEER_HALF).astype(BF16)
    dpeer = peer_mixer(h2d, norm_ffn_w[i], wqt, keys_hc, peer_u[i].T.astype(BF16), peer_v[i].astype(BF16),
                       tile=PEER_TILE, eb=PEER_EXPERT_BLOCK)

    out = ple_final(h2d, dpeer, p[i].reshape(bsz * s, -1), norm_ple_w[i], w_ple_gate[i].astype(BF16),
                    w_ple_proj[i].astype(BF16), final_norm_w, tm=PLE_TM)
    return out.reshape(bsz, s, d)
```

```python
import functools
import jax
import jax.numpy as jnp
from jax import lax
import numpy as np
from jax.experimental import pallas as pl
from jax.experimental.pallas import tpu as pltpu

D_MODEL = 1024
D_INNER = 2 * D_MODEL
SSD_HEAD_DIM = 64
SSD_HEADS = D_INNER // SSD_HEAD_DIM
SSD_GROUPS = 8
SSD_STATE = 128
SSD_CONV = 4
HEADS_PER_GROUP = SSD_HEADS // SSD_GROUPS
GROUP_DIM = D_INNER // SSD_GROUPS
CONF_KERNEL = 31
PEER_HEADS = 8
PEER_N_KEYS = 128
PEER_TOPK = 16
PEER_HALF = 128
EPS = 1e-6
F32 = jnp.float32
BF16 = jnp.bfloat16
LANES = 128
HIGHEST = lax.Precision.HIGHEST

INPROJ_TM = 1024
INPROJ_TN = 2048
MIX_L = 256
PEER_TILE = 512
PEER_EXPERT_BLOCK = 2048
PLE_TM = 512


def _sigmoid(x):
    return 0.5 * jnp.tanh(0.5 * x) + 0.5


def _silu(x):
    h = 0.5 * x
    return h + h * jnp.tanh(h)


def _rms(x, w):
    return x * lax.rsqrt(jnp.mean(x * x, axis=-1, keepdims=True) + EPS) * w


def _inproj_body(x_ref, nw_ref, w_ref, wdt_ref, bdt_ref, o_ref, dt_ref, hn_scr):
    @pl.when(pl.program_id(1) == 0)
    def _():
        hn = _rms(x_ref[...], nw_ref[...]).astype(BF16)
        hn_scr[...] = hn
        v = jnp.dot(hn, wdt_ref[...], preferred_element_type=F32) + bdt_ref[...]
        dt_ref[...] = jnp.maximum(v, 0.0) + jnp.log(1.0 + jnp.exp(-jnp.abs(v)))

    o_ref[...] = jnp.dot(hn_scr[...], w_ref[...], preferred_element_type=F32).astype(BF16)


def in_projection(x2d, norm_w, w_main, w_dt, b_dt, *, tm, tn):
    n, d = x2d.shape
    c = w_main.shape[1]
    return pl.pallas_call(
        _inproj_body,
        grid=(n // tm, c // tn),
        in_specs=[pl.BlockSpec((tm, d), lambda i, j: (i, 0)),
                  pl.BlockSpec((1, d), lambda i, j: (0, 0)),
                  pl.BlockSpec((d, tn), lambda i, j: (0, j)),
                  pl.BlockSpec((d, LANES), lambda i, j: (0, 0)),
                  pl.BlockSpec((1, LANES), lambda i, j: (0, 0))],
        out_specs=[pl.BlockSpec((tm, tn), lambda i, j: (i, j)),
                   pl.BlockSpec((tm, LANES), lambda i, j: (i, 0))],
        out_shape=[jax.ShapeDtypeStruct((n, c), BF16), jax.ShapeDtypeStruct((n, LANES), F32)],
        scratch_shapes=[pltpu.VMEM((tm, d), BF16)],
        compiler_params=pltpu.CompilerParams(dimension_semantics=("arbitrary", "arbitrary"),
                                             vmem_limit_bytes=40 * 1024 * 1024),
        name="in_projection",
    )(x2d, norm_w.reshape(1, d), w_main, w_dt, b_dt)


SSD_HALO = 8
CONF_HALO = 32
CONV_COLS = 512


def _mixer_body(x_ref, z_ref, xs_ref, b_ref, c_ref, glua_ref, glub_ref, ga_ref, gb_ref, dt_ref,
                cw_ref, cb_ref, a_ref, expand_ref, dsk_ref, nw_ref, wso_ref,
                cdw_ref, cdb_ref, lnw_ref, lnb_ref, wco_ref, bco_ref, wo_ref,
                h_ref,
                ext_scr, act_scr, uext_scr, ushift_scr, state_scr, y_scr, conv_scr, *, L):
    step = pl.program_id(1)

    @pl.when(step == 0)
    def _():
        ext_scr[...] = jnp.zeros_like(ext_scr)
        uext_scr[0:CONF_HALO, :] = jnp.zeros((CONF_HALO, uext_scr.shape[1]), F32)
        state_scr[...] = jnp.zeros_like(state_scr)

    nx = xs_ref.shape[1]
    nb = b_ref.shape[1]
    ri = lax.broadcasted_iota(jnp.int32, (L, L), 0)
    ci = lax.broadcasted_iota(jnp.int32, (L, L), 1)
    row8 = lax.broadcasted_iota(jnp.int32, (SSD_HALO, CONV_COLS), 0)
    shifts = [jnp.where(ri - ci == d, 1.0, 0.0).astype(BF16) for d in range(1, SSD_CONV)]
    for j in range(ext_scr.shape[1] // CONV_COLS):
        cs = slice(j * CONV_COLS, (j + 1) * CONV_COLS)
        src, c0 = (xs_ref, 0) if cs.start < nx else ((b_ref, nx) if cs.start < nx + nb else (c_ref, nx + nb))
        cur = src[:, cs.start - c0:cs.stop - c0]
        cur32 = cur.astype(F32)
        halo = ext_scr[:, cs]
        acc = cb_ref[:, cs] + cw_ref[SSD_CONV - 1:SSD_CONV, cs] * cur32
        for d in range(1, SSD_CONV):
            sh = jnp.dot(shifts[d - 1], cur, preferred_element_type=F32)
            top = sh[0:SSD_HALO] + jnp.where(row8 < d, pltpu.roll(halo, d, 0), 0.0)
            acc = acc + cw_ref[SSD_CONV - 1 - d:SSD_CONV - d, cs] * jnp.concatenate([top, sh[SSD_HALO:]], axis=0)
        act_scr[:, cs] = _silu(acc)
        ext_scr[:, cs] = cur32[L - SSD_HALO:L]

    dt = dt_ref[...]
    la = dt * a_ref[...]
    causal = ri >= ci
    tril = jnp.where(causal, 1.0, 0.0)
    triu = jnp.where(ri <= ci, 1.0, 0.0)
    acum = jnp.dot(tril, la, precision=HIGHEST, preferred_element_type=F32)
    acum_t = jnp.dot(la.T, triu, precision=HIGHEST, preferred_element_type=F32)
    acum_last = acum[L - 1:L, :]
    stacked = jnp.concatenate([dt, jnp.exp(acum_last - acum), jnp.exp(acum)], axis=0)
    s_hi = stacked.astype(BF16)
    s_lo = (stacked - s_hi.astype(F32)).astype(BF16)
    ex = (jnp.dot(s_hi, expand_ref[...], preferred_element_type=F32)
          + jnp.dot(s_lo, expand_ref[...], preferred_element_type=F32))
    dt_x = ex[0:L]
    dec_x = ex[L:2 * L]
    eac_x = ex[2 * L:3 * L]
    elast_x = eac_x[L - 1:L, :]

    for g in range(SSD_GROUPS):
        gs = slice(g * GROUP_DIM, (g + 1) * GROUP_DIM)
        bg = act_scr[:, nx + g * SSD_STATE:nx + (g + 1) * SSD_STATE]
        cg = act_scr[:, nx + nb + g * SSD_STATE:nx + nb + (g + 1) * SSD_STATE]
        bgb = bg.astype(BF16)
        cgb = cg.astype(BF16)
        cb = lax.dot_general(cgb, bgb, (((1,), (1,)), ((), ())), preferred_element_type=F32)
        xg = act_scr[:, gs]
        xdt = xg * dt_x[:, gs]
        xdtb = xdt.astype(BF16)
        yd = []
        for r in range(HEADS_PER_GROUP):
            hd = g * HEADS_PER_GROUP + r
            seg = acum[:, hd:hd + 1] - acum_t[hd:hd + 1, :]
            lm = jnp.exp(jnp.where(causal, seg, -jnp.inf))
            m = (cb * lm).astype(BF16)
            yd.append(jnp.dot(m, xdtb[:, r * SSD_HEAD_DIM:(r + 1) * SSD_HEAD_DIM], preferred_element_type=F32))
        st = state_scr[g]
        y = (jnp.concatenate(yd, axis=1)
             + jnp.dot(cgb, st.astype(BF16), preferred_element_type=F32) * eac_x[:, gs]
             + dsk_ref[:, gs] * xg)
        state_scr[g] = st * elast_x[:, gs] + jnp.dot(bg.T.astype(BF16), (xdt * dec_x[:, gs]).astype(BF16),
                                                    preferred_element_type=F32)
        yz = y * _silu(z_ref[:, gs].astype(F32))
        y_scr[:, gs] = _rms(yz, nw_ref[:, gs]).astype(BF16)
    y_a = jnp.dot(y_scr[...], wso_ref[...], preferred_element_type=F32)

    uext_scr[CONF_HALO:CONF_HALO + L, :] = glua_ref[...].astype(F32) * _sigmoid(glub_ref[...].astype(F32))
    for j in range(uext_scr.shape[1] // CONV_COLS):
        cs = slice(j * CONV_COLS, (j + 1) * CONV_COLS)
        for sft in range(1, 8):
            ushift_scr[sft - 1] = uext_scr[sft:sft + L + CONF_HALO - 8, cs]
        acc = jnp.broadcast_to(cdb_ref[:, cs], (L, CONV_COLS))
        for k in range(CONF_KERNEL):
            off = CONF_HALO - (CONF_KERNEL - 1) + k
            q8, sft = (off // 8) * 8, off % 8
            tap = uext_scr[q8:q8 + L, cs] if sft == 0 else ushift_scr[sft - 1, q8:q8 + L, :]
            acc = acc + cdw_ref[k:k + 1, cs] * tap
        conv_scr[:, cs] = acc
    uext_scr[0:CONF_HALO, :] = uext_scr[L:L + CONF_HALO, :]
    u = conv_scr[...]
    mu = jnp.mean(u, axis=-1, keepdims=True)
    uc = u - mu
    un = uc * lax.rsqrt(jnp.mean(uc * uc, axis=-1, keepdims=True) + EPS) * lnw_ref[...] + lnb_ref[...]
    y_b = jnp.dot(_silu(un).astype(BF16), wco_ref[...], preferred_element_type=F32) + bco_ref[...]

    merged = _sigmoid(ga_ref[...].astype(F32)) * y_a + _sigmoid(gb_ref[...].astype(F32)) * y_b
    h_ref[...] = x_ref[...] + jnp.dot(merged.astype(BF16), wo_ref[...], preferred_element_type=F32)


def token_mixers(x2d, proj, dt, conv_w, conv_b, a_pad, expand, dskip_x, ssd_norm_w, w_ssd_out,
                 conv_dw_w, conv_dw_b, ln_w, ln_b, w_conv_out, b_conv_out, w_o, *, batch, L):
    n, d = x2d.shape
    spb = n // batch // L
    row = lambda b, c: b * spb + c
    col = lambda k, w=1: pl.BlockSpec((L, w * d), lambda b, c, k=k: (row(b, c), k))
    full = lambda a: pl.BlockSpec(a.shape, lambda b, c: (0,) * a.ndim)
    consts = [conv_w, conv_b, a_pad, expand, dskip_x, ssd_norm_w, w_ssd_out,
              conv_dw_w, conv_dw_b, ln_w, ln_b, w_conv_out, b_conv_out, w_o]
    nxbc = conv_w.shape[1]
    return pl.pallas_call(
        functools.partial(_mixer_body, L=L),
        grid=(batch, spb),
        in_specs=[pl.BlockSpec((L, d), lambda b, c: (row(b, c), 0)),
                  col(0, 2),
                  col(1, 2),
                  col(4), col(5),
                  col(6), col(7),
                  col(8), col(9),
                  pl.BlockSpec((L, LANES), lambda b, c: (row(b, c), 0))] + [full(a) for a in consts],
        out_specs=pl.BlockSpec((L, d), lambda b, c: (row(b, c), 0)),
        out_shape=jax.ShapeDtypeStruct((n, d), F32),
        scratch_shapes=[pltpu.VMEM((SSD_HALO, nxbc), F32),
                        pltpu.VMEM((L, nxbc), F32),
                        pltpu.VMEM((L + CONF_HALO, d), F32),
                        pltpu.VMEM((7, L + CONF_HALO - 8, CONV_COLS), F32),
                        pltpu.VMEM((SSD_GROUPS, SSD_STATE, GROUP_DIM), F32),
                        pltpu.VMEM((L, D_INNER), BF16),
                        pltpu.VMEM((L, d), F32)],
        compiler_params=pltpu.CompilerParams(dimension_semantics=("arbitrary", "arbitrary"),
                                             vmem_limit_bytes=56 * 1024 * 1024),
        name="token_mixers",
    )(x2d, proj, proj, proj, proj, proj, proj, proj, proj, dt, *consts)


NK = PEER_N_KEYS
NK_BITS = NK.bit_length() - 1
TOPK = PEER_TOPK
NEG = float("-inf")
BF16_ROWS = 16


def _sort16_network():
    pairs, p = [], 1
    while p < 16:
        k = p
        while k >= 1:
            for j in range(k % p, 16 - k, 2 * k):
                for i in range(min(k, 16 - j - k)):
                    if (i + j) // (2 * p) == (i + j + k) // (2 * p):
                        pairs.append((i + j, i + j + k))
            k //= 2
        p *= 2
    return pairs


def _top16_of_slabs(vals, rows, tags=None):
    n = len(vals)
    lists = [vals, rows] + ([tags] if tags is not None else [])
    for a, b in _sort16_network():
        if b >= n:
            continue
        swap = (vals[b] > vals[a]) | ((vals[b] == vals[a]) & (rows[b] < rows[a]))
        for x in lists:
            x[a], x[b] = jnp.where(swap, x[b], x[a]), jnp.where(swap, x[a], x[b])
    no_row = n * 8
    out = [[] for _ in lists]
    for r in range(TOPK):
        m = jnp.max(vals[0], axis=0, keepdims=True)
        row = jnp.min(jnp.where(vals[0] == m, rows[0], no_row), axis=0, keepdims=True)
        pop = rows[0] == row
        out[0].append(m)
        out[1].append(row)
        if tags is not None:
            out[2].append(jnp.sum(jnp.where(pop, tags[0], 0), axis=0, keepdims=True))
        for j in range(min(n, TOPK - 1 - r)):
            for x in lists:
                below = x[j + 1] if j + 1 < n else (jnp.full_like(x[j], NEG) if x is vals else x[j])
                x[j] = jnp.where(pop, below, x[j])
    return [jnp.concatenate(o, axis=0) for o in out]


def _top16_rows(s):
    iota8 = lax.broadcasted_iota(jnp.int32, (8, s.shape[1]), 0)
    nslab = s.shape[0] // 8
    return _top16_of_slabs([s[8 * i:8 * i + 8] for i in range(nslab)], [iota8 + 8 * i for i in range(nslab)])


def _gelu(x):
    return 0.5 * x * (1.0 + lax.erf(x * np.float32(0.7071067811865476)))


def _pair_top16(sv0, si0, sv1, si1):
    t = sv0.shape[1]
    iota8 = lax.broadcasted_iota(jnp.int32, (8, t), 0)
    pv = [sv0[0:1] + sv1, sv0[1:2] + sv1[0:8]]
    pe = [si0[0:1] * NK + si1, si0[1:2] * NK + si1[0:8]]
    for p, n in ((2, 5), (3, 4), (4, 3), (5, 2), (6, 2), (7, 2)):
        pv.append(jnp.where(iota8 < n, sv0[p:p + 1] + sv1[0:8], NEG))
        pe.append(si0[p:p + 1] * NK + si1[0:8])
    pv.append(sv0[8:16] + sv1[0:1])
    pe.append(si0[8:16] * NK + si1[0:1])
    slabs_v = [pv[0][0:8], pv[0][8:16]] + pv[1:]
    slabs_e = [pe[0][0:8], pe[0][8:16]] + pe[1:]
    slabs_r = [iota8 + 8 * j for j in range(len(slabs_v))]
    best, _, experts = _top16_of_slabs(slabs_v, slabs_r, slabs_e)
    ex = jnp.exp(best - best[0:1])
    return experts, ex / jnp.sum(ex, axis=0, keepdims=True)


def _peer_body(hn_ref, nw_ref, wqt_ref, keys_ref, ut_ref, v_ref, o_ref,
               xnt_nxt, et_nxt, gt_nxt, xn_cur, e_cur, g_cur, s_scr, sv0_scr, si0_scr,
               act_scr, sc0_scr, sc1_scr, stg0_scr, stg1_scr, w3_scr, *, nblk, tile):
    i = pl.program_id(0)
    k = pl.program_id(1)
    eb = ut_ref.shape[1]
    cpb = eb // NK
    nsel = e_cur.shape[1]
    hb = eb // 2
    cph = cpb // 2

    def key_scores(c):
        q = jnp.dot(wqt_ref[c * NK:(c + 1) * NK, :], xnt_nxt[...], preferred_element_type=F32)
        return jnp.dot(keys_ref[c], q.astype(BF16), preferred_element_type=F32)

    def route_piece(c):
        v, idx = _top16_rows(s_scr[...])
        if c == 0:
            sv0_scr[...] = v
            si0_scr[...] = idx
        else:
            experts, gate = _pair_top16(sv0_scr[...], si0_scr[...], v, idx)
            r0 = pl.multiple_of((k // 2) * TOPK, TOPK)
            et_nxt[pl.ds(r0, TOPK), :] = experts
            gt_nxt[pl.ds(r0, TOPK), :] = gate
        s_scr[...] = key_scores(1 - c)

    @pl.when(k == 0)
    def _():
        @pl.when(i == 0)
        def _():
            xnt_nxt[...] = jnp.zeros_like(xnt_nxt)
            et_nxt[...] = jnp.zeros_like(et_nxt)
            gt_nxt[...] = jnp.zeros_like(gt_nxt)
            sc1_scr[...] = jnp.zeros_like(sc1_scr)

        xn_cur[...] = xnt_nxt[...].T
        e_cur[...] = et_nxt[...].T
        g_cur[...] = gt_nxt[...].T
        xn = _rms(hn_ref[...], nw_ref[...])
        xnt_nxt[...] = xn.astype(BF16).T
        s_scr[...] = key_scores(0)
        act_scr[...] = jnp.zeros_like(act_scr)

    def score_half(sc_ref, half):
        sc_ref[...] = jnp.dot(xn_cur[...], ut_ref[:, half * hb:(half + 1) * hb], preferred_element_type=F32)

    def pick_half(sc_ref, chunk0):
        e = e_cur[...]
        row = e >> NK_BITS
        col = e & (NK - 1)
        act = act_scr[...]
        for cc in range(cph):
            picked = jnp.take_along_axis(sc_ref[:, cc * NK:(cc + 1) * NK], col, axis=1)
            act = jnp.where(row == chunk0 + cc, picked, act)
        act_scr[...] = act

    def scatter_weights():
        act_scr[...] = g_cur[...] * _gelu(act_scr[...])
        iota = lax.broadcasted_iota(jnp.int32, (NK, nsel), 0).astype(F32).astype(BF16)
        one = jnp.ones((NK, nsel), BF16)
        zero = jnp.zeros((NK, nsel), BF16)

        def scatter_group(grp, stage_ref):
            t0 = pl.multiple_of(grp * BF16_ROWS, BF16_ROWS)
            e_rows = e_cur[pl.ds(t0, BF16_ROWS), :]
            i1_rows = (e_rows >> NK_BITS).astype(F32).astype(BF16)
            i2_rows = (e_rows & (NK - 1)).astype(F32).astype(BF16)
            w_rows = act_scr[pl.ds(t0, BF16_ROWS), :].astype(BF16)
            for j in range(BF16_ROWS):
                pm = jnp.where(iota == i1_rows[j:j + 1], one, zero)
                qm = jnp.where(iota == i2_rows[j:j + 1], jnp.broadcast_to(w_rows[j:j + 1], (NK, nsel)), zero)
                stage_ref[j] = lax.dot_general(pm, qm, (((1,), (1,)), ((), ())),
                                               preferred_element_type=F32).astype(BF16)

        def swap_group(grp, stage_ref):
            t0 = pl.multiple_of(grp * BF16_ROWS, BF16_ROWS)
            w3_scr[:, pl.ds(t0, BF16_ROWS), :] = jnp.swapaxes(stage_ref[...], 0, 1)

        ngrp = tile // BF16_ROWS
        scatter_group(0, stg0_scr)

        def pair(i2, carry):
            scatter_group(2 * i2 + 1, stg1_scr)
            swap_group(2 * i2, stg0_scr)
            scatter_group(2 * i2 + 2, stg0_scr)
            swap_group(2 * i2 + 1, stg1_scr)
            return carry

        lax.fori_loop(0, ngrp // 2 - 1, pair, 0, unroll=5)
        scatter_group(ngrp - 1, stg1_scr)
        swap_group(ngrp - 2, stg0_scr)
        swap_group(ngrp - 1, stg1_scr)

    def weights_times_values():
        kk = k - nblk
        parts = [w3_scr[kk * cpb + cc] for cc in range(cpb)]
        return jnp.dot(jnp.concatenate(parts, axis=1), v_ref[...], preferred_element_type=F32)

    for c in range(2):
        @pl.when(jnp.logical_and(k < nblk, k % 2 == c))
        def _(c=c):
            route_piece(c)
            score_half(sc0_scr, 0)
            pick_half(sc1_scr, (k - 1) * cpb + cph)
            score_half(sc1_scr, 1)
            pick_half(sc0_scr, k * cpb)

    @pl.when(k == nblk)
    def _():
        pick_half(sc1_scr, (k - 1) * cpb + cph)
        scatter_weights()
        route_piece(nblk % 2)
        o_ref[...] = weights_times_values()

    for c in range(2):
        @pl.when(jnp.logical_and(k > nblk, k % 2 == c))
        def _(c=c):
            route_piece(c)
            o_ref[...] += weights_times_values()


def peer_mixer(h2d, norm_w, wqt, keys_hc, ut, v, *, tile, eb):
    n, d = h2d.shape
    heads = keys_hc.shape[0] // 2
    nsel = heads * TOPK
    ne = ut.shape[1]
    nblk = ne // eb
    assert nblk == heads, "one half-head is routed per grid step, so steps per tile = 2 * heads"
    nt = n // tile
    return pl.pallas_call(
        functools.partial(_peer_body, nblk=nblk, tile=tile),
        grid=(nt + 1, 2 * nblk),
        in_specs=[pl.BlockSpec((tile, d), lambda i, k: (jnp.minimum(i, nt - 1), 0)),
                  pl.BlockSpec((1, d), lambda i, k: (0, 0)),
                  pl.BlockSpec((2 * NK, d), lambda i, k: (jnp.minimum((k + 1) // 2, heads - 1), 0)),
                  pl.BlockSpec((2, NK, NK), lambda i, k: (jnp.minimum((k + 1) // 2, heads - 1), 0, 0)),
                  pl.BlockSpec((d, eb), lambda i, k: (0, jnp.minimum(k, nblk - 1))),
                  pl.BlockSpec((eb, d), lambda i, k: (jnp.maximum(k - nblk, 0), 0))],
        out_specs=pl.BlockSpec((tile, d), lambda i, k: (jnp.maximum(i - 1, 0), 0)),
        out_shape=jax.ShapeDtypeStruct((n, d), F32),
        scratch_shapes=[pltpu.VMEM((d, tile), BF16),
                        pltpu.VMEM((nsel, tile), jnp.int32),
                        pltpu.VMEM((nsel, tile), F32),
                        pltpu.VMEM((tile, d), BF16),
                        pltpu.VMEM((tile, nsel), jnp.int32),
                        pltpu.VMEM((tile, nsel), F32),
                        pltpu.VMEM((NK, tile), F32),
                        pltpu.VMEM((TOPK, tile), F32),
                        pltpu.VMEM((TOPK, tile), jnp.int32),
                        pltpu.VMEM((tile, nsel), F32),
                        pltpu.VMEM((tile, eb // 2), F32),
                        pltpu.VMEM((tile, eb // 2), F32),
                        pltpu.VMEM((BF16_ROWS, NK, NK), BF16),
                        pltpu.VMEM((BF16_ROWS, NK, NK), BF16),
                        pltpu.VMEM((ne // NK, tile, NK), BF16)],
        compiler_params=pltpu.CompilerParams(dimension_semantics=("arbitrary", "arbitrary"),
                                             vmem_limit_bytes=56 * 1024 * 1024),
        name="peer_mixer",
    )(h2d, norm_w.reshape(1, d), wqt, keys_hc, ut, v)


def _ple_body(h_ref, dpeer_ref, p_ref, nw_ref, wg_ref, wp_ref, fw_ref, o_ref):
    h = h_ref[...] + dpeer_ref[...]
    gate = _sigmoid(jnp.dot(_rms(h, nw_ref[...]).astype(BF16), wg_ref[...], preferred_element_type=F32))
    h = h + gate * jnp.dot(p_ref[...].astype(BF16), wp_ref[...], preferred_element_type=F32)
    o_ref[...] = _rms(h, fw_ref[...])


def ple_final(h2d, dpeer, p2d, norm_w, w_gate, w_proj, final_w, *, tm):
    n, d = h2d.shape
    pd = p2d.shape[1]
    return pl.pallas_call(
        _ple_body,
        grid=(n // tm,),
        in_specs=[pl.BlockSpec((tm, d), lambda i: (i, 0)),
                  pl.BlockSpec((tm, d), lambda i: (i, 0)),
                  pl.BlockSpec((tm, pd), lambda i: (i, 0)),
                  pl.BlockSpec((1, d), lambda i: (0, 0)),
                  pl.BlockSpec((d, d), lambda i: (0, 0)),
                  pl.BlockSpec((pd, d), lambda i: (0, 0)),
                  pl.BlockSpec((1, d), lambda i: (0, 0))],
        out_specs=pl.BlockSpec((tm, d), lambda i: (i, 0)),
        out_shape=jax.ShapeDtypeStruct((n, d), F32),
        compiler_params=pltpu.CompilerParams(dimension_semantics=("arbitrary",),
                                             vmem_limit_bytes=40 * 1024 * 1024),
        name="ple_final",
    )(h2d, dpeer, p2d, norm_w.reshape(1, d), w_gate, w_proj, final_w.reshape(1, d))


def kernel(x, p, norm_mix_w, w_in, conv_ssd_w, conv_ssd_b, dt_bias, a_log, d_skip,
           ssd_norm_w, w_ssd_out, conv_dw_w, conv_dw_b, conv_ln_w, conv_ln_b,
           w_conv_out, b_conv_out, w_o, norm_ffn_w, peer_wq, peer_keys, peer_u, peer_v,
           norm_ple_w, w_ple_gate, w_ple_proj, final_norm_w):
    bsz, s, d = x.shape
    x2d = x.reshape(bsz * s, d)
    i = 0
    r1 = lambda v: v.reshape(1, -1)

    col_xbc = D_INNER + conv_ssd_w.shape[2]
    col_dt = col_xbc + SSD_HEADS
    w_main = jnp.concatenate([w_in[i][:, :col_xbc], w_in[i][:, col_dt:]], axis=1).astype(BF16)
    w_dt = jnp.pad(w_in[i][:, col_xbc:col_dt], ((0, 0), (0, LANES - SSD_HEADS))).astype(BF16)
    b_dt = jnp.pad(dt_bias[i], (0, LANES - SSD_HEADS)).reshape(1, LANES)
    a_pad = jnp.pad(-jnp.exp(a_log[i]), (0, LANES - SSD_HEADS)).reshape(1, LANES)
    dskip_x = jnp.repeat(d_skip[i], SSD_HEAD_DIM).reshape(1, D_INNER)
    expand = (jnp.arange(D_INNER)[None, :] // SSD_HEAD_DIM == jnp.arange(LANES)[:, None]).astype(BF16)

    proj, dt = in_projection(x2d, norm_mix_w[i], w_main, w_dt, b_dt, tm=INPROJ_TM, tn=INPROJ_TN)
    h2d = token_mixers(x2d, proj, dt, conv_ssd_w[i], r1(conv_ssd_b[i]), a_pad, expand, dskip_x, r1(ssd_norm_w[i]),
                       w_ssd_out[i].astype(BF16), conv_dw_w[i], r1(conv_dw_b[i]), r1(conv_ln_w[i]),
                       r1(conv_ln_b[i]), w_conv_out[i].astype(BF16), r1(b_conv_out[i]), w_o[i].astype(BF16),
                       batch=bsz, L=MIX_L)

    wqt = peer_wq[i].T.astype(BF16)
    keys_hc = peer_keys[i].reshape(PEER_HEADS * 2, PEER_N_KEYS, PEER_HALF).astype(BF16)
    dpeer = peer_mixer(h2d, norm_ffn_w[i], wqt, keys_hc, peer_u[i].T.astype(BF16), peer_v[i].astype(BF16),
                       tile=PEER_TILE, eb=PEER_EXPERT_BLOCK)

    out = ple_final(h2d, dpeer, p[i].reshape(bsz * s, -1), norm_ple_w[i], w_ple_gate[i].astype(BF16),
                    w_ple_proj[i].astype(BF16), final_norm_w, tm=PLE_TM)
    return out.reshape(bsz, s, d)
```

```python
import functools
import jax
import jax.numpy as jnp
from jax import lax
import numpy as np
from jax.experimental import pallas as pl
from jax.experimental.pallas import tpu as pltpu

D_MODEL = 1024
D_INNER = 2 * D_MODEL
SSD_HEAD_DIM = 64
SSD_HEADS = D_INNER // SSD_HEAD_DIM
SSD_GROUPS = 8
SSD_STATE = 128
SSD_CONV = 4
HEADS_PER_GROUP = SSD_HEADS // SSD_GROUPS
GROUP_DIM = D_INNER // SSD_GROUPS
CONF_KERNEL = 31
PEER_HEADS = 8
PEER_N_KEYS = 128
PEER_TOPK = 16
PEER_HALF = 128
EPS = 1e-6
F32 = jnp.float32
BF16 = jnp.bfloat16
LANES = 128
HIGHEST = lax.Precision.HIGHEST

INPROJ_TM = 1024
INPROJ_TN = 2048
MIX_L = 256
PEER_TILE = 512
PEER_EXPERT_BLOCK = 2048
PLE_TM = 512


def _sigmoid(x):
    return 0.5 * jnp.tanh(0.5 * x) + 0.5


def _silu(x):
    h = 0.5 * x
    return h + h * jnp.tanh(h)


def _rms(x, w):
    return x * lax.rsqrt(jnp.mean(x * x, axis=-1, keepdims=True) + EPS) * w


def _inproj_body(x_ref, nw_ref, w_ref, wdt_ref, bdt_ref, o_ref, dt_ref, hn_scr):
    @pl.when(pl.program_id(1) == 0)
    def _():
        hn = _rms(x_ref[...], nw_ref[...]).astype(BF16)
        hn_scr[...] = hn
        v = jnp.dot(hn, wdt_ref[...], preferred_element_type=F32) + bdt_ref[...]
        dt_ref[...] = jnp.maximum(v, 0.0) + jnp.log(1.0 + jnp.exp(-jnp.abs(v)))

    o_ref[...] = jnp.dot(hn_scr[...], w_ref[...], preferred_element_type=F32).astype(BF16)


def in_projection(x2d, norm_w, w_main, w_dt, b_dt, *, tm, tn):
    n, d = x2d.shape
    c = w_main.shape[1]
    return pl.pallas_call(
        _inproj_body,
        grid=(n // tm, c // tn),
        in_specs=[pl.BlockSpec((tm, d), lambda i, j: (i, 0)),
                  pl.BlockSpec((1, d), lambda i, j: (0, 0)),
                  pl.BlockSpec((d, tn), lambda i, j: (0, j)),
                  pl.BlockSpec((d, LANES), lambda i, j: (0, 0)),
                  pl.BlockSpec((1, LANES), lambda i, j: (0, 0))],
        out_specs=[pl.BlockSpec((tm, tn), lambda i, j: (i, j)),
                   pl.BlockSpec((tm, LANES), lambda i, j: (i, 0))],
        out_shape=[jax.ShapeDtypeStruct((n, c), BF16), jax.ShapeDtypeStruct((n, LANES), F32)],
        scratch_shapes=[pltpu.VMEM((tm, d), BF16)],
        compiler_params=pltpu.CompilerParams(dimension_semantics=("arbitrary", "arbitrary"),
                                             vmem_limit_bytes=40 * 1024 * 1024),
        name="in_projection",
    )(x2d, norm_w.reshape(1, d), w_main, w_dt, b_dt)


SSD_HALO = 8
CONF_HALO = 32
CONV_COLS = 512


def _mixer_body(x_ref, z_ref, xs_ref, b_ref, c_ref, glua_ref, glub_ref, ga_ref, gb_ref, dt_ref,
                cw_ref, cb_ref, a_ref, expand_ref, dsk_ref, nw_ref, wso_ref,
                cdw_ref, cdb_ref, lnw_ref, lnb_ref, wco_ref, bco_ref, wo_ref,
                h_ref,
                ext_scr, act_scr, uext_scr, ushift_scr, state_scr, y_scr, conv_scr, *, L):
    step = pl.program_id(1)

    @pl.when(step == 0)
    def _():
        ext_scr[...] = jnp.zeros_like(ext_scr)
        uext_scr[0:CONF_HALO, :] = jnp.zeros((CONF_HALO, uext_scr.shape[1]), F32)
        state_scr[...] = jnp.zeros_like(state_scr)

    nx = xs_ref.shape[1]
    nb = b_ref.shape[1]
    ri = lax.broadcasted_iota(jnp.int32, (L, L), 0)
    ci = lax.broadcasted_iota(jnp.int32, (L, L), 1)
    row8 = lax.broadcasted_iota(jnp.int32, (SSD_HALO, CONV_COLS), 0)
    shifts = [jnp.where(ri - ci == d, 1.0, 0.0).astype(BF16) for d in range(1, SSD_CONV)]
    for j in range(ext_scr.shape[1] // CONV_COLS):
        cs = slice(j * CONV_COLS, (j + 1) * CONV_COLS)
        src, c0 = (xs_ref, 0) if cs.start < nx else ((b_ref, nx) if cs.start < nx + nb else (c_ref, nx + nb))
        cur = src[:, cs.start - c0:cs.stop - c0]
        cur32 = cur.astype(F32)
        halo = ext_scr[:, cs]
        acc = cb_ref[:, cs] + cw_ref[SSD_CONV - 1:SSD_CONV, cs] * cur32
        for d in range(1, SSD_CONV):
            sh = jnp.dot(shifts[d - 1], cur, preferred_element_type=F32)
            top = sh[0:SSD_HALO] + jnp.where(row8 < d, pltpu.roll(halo, d, 0), 0.0)
            acc = acc + cw_ref[SSD_CONV - 1 - d:SSD_CONV - d, cs] * jnp.concatenate([top, sh[SSD_HALO:]], axis=0)
        act_scr[:, cs] = _silu(acc)
        ext_scr[:, cs] = cur32[L - SSD_HALO:L]

    dt = dt_ref[...]
    la = dt * a_ref[...]
    causal = ri >= ci
    H = L // 2
    causal_h = causal[0:H, 0:H]
    tril = jnp.where(causal, 1.0, 0.0)
    triu = jnp.where(ri <= ci, 1.0, 0.0)
    acum = jnp.dot(tril, la, precision=HIGHEST, preferred_element_type=F32)
    acum_t = jnp.dot(la.T, triu, precision=HIGHEST, preferred_element_type=F32)
    acum_last = acum[L - 1:L, :]
    stacked = jnp.concatenate([dt, jnp.exp(acum_last - acum), jnp.exp(acum)], axis=0)
    s_hi = stacked.astype(BF16)
    s_lo = (stacked - s_hi.astype(F32)).astype(BF16)
    ex = (jnp.dot(s_hi, expand_ref[...], preferred_element_type=F32)
          + jnp.dot(s_lo, expand_ref[...], preferred_element_type=F32))
    dt_x = ex[0:L]
    dec_x = ex[L:2 * L]
    eac_x = ex[2 * L:3 * L]
    elast_x = eac_x[L - 1:L, :]

    for g in range(SSD_GROUPS):
        gs = slice(g * GROUP_DIM, (g + 1) * GROUP_DIM)
        bg = act_scr[:, nx + g * SSD_STATE:nx + (g + 1) * SSD_STATE]
        cg = act_scr[:, nx + nb + g * SSD_STATE:nx + nb + (g + 1) * SSD_STATE]
        bgb = bg.astype(BF16)
        cgb = cg.astype(BF16)
        cb = lax.dot_general(cgb, bgb, (((1,), (1,)), ((), ())), preferred_element_type=F32)
        xg = act_scr[:, gs]
        xdt = xg * dt_x[:, gs]
        xdtb = xdt.astype(BF16)
        yd = []
        for r in range(HEADS_PER_GROUP):
            hd = g * HEADS_PER_GROUP + r
            a_col = acum[:, hd:hd + 1]
            a_row = acum_t[hd:hd + 1, :]
            xh = xdtb[:, r * SSD_HEAD_DIM:(r + 1) * SSD_HEAD_DIM]
            m_tl = cb[0:H, 0:H] * jnp.exp(jnp.where(causal_h, a_col[0:H] - a_row[:, 0:H], -jnp.inf))
            m_bl = cb[H:L, 0:H] * jnp.exp(a_col[H:L] - a_row[:, 0:H])
            m_br = cb[H:L, H:L] * jnp.exp(jnp.where(causal_h, a_col[H:L] - a_row[:, H:L], -jnp.inf))
            y_top = jnp.dot(m_tl.astype(BF16), xh[0:H], preferred_element_type=F32)
            y_bot = jnp.dot(jnp.concatenate([m_bl, m_br], axis=1).astype(BF16), xh, preferred_element_type=F32)
            yd.append(jnp.concatenate([y_top, y_bot], axis=0))
        st = state_scr[g]
        y = (jnp.concatenate(yd, axis=1)
             + jnp.dot(cgb, st.astype(BF16), preferred_element_type=F32) * eac_x[:, gs]
             + dsk_ref[:, gs] * xg)
        state_scr[g] = st * elast_x[:, gs] + jnp.dot(bg.T.astype(BF16), (xdt * dec_x[:, gs]).astype(BF16),
                                                    preferred_element_type=F32)
        yz = y * _silu(z_ref[:, gs].astype(F32))
        y_scr[:, gs] = _rms(yz, nw_ref[:, gs]).astype(BF16)
    y_a = jnp.dot(y_scr[...], wso_ref[...], preferred_element_type=F32)

    uext_scr[CONF_HALO:CONF_HALO + L, :] = glua_ref[...].astype(F32) * _sigmoid(glub_ref[...].astype(F32))
    for j in range(uext_scr.shape[1] // CONV_COLS):
        cs = slice(j * CONV_COLS, (j + 1) * CONV_COLS)
        for sft in range(1, 8):
            ushift_scr[sft - 1] = uext_scr[sft:sft + L + CONF_HALO - 8, cs]
        acc = jnp.broadcast_to(cdb_ref[:, cs], (L, CONV_COLS))
        for k in range(CONF_KERNEL):
            off = CONF_HALO - (CONF_KERNEL - 1) + k
            q8, sft = (off // 8) * 8, off % 8
            tap = uext_scr[q8:q8 + L, cs] if sft == 0 else ushift_scr[sft - 1, q8:q8 + L, :]
            acc = acc + cdw_ref[k:k + 1, cs] * tap
        conv_scr[:, cs] = acc
    uext_scr[0:CONF_HALO, :] = uext_scr[L:L + CONF_HALO, :]
    u = conv_scr[...]
    mu = jnp.mean(u, axis=-1, keepdims=True)
    uc = u - mu
    un = uc * lax.rsqrt(jnp.mean(uc * uc, axis=-1, keepdims=True) + EPS) * lnw_ref[...] + lnb_ref[...]
    y_b = jnp.dot(_silu(un).astype(BF16), wco_ref[...], preferred_element_type=F32) + bco_ref[...]

    merged = _sigmoid(ga_ref[...].astype(F32)) * y_a + _sigmoid(gb_ref[...].astype(F32)) * y_b
    h_ref[...] = x_ref[...] + jnp.dot(merged.astype(BF16), wo_ref[...], preferred_element_type=F32)


def token_mixers(x2d, proj, dt, conv_w, conv_b, a_pad, expand, dskip_x, ssd_norm_w, w_ssd_out,
                 conv_dw_w, conv_dw_b, ln_w, ln_b, w_conv_out, b_conv_out, w_o, *, batch, L):
    n, d = x2d.shape
    spb = n // batch // L
    row = lambda b, c: b * spb + c
    col = lambda k, w=1: pl.BlockSpec((L, w * d), lambda b, c, k=k: (row(b, c), k))
    full = lambda a: pl.BlockSpec(a.shape, lambda b, c: (0,) * a.ndim)
    consts = [conv_w, conv_b, a_pad, expand, dskip_x, ssd_norm_w, w_ssd_out,
              conv_dw_w, conv_dw_b, ln_w, ln_b, w_conv_out, b_conv_out, w_o]
    nxbc = conv_w.shape[1]
    return pl.pallas_call(
        functools.partial(_mixer_body, L=L),
        grid=(batch, spb),
        in_specs=[pl.BlockSpec((L, d), lambda b, c: (row(b, c), 0)),
                  col(0, 2),
                  col(1, 2),
                  col(4), col(5),
                  col(6), col(7),
                  col(8), col(9),
                  pl.BlockSpec((L, LANES), lambda b, c: (row(b, c), 0))] + [full(a) for a in consts],
        out_specs=pl.BlockSpec((L, d), lambda b, c: (row(b, c), 0)),
        out_shape=jax.ShapeDtypeStruct((n, d), F32),
        scratch_shapes=[pltpu.VMEM((SSD_HALO, nxbc), F32),
                        pltpu.VMEM((L, nxbc), F32),
                        pltpu.VMEM((L + CONF_HALO, d), F32),
                        pltpu.VMEM((7, L + CONF_HALO - 8, CONV_COLS), F32),
                        pltpu.VMEM((SSD_GROUPS, SSD_STATE, GROUP_DIM), F32),
                        pltpu.VMEM((L, D_INNER), BF16),
                        pltpu.VMEM((L, d), F32)],
        compiler_params=pltpu.CompilerParams(dimension_semantics=("arbitrary", "arbitrary"),
                                             vmem_limit_bytes=56 * 1024 * 1024),
        name="token_mixers",
    )(x2d, proj, proj, proj, proj, proj, proj, proj, proj, dt, *consts)


NK = PEER_N_KEYS
NK_BITS = NK.bit_length() - 1
TOPK = PEER_TOPK
NEG = float("-inf")
BF16_ROWS = 16


def _sort16_network():
    pairs, p = [], 1
    while p < 16:
        k = p
        while k >= 1:
            for j in range(k % p, 16 - k, 2 * k):
                for i in range(min(k, 16 - j - k)):
                    if (i + j) // (2 * p) == (i + j + k) // (2 * p):
                        pairs.append((i + j, i + j + k))
            k //= 2
        p *= 2
    return pairs


def _top16_of_slabs(vals, rows, tags=None):
    n = len(vals)
    lists = [vals, rows] + ([tags] if tags is not None else [])
    for a, b in _sort16_network():
        if b >= n:
            continue
        swap = (vals[b] > vals[a]) | ((vals[b] == vals[a]) & (rows[b] < rows[a]))
        for x in lists:
            x[a], x[b] = jnp.where(swap, x[b], x[a]), jnp.where(swap, x[a], x[b])
    no_row = n * 8
    out = [[] for _ in lists]
    for r in range(TOPK):
        m = jnp.max(vals[0], axis=0, keepdims=True)
        row = jnp.min(jnp.where(vals[0] == m, rows[0], no_row), axis=0, keepdims=True)
        pop = rows[0] == row
        out[0].append(m)
        out[1].append(row)
        if tags is not None:
            out[2].append(jnp.sum(jnp.where(pop, tags[0], 0), axis=0, keepdims=True))
        for j in range(min(n, TOPK - 1 - r)):
            for x in lists:
                below = x[j + 1] if j + 1 < n else (jnp.full_like(x[j], NEG) if x is vals else x[j])
                x[j] = jnp.where(pop, below, x[j])
    return [jnp.concatenate(o, axis=0) for o in out]


def _top16_rows(s):
    iota8 = lax.broadcasted_iota(jnp.int32, (8, s.shape[1]), 0)
    nslab = s.shape[0] // 8
    return _top16_of_slabs([s[8 * i:8 * i + 8] for i in range(nslab)], [iota8 + 8 * i for i in range(nslab)])


def _gelu(x):
    return 0.5 * x * (1.0 + lax.erf(x * np.float32(0.7071067811865476)))


def _pair_top16(sv0, si0, sv1, si1):
    t = sv0.shape[1]
    iota8 = lax.broadcasted_iota(jnp.int32, (8, t), 0)
    pv = [sv0[0:1] + sv1, sv0[1:2] + sv1[0:8]]
    pe = [si0[0:1] * NK + si1, si0[1:2] * NK + si1[0:8]]
    for p, n in ((2, 5), (3, 4), (4, 3), (5, 2), (6, 2), (7, 2)):
        pv.append(jnp.where(iota8 < n, sv0[p:p + 1] + sv1[0:8], NEG))
        pe.append(si0[p:p + 1] * NK + si1[0:8])
    pv.append(sv0[8:16] + sv1[0:1])
    pe.append(si0[8:16] * NK + si1[0:1])
    slabs_v = [pv[0][0:8], pv[0][8:16]] + pv[1:]
    slabs_e = [pe[0][0:8], pe[0][8:16]] + pe[1:]
    slabs_r = [iota8 + 8 * j for j in range(len(slabs_v))]
    best, _, experts = _top16_of_slabs(slabs_v, slabs_r, slabs_e)
    ex = jnp.exp(best - best[0:1])
    return experts, ex / jnp.sum(ex, axis=0, keepdims=True)


def _peer_body(hn_ref, nw_ref, wqt_ref, keys_ref, ut_ref, v_ref, o_ref,
               xnt_nxt, et_nxt, gt_nxt, xn_cur, e_cur, g_cur, s_scr, sv0_scr, si0_scr,
               act_scr, sc0_scr, sc1_scr, stg0_scr, stg1_scr, w3_scr, *, nblk, tile):
    i = pl.program_id(0)
    k = pl.program_id(1)
    eb = ut_ref.shape[1]
    cpb = eb // NK
    nsel = e_cur.shape[1]
    hb = eb // 2
    cph = cpb // 2

    def key_scores(c):
        q = jnp.dot(wqt_ref[c * NK:(c + 1) * NK, :], xnt_nxt[...], preferred_element_type=F32)
        return jnp.dot(keys_ref[c], q.astype(BF16), preferred_element_type=F32)

    def route_piece(c):
        v, idx = _top16_rows(s_scr[...])
        if c == 0:
            sv0_scr[...] = v
            si0_scr[...] = idx
        else:
            experts, gate = _pair_top16(sv0_scr[...], si0_scr[...], v, idx)
            r0 = pl.multiple_of((k // 2) * TOPK, TOPK)
            et_nxt[pl.ds(r0, TOPK), :] = experts
            gt_nxt[pl.ds(r0, TOPK), :] = gate
        s_scr[...] = key_scores(1 - c)

    @pl.when(k == 0)
    def _():
        @pl.when(i == 0)
        def _():
            xnt_nxt[...] = jnp.zeros_like(xnt_nxt)
            et_nxt[...] = jnp.zeros_like(et_nxt)
            gt_nxt[...] = jnp.zeros_like(gt_nxt)
            sc1_scr[...] = jnp.zeros_like(sc1_scr)

        xn_cur[...] = xnt_nxt[...].T
        e_cur[...] = et_nxt[...].T
        g_cur[...] = gt_nxt[...].T
        xn = _rms(hn_ref[...], nw_ref[...])
        xnt_nxt[...] = xn.astype(BF16).T
        s_scr[...] = key_scores(0)
        act_scr[...] = jnp.zeros_like(act_scr)

    def score_half(sc_ref, half):
        sc_ref[...] = jnp.dot(xn_cur[...], ut_ref[:, half * hb:(half + 1) * hb], preferred_element_type=F32)

    def pick_half(sc_ref, chunk0):
        e = e_cur[...]
        row = e >> NK_BITS
        col = e & (NK - 1)
        act = act_scr[...]
        for cc in range(cph):
            picked = jnp.take_along_axis(sc_ref[:, cc * NK:(cc + 1) * NK], col, axis=1)
            act = jnp.where(row == chunk0 + cc, picked, act)
        act_scr[...] = act

    def scatter_weights():
        act_scr[...] = g_cur[...] * _gelu(act_scr[...])
        iota = lax.broadcasted_iota(jnp.int32, (NK, nsel), 0).astype(F32).astype(BF16)
        one = jnp.ones((NK, nsel), BF16)
        zero = jnp.zeros((NK, nsel), BF16)

        def scatter_group(grp, stage_ref):
            t0 = pl.multiple_of(grp * BF16_ROWS, BF16_ROWS)
            e_rows = e_cur[pl.ds(t0, BF16_ROWS), :]
            i1_rows = (e_rows >> NK_BITS).astype(F32).astype(BF16)
            i2_rows = (e_rows & (NK - 1)).astype(F32).astype(BF16)
            w_rows = act_scr[pl.ds(t0, BF16_ROWS), :].astype(BF16)
            for j in range(BF16_ROWS):
                pm = jnp.where(iota == i1_rows[j:j + 1], one, zero)
                qm = jnp.where(iota == i2_rows[j:j + 1], jnp.broadcast_to(w_rows[j:j + 1], (NK, nsel)), zero)
                stage_ref[j] = lax.dot_general(pm, qm, (((1,), (1,)), ((), ())),
                                               preferred_element_type=F32).astype(BF16)

        def swap_group(grp, stage_ref):
            t0 = pl.multiple_of(grp * BF16_ROWS, BF16_ROWS)
            w3_scr[:, pl.ds(t0, BF16_ROWS), :] = jnp.swapaxes(stage_ref[...], 0, 1)

        ngrp = tile // BF16_ROWS
        scatter_group(0, stg0_scr)

        def pair(i2, carry):
            scatter_group(2 * i2 + 1, stg1_scr)
            swap_group(2 * i2, stg0_scr)
            scatter_group(2 * i2 + 2, stg0_scr)
            swap_group(2 * i2 + 1, stg1_scr)
            return carry

        lax.fori_loop(0, ngrp // 2 - 1, pair, 0, unroll=5)
        scatter_group(ngrp - 1, stg1_scr)
        swap_group(ngrp - 2, stg0_scr)
        swap_group(ngrp - 1, stg1_scr)

    def weights_times_values():
        kk = k - nblk
        parts = [w3_scr[kk * cpb + cc] for cc in range(cpb)]
        return jnp.dot(jnp.concatenate(parts, axis=1), v_ref[...], preferred_element_type=F32)

    for c in range(2):
        @pl.when(jnp.logical_and(k < nblk, k % 2 == c))
        def _(c=c):
            route_piece(c)
            score_half(sc0_scr, 0)
            pick_half(sc1_scr, (k - 1) * cpb + cph)
            score_half(sc1_scr, 1)
            pick_half(sc0_scr, k * cpb)

    @pl.when(k == nblk)
    def _():
        pick_half(sc1_scr, (k - 1) * cpb + cph)
        scatter_weights()
        route_piece(nblk % 2)
        o_ref[...] = weights_times_values()

    for c in range(2):
        @pl.when(jnp.logical_and(k > nblk, k % 2 == c))
        def _(c=c):
            route_piece(c)
            o_ref[...] += weights_times_values()


def peer_mixer(h2d, norm_w, wqt, keys_hc, ut, v, *, tile, eb):
    n, d = h2d.shape
    heads = keys_hc.shape[0] // 2
    nsel = heads * TOPK
    ne = ut.shape[1]
    nblk = ne // eb
    assert nblk == heads, "one half-head is routed per grid step, so steps per tile = 2 * heads"
    nt = n // tile
    return pl.pallas_call(
        functools.partial(_peer_body, nblk=nblk, tile=tile),
        grid=(nt + 1, 2 * nblk),
        in_specs=[pl.BlockSpec((tile, d), lambda i, k: (jnp.minimum(i, nt - 1), 0)),
                  pl.BlockSpec((1, d), lambda i, k: (0, 0)),
                  pl.BlockSpec((2 * NK, d), lambda i, k: (jnp.minimum((k + 1) // 2, heads - 1), 0)),
                  pl.BlockSpec((2, NK, NK), lambda i, k: (jnp.minimum((k + 1) // 2, heads - 1), 0, 0)),
                  pl.BlockSpec((d, eb), lambda i, k: (0, jnp.minimum(k, nblk - 1))),
                  pl.BlockSpec((eb, d), lambda i, k: (jnp.maximum(k - nblk, 0), 0))],
        out_specs=pl.BlockSpec((tile, d), lambda i, k: (jnp.maximum(i - 1, 0), 0)),
        out_shape=jax.ShapeDtypeStruct((n, d), F32),
        scratch_shapes=[pltpu.VMEM((d, tile), BF16),
                        pltpu.VMEM((nsel, tile), jnp.int32),
                        pltpu.VMEM((nsel, tile), F32),
                        pltpu.VMEM((tile, d), BF16),
                        pltpu.VMEM((tile, nsel), jnp.int32),
                        pltpu.VMEM((tile, nsel), F32),
                        pltpu.VMEM((NK, tile), F32),
                        pltpu.VMEM((TOPK, tile), F32),
                        pltpu.VMEM((TOPK, tile), jnp.int32),
                        pltpu.VMEM((tile, nsel), F32),
                        pltpu.VMEM((tile, eb // 2), F32),
                        pltpu.VMEM((tile, eb // 2), F32),
                        pltpu.VMEM((BF16_ROWS, NK, NK), BF16),
                        pltpu.VMEM((BF16_ROWS, NK, NK), BF16),
                        pltpu.VMEM((ne // NK, tile, NK), BF16)],
        compiler_params=pltpu.CompilerParams(dimension_semantics=("arbitrary", "arbitrary"),
                                             vmem_limit_bytes=56 * 1024 * 1024),
        name="peer_mixer",
    )(h2d, norm_w.reshape(1, d), wqt, keys_hc, ut, v)


def _ple_body(h_ref, dpeer_ref, p_ref, nw_ref, wg_ref, wp_ref, fw_ref, o_ref):
    h = h_ref[...] + dpeer_ref[...]
    gate = _sigmoid(jnp.dot(_rms(h, nw_ref[...]).astype(BF16), wg_ref[...], preferred_element_type=F32))
    h = h + gate * jnp.dot(p_ref[...].astype(BF16), wp_ref[...], preferred_element_type=F32)
    o_ref[...] = _rms(h, fw_ref[...])


def ple_final(h2d, dpeer, p2d, norm_w, w_gate, w_proj, final_w, *, tm):
    n, d = h2d.shape
    pd = p2d.shape[1]
    return pl.pallas_call(
        _ple_body,
        grid=(n // tm,),
        in_specs=[pl.BlockSpec((tm, d), lambda i: (i, 0)),
                  pl.BlockSpec((tm, d), lambda i: (i, 0)),
                  pl.BlockSpec((tm, pd), lambda i: (i, 0)),
                  pl.BlockSpec((1, d), lambda i: (0, 0)),
                  pl.BlockSpec((d, d), lambda i: (0, 0)),
                  pl.BlockSpec((pd, d), lambda i: (0, 0)),
                  pl.BlockSpec((1, d), lambda i: (0, 0))],
        out_specs=pl.BlockSpec((tm, d), lambda i: (i, 0)),
        out_shape=jax.ShapeDtypeStruct((n, d), F32),
        compiler_params=pltpu.CompilerParams(dimension_semantics=("arbitrary",),
                                             vmem_limit_bytes=40 * 1024 * 1024),
        name="ple_final",
    )(h2d, dpeer, p2d, norm_w.reshape(1, d), w_gate, w_proj, final_w.reshape(1, d))


def kernel(x, p, norm_mix_w, w_in, conv_ssd_w, conv_ssd_b, dt_bias, a_log, d_skip,
           ssd_norm_w, w_ssd_out, conv_dw_w, conv_dw_b, conv_ln_w, conv_ln_b,
           w_conv_out, b_conv_out, w_o, norm_ffn_w, peer_wq, peer_keys, peer_u, peer_v,
           norm_ple_w, w_ple_gate, w_ple_proj, final_norm_w):
    bsz, s, d = x.shape
    x2d = x.reshape(bsz * s, d)
    i = 0
    r1 = lambda v: v.reshape(1, -1)

    col_xbc = D_INNER + conv_ssd_w.shape[2]
    col_dt = col_xbc + SSD_HEADS
    w_main = jnp.concatenate([w_in[i][:, :col_xbc], w_in[i][:, col_dt:]], axis=1).astype(BF16)
    w_dt = jnp.pad(w_in[i][:, col_xbc:col_dt], ((0, 0), (0, LANES - SSD_HEADS))).astype(BF16)
    b_dt = jnp.pad(dt_bias[i], (0, LANES - SSD_HEADS)).reshape(1, LANES)
    a_pad = jnp.pad(-jnp.exp(a_log[i]), (0, LANES - SSD_HEADS)).reshape(1, LANES)
    dskip_x = jnp.repeat(d_skip[i], SSD_HEAD_DIM).reshape(1, D_INNER)
    expand = (jnp.arange(D_INNER)[None, :] // SSD_HEAD_DIM == jnp.arange(LANES)[:, None]).astype(BF16)

    proj, dt = in_projection(x2d, norm_mix_w[i], w_main, w_dt, b_dt, tm=INPROJ_TM, tn=INPROJ_TN)
    h2d = token_mixers(x2d, proj, dt, conv_ssd_w[i], r1(conv_ssd_b[i]), a_pad, expand, dskip_x, r1(ssd_norm_w[i]),
                       w_ssd_out[i].astype(BF16), conv_dw_w[i], r1(conv_dw_b[i]), r1(conv_ln_w[i]),
                       r1(conv_ln_b[i]), w_conv_out[i].astype(BF16), r1(b_conv_out[i]), w_o[i].astype(BF16),
                       batch=bsz, L=MIX_L)

    wqt = peer_wq[i].T.astype(BF16)
    keys_hc = peer_keys[i].reshape(PEER_HEADS * 2, PEER_N_KEYS, PEER_HALF).astype(BF16)
    dpeer = peer_mixer(h2d, norm_ffn_w[i], wqt, keys_hc, peer_u[i].T.astype(BF16), peer_v[i].astype(BF16),
                       tile=PEER_TILE, eb=PEER_EXPERT_BLOCK)

    out = ple_final(h2d, dpeer, p[i].reshape(bsz * s, -1), norm_ple_w[i], w_ple_gate[i].astype(BF16),
                    w_ple_proj[i].astype(BF16), final_norm_w, tm=PLE_TM)
    return out.reshape(bsz, s, d)
```
